```python
import jax, jax.numpy as jnp
from jax import lax
import numpy as np

D_MODEL = 1024
BATCH = 8
SEQ = 4096
DEPTH = 4

N_MIXERS = 2
FOURIER_GROUPS = 4
FOURIER_GROUP_DIM = D_MODEL // FOURIER_GROUPS
RET_HEADS = 4
RET_QK_DIM = D_MODEL // RET_HEADS
RET_V_DIM = 2 * RET_QK_DIM
RET_CHUNK = 128
ROPE_BASE = 10000.0
D_FF = -(-8 * D_MODEL // (3 * 256)) * 256
EPS = 1e-6

kernel_name = "hybrid_fourier_retention_adaln_encoder"


def rmsnorm(x, g):
    xf = x.astype(jnp.float32)
    y = xf * lax.rsqrt(jnp.mean(xf * xf, axis=-1, keepdims=True) + EPS)
    return (y * g.astype(jnp.float32)).astype(x.dtype)


def modulate(h, shift, scale):
    return h * (1.0 + scale[:, None, :]) + shift[:, None, :]


def fourier_mixer(h, w_o):
    B, S, D = h.shape
    hg = h.astype(jnp.float32).reshape(B, S, FOURIER_GROUPS, FOURIER_GROUP_DIM)
    f = jnp.fft.fftn(hg, axes=(1, 3), norm="ortho").real.astype(h.dtype)
    return f.reshape(B, S, D) @ w_o


def rotary(x, pos):
    half = x.shape[-1] // 2
    inv_freq = ROPE_BASE ** (-jnp.arange(half, dtype=jnp.float32) / half)
    ang = pos[:, None] * inv_freq[None, :]
    cos, sin = jnp.cos(ang), jnp.sin(ang)
    x1, x2 = x[..., :half], x[..., half:]
    return jnp.concatenate([x1 * cos - x2 * sin, x1 * sin + x2 * cos], axis=-1)


def chunk_retention(q, k, v, log_gamma, inclusive):
    B, H, S, dk = q.shape
    dv = v.shape[-1]
    C = RET_CHUNK
    N = S // C
    to_chunks = lambda a: jnp.moveaxis(a.reshape(B, H, N, C, a.shape[-1]), 2, 0)
    qc, kc, vc = to_chunks(q), to_chunks(k), to_chunks(v)

    t = jnp.arange(C)
    diff = t[:, None] - t[None, :]
    mask = (diff >= 0) if inclusive else (diff > 0)
    lg = log_gamma.astype(jnp.float32)[:, None, None]
    dmask = jnp.where(mask, jnp.exp(jnp.where(mask, diff, 0).astype(jnp.float32) * lg), 0.0)
    tf = t.astype(jnp.float32)[None, :, None]
    q_decay = jnp.exp((tf + 1.0) * lg)
    k_decay = jnp.exp((C - 1.0 - tf) * lg)
    chunk_decay = jnp.exp(C * lg)

    def step(R, xs):
        qb, kb, vb = xs
        scores = jnp.einsum('bhtd,bhsd->bhts', qb, kb) * dmask
        inner = jnp.einsum('bhts,bhsv->bhtv', scores, vb)
        cross = jnp.einsum('bhtd,bhdv->bhtv', qb, R) * q_decay
        R = R * chunk_decay + jnp.einsum('bhsd,bhsv->bhdv', kb * k_decay, vb)
        return R, inner + cross

    R0 = jnp.zeros((B, H, dk, dv), jnp.float32)
    _, out = lax.scan(step, R0, (qc, kc, vc))
    return jnp.moveaxis(out, 0, 2).reshape(B, H, S, dv)


def retention_mixer(h, w_in, w_out, log_g_fwd, log_g_bwd):
    B, S, _ = h.shape
    H, dk, dv = RET_HEADS, RET_QK_DIM, RET_V_DIM
    proj = h @ w_in
    q, k, v, g = jnp.split(proj, [H * dk, 2 * H * dk, 2 * H * dk + H * dv], axis=-1)
    heads = lambda a, d: a.astype(jnp.float32).reshape(B, S, H, d).transpose(0, 2, 1, 3)
    pos = jnp.arange(S, dtype=jnp.float32)
    q = rotary(heads(q, dk), pos)
    k = rotary(heads(k, dk), pos) * (dk ** -0.5)
    v = heads(v, dv)
    y_fwd = chunk_retention(q, k, v, log_g_fwd, True)
    flip = lambda a: jnp.flip(a, axis=2)
    y_bwd = flip(chunk_retention(flip(q), flip(k), flip(v), log_g_bwd, False))
    y = y_fwd + y_bwd
    mu = jnp.mean(y, axis=-1, keepdims=True)
    var = jnp.mean(jnp.square(y - mu), axis=-1, keepdims=True)
    y = (y - mu) * lax.rsqrt(var + EPS)
    y = y.transpose(0, 2, 1, 3).reshape(B, S, H * dv).astype(h.dtype)
    return (jax.nn.silu(g) * y) @ w_out


def swiglu(h, w_in, w_out):
    gate, up = jnp.split(h @ w_in, 2, axis=-1)
    return (jax.nn.silu(gate) * up) @ w_out


def setup_inputs(seed: int = 0) -> dict:
    key = jax.random.key(seed)
    ks = jax.random.split(key, 14)
    D, F = D_MODEL, D_FF
    n_four = len(range(0, DEPTH, N_MIXERS))
    n_ret = len(range(1, DEPTH, N_MIXERS))
    ret_in_w = 2 * RET_HEADS * RET_QK_DIM + 2 * RET_HEADS * RET_V_DIM
    ret_v_w = RET_HEADS * RET_V_DIM
    nrm = lambda k, shape, s: jax.random.normal(k, shape, jnp.float32) * s
    return {
        "x": nrm(ks[0], (BATCH, SEQ, D), 1.0),
        "c": nrm(ks[1], (BATCH, D), 1.0),
        "w_ada": nrm(ks[2], (DEPTH, D, 6 * D), 0.5 * D ** -0.5),
        "b_ada": nrm(ks[3], (DEPTH, 6 * D), 0.01),
        "norm_mix_g": 1.0 + nrm(ks[4], (DEPTH, D), 0.02),
        "norm_ffn_g": 1.0 + nrm(ks[5], (DEPTH, D), 0.02),
        "w_fourier_out": nrm(ks[6], (n_four, D, D), D ** -0.5),
        "w_ret_in": nrm(ks[7], (n_ret, D, ret_in_w), D ** -0.5),
        "w_ret_out": nrm(ks[8], (n_ret, ret_v_w, D), ret_v_w ** -0.5),
        "w_ffn_in": nrm(ks[9], (DEPTH, D, 2 * F), D ** -0.5),
        "w_ffn_out": nrm(ks[10], (DEPTH, F, D), F ** -0.5),
        "final_norm_g": 1.0 + nrm(ks[11], (D,), 0.02),
        "w_ada_final": nrm(ks[12], (D, 2 * D), 0.5 * D ** -0.5),
        "b_ada_final": nrm(ks[13], (2 * D,), 0.01),
    }


def reference(x, c, w_ada, b_ada, norm_mix_g, norm_ffn_g, w_fourier_out, w_ret_in, w_ret_out,
              w_ffn_in, w_ffn_out, final_norm_g, w_ada_final, b_ada_final):
    hidx = jnp.arange(RET_HEADS, dtype=jnp.float32)
    log_g_fwd = jnp.log1p(-jnp.exp2(-5.0 - hidx))
    log_g_bwd = jnp.flip(log_g_fwd)
    c_act = jax.nn.silu(c)
    for i in range(DEPTH):
        mod = c_act @ w_ada[i] + b_ada[i]
        sh1, sc1, g1, sh2, sc2, g2 = jnp.split(mod, 6, axis=-1)
        h = modulate(rmsnorm(x, norm_mix_g[i]), sh1, sc1)
        j = i // N_MIXERS
        if i % N_MIXERS == 0:
            m = fourier_mixer(h, w_fourier_out[j])
        else:
            m = retention_mixer(h, w_ret_in[j], w_ret_out[j], log_g_fwd, log_g_bwd)
        x = x + g1[:, None, :] * m
        h = modulate(rmsnorm(x, norm_ffn_g[i]), sh2, sc2)
        x = x + g2[:, None, :] * swiglu(h, w_ffn_in[i], w_ffn_out[i])
    shf, scf = jnp.split(c_act @ w_ada_final + b_ada_final, 2, axis=-1)
    return modulate(rmsnorm(x, final_norm_g), shf, scf)
```

```python
import functools
import math

import jax
import jax.numpy as jnp
from jax import lax
from jax.experimental import pallas as pl
from jax.experimental.pallas import tpu as pltpu

EPS = 1e-6
N_MIXERS = 2
FOURIER_GROUPS = 4
RET_HEADS = 4
ROPE_BASE = 10000.0

DFT_RADIX = 16
RET_CHUNK = 256
FFN_CHUNK = 256
LANES = 128
ROW_TILE = 512
VMEM_LIMIT_BYTES = 56 * 1024 * 1024

F32 = jnp.float32
BF16 = jnp.bfloat16


def _params(n_grid_dims):
    return pltpu.CompilerParams(
        dimension_semantics=("arbitrary",) * n_grid_dims,
        vmem_limit_bytes=VMEM_LIMIT_BYTES)


def _resident(block_shape, index_map):
    return pl.BlockSpec(block_shape, index_map, pipeline_mode=pl.Buffered(1))


def _silu(v):
    return v * jax.nn.sigmoid(v)


def _norm_mod(x, g, shift, scale):
    ms = jnp.mean(x * x, axis=-1, keepdims=True)
    y = x * lax.rsqrt(ms + EPS)
    return (y * g) * (1.0 + scale) + shift


def _ada_kernel(c_ref, w_ref, b_ref, o_ref):
    ca = _silu(c_ref[...])
    o_ref[...] = jnp.dot(ca, w_ref[...], preferred_element_type=F32,
                         precision=lax.Precision.HIGHEST) + b_ref[...]


def _ada(c, w, b, tn=1024):
    n_layers, d, n = w.shape
    bsz = c.shape[0]
    tn = min(tn, n)
    return pl.pallas_call(
        _ada_kernel,
        grid=(n_layers, n // tn),
        in_specs=[
            pl.BlockSpec((bsz, d), lambda l, j: (0, 0)),
            pl.BlockSpec((None, d, tn), lambda l, j: (l, 0, j)),
            pl.BlockSpec((None, 1, tn), lambda l, j: (l, 0, j)),
        ],
        out_specs=pl.BlockSpec((None, bsz, tn), lambda l, j: (l, 0, j)),
        out_shape=jax.ShapeDtypeStruct((n_layers, bsz, n), F32),
        compiler_params=_params(2),
        name="ada_mod",
    )(c, w, b.reshape(n_layers, 1, n))


def _prenorm_kernel(x_ref, g_ref, sh_ref, sc_ref, o_ref, *scratch, residue_major):
    h = _norm_mod(x_ref[...], g_ref[...], sh_ref[...], sc_ref[...])
    if not residue_major:
        o_ref[...] = h.astype(o_ref.dtype)
        return
    tmp_ref, = scratch
    rows = tmp_ref.shape[1] // DFT_RADIX
    for l in range(tmp_ref.shape[0]):
        tmp_ref[l] = h[:, l * LANES:(l + 1) * LANES]
    for l in range(tmp_ref.shape[0]):
        for s1 in range(DFT_RADIX):
            o_ref[s1, :, l * LANES:(l + 1) * LANES] = (
                tmp_ref[l, pl.ds(s1, rows, stride=DFT_RADIX), :].astype(o_ref.dtype))


def _prenorm(x, g, shift, scale, *, residue_major, out_dtype, tm=ROW_TILE):
    bsz, s, d = x.shape
    tm = min(tm, s)
    vec = lambda a: a.reshape(bsz, 1, d)
    in_specs = [
        pl.BlockSpec((None, tm, d), lambda b, i: (b, i, 0)),
        pl.BlockSpec((1, d), lambda b, i: (0, 0)),
        pl.BlockSpec((None, 1, d), lambda b, i: (b, 0, 0)),
        pl.BlockSpec((None, 1, d), lambda b, i: (b, 0, 0)),
    ]
    if residue_major:
        out_shape = jax.ShapeDtypeStruct((bsz, DFT_RADIX, s // DFT_RADIX, d), out_dtype)
        out_spec = pl.BlockSpec((None, DFT_RADIX, tm // DFT_RADIX, d), lambda b, i: (b, 0, i, 0))
        scratch = [pltpu.VMEM((d // LANES, tm, LANES), F32)]
    else:
        out_shape = jax.ShapeDtypeStruct((bsz, s, d), out_dtype)
        out_spec = pl.BlockSpec((None, tm, d), lambda b, i: (b, i, 0))
        scratch = []
    return pl.pallas_call(
        functools.partial(_prenorm_kernel, residue_major=residue_major),
        grid=(bsz, s // tm),
        in_specs=in_specs,
        out_specs=out_spec,
        out_shape=out_shape,
        scratch_shapes=scratch,
        compiler_params=_params(2),
        name="prenorm_rm" if residue_major else "prenorm",
    )(x, g.reshape(1, d), vec(shift), vec(scale))


def _dft4(x):
    (x0r, x0i), (x1r, x1i), (x2r, x2i), (x3r, x3i) = x
    t0r, t0i = x0r + x2r, x0i + x2i
    t1r, t1i = x0r - x2r, x0i - x2i
    t2r, t2i = x1r + x3r, x1i + x3i
    t3r, t3i = x1r - x3r, x1i - x3i
    return [(t0r + t2r, t0i + t2i),
            (t1r + t3i, t1i - t3r),
            (t0r - t2r, t0i - t2i),
            (t1r - t3i, t1i + t3r)]


def _twiddle16(z, p):
    zr, zi = z
    p = p % 16
    if p == 0:
        return zr, zi
    if p == 4:
        return zi, -zr
    if p == 8:
        return -zr, -zi
    if p == 12:
        return -zi, zr
    cr = math.cos(2.0 * math.pi * p / 16.0)
    ci = -math.sin(2.0 * math.pi * p / 16.0)
    return zr * cr - zi * ci, zr * ci + zi * cr


def _seqdft_kernel(h_ref, w_ref, yr_ref, yi_ref, br_ref, bi_ref):
    n2 = h_ref.shape[1]
    td = h_ref.shape[2]
    rows = 16
    for s1 in range(DFT_RADIX):
        p = jnp.dot(w_ref[s1], h_ref[s1], preferred_element_type=F32)
        br_ref[s1] = p[:n2]
        bi_ref[s1] = p[n2:]

    def pass1(it, carry):
        r = pl.multiple_of(it * rows, rows)
        for l0 in range(0, td, LANES):
            for b in range(4):
                xs = [(br_ref[4 * a + b, pl.ds(r, rows), l0:l0 + LANES],
                       bi_ref[4 * a + b, pl.ds(r, rows), l0:l0 + LANES]) for a in range(4)]
                us = _dft4(xs)
                for c in range(4):
                    ur, ui = _twiddle16(us[c], b * c)
                    br_ref[4 * c + b, pl.ds(r, rows), l0:l0 + LANES] = ur
                    bi_ref[4 * c + b, pl.ds(r, rows), l0:l0 + LANES] = ui
        return carry

    def pass2(it, carry):
        r = pl.multiple_of(it * rows, rows)
        for l0 in range(0, td, LANES):
            for c in range(4):
                us = [(br_ref[4 * c + b, pl.ds(r, rows), l0:l0 + LANES],
                       bi_ref[4 * c + b, pl.ds(r, rows), l0:l0 + LANES]) for b in range(4)]
                ys = _dft4(us)
                for d in range(4):
                    k1 = c + 4 * d
                    yr_ref[pl.ds(k1 * n2 + r, rows), l0:l0 + LANES] = ys[d][0].astype(yr_ref.dtype)
                    yi_ref[pl.ds(k1 * n2 + r, rows), l0:l0 + LANES] = ys[d][1].astype(yi_ref.dtype)
        return carry

    lax.fori_loop(0, n2 // rows, pass1, 0)
    lax.fori_loop(0, n2 // rows, pass2, 0)


def _seqdft_tables(s):
    n2 = s // DFT_RADIX
    s1 = jnp.arange(DFT_RADIX, dtype=jnp.int32)[:, None, None]
    k2 = jnp.arange(n2, dtype=jnp.int32)[None, :, None]
    s2 = jnp.arange(n2, dtype=jnp.int32)[None, None, :]
    j = (k2 * (DFT_RADIX * s2 + s1)) % s
    ang = j.astype(F32) * (2.0 * jnp.pi / s)
    return jnp.concatenate([jnp.cos(ang), -jnp.sin(ang)], axis=1).astype(BF16)


def _seqdft(h_rm, w_tab, td=256):
    bsz, _, n2, d = h_rm.shape
    s = DFT_RADIX * n2
    td = min(td, d)
    out = jax.ShapeDtypeStruct((bsz, s, d), BF16)
    return pl.pallas_call(
        _seqdft_kernel,
        grid=(bsz, d // td),
        in_specs=[
            pl.BlockSpec((None, DFT_RADIX, n2, td), lambda b, j: (b, 0, 0, j)),
            _resident((DFT_RADIX, 2 * n2, n2), lambda b, j: (0, 0, 0)),
        ],
        out_specs=[pl.BlockSpec((None, s, td), lambda b, j: (b, 0, j)),
                   pl.BlockSpec((None, s, td), lambda b, j: (b, 0, j))],
        out_shape=[out, out],
        scratch_shapes=[pltpu.VMEM((DFT_RADIX, n2, td), F32),
                        pltpu.VMEM((DFT_RADIX, n2, td), F32)],
        compiler_params=_params(2),
        name="seq_dft",
    )(h_rm, w_tab)


def _retproj_kernel(h_ref, wq_ref, wkt_ref, wv_ref, wg_ref, cos_ref, sin_ref, cost_ref, sint_ref,
                    q_ref, kt_ref, v_ref, sg_ref, *, dk, chunk):
    h = h_ref[...]
    half = dk // 2
    tm = h.shape[0]
    cos, sin = cos_ref[...], sin_ref[...]
    cost, sint = cost_ref[...], sint_ref[...]
    k_scale = dk ** -0.5
    for hd in range(RET_HEADS):
        c0 = hd * dk
        q = jnp.dot(h, wq_ref[:, c0:c0 + dk], preferred_element_type=F32)
        q1, q2 = q[:, :half], q[:, half:]
        q_ref[:, c0:c0 + half] = (q1 * cos - q2 * sin).astype(q_ref.dtype)
        q_ref[:, c0 + half:c0 + dk] = (q1 * sin + q2 * cos).astype(q_ref.dtype)
        kt = lax.dot_general(wkt_ref[c0:c0 + dk, :], h, (((1,), (1,)), ((), ())),
                             preferred_element_type=F32)
        k1, k2 = kt[:half], kt[half:]
        r1 = (k1 * cost - k2 * sint) * k_scale
        r2 = (k1 * sint + k2 * cost) * k_scale
        for j in range(tm // chunk):
            kt_ref[j, c0:c0 + half, :] = r1[:, j * chunk:(j + 1) * chunk].astype(kt_ref.dtype)
            kt_ref[j, c0 + half:c0 + dk, :] = r2[:, j * chunk:(j + 1) * chunk].astype(kt_ref.dtype)
    dvt = wv_ref.shape[1]
    for c0 in range(0, dvt, 256):
        c1 = min(c0 + 256, dvt)
        v_ref[:, c0:c1] = jnp.dot(h, wv_ref[:, c0:c1], preferred_element_type=F32).astype(v_ref.dtype)
        g = jnp.dot(h, wg_ref[:, c0:c1], preferred_element_type=F32)
        sg_ref[:, c0:c1] = _silu(g).astype(sg_ref.dtype)


def _retproj(h, wq, wkt, wv, wg, cos, sin, chunk, tm=ROW_TILE):
    bsz, s, d = h.shape
    hdk = wq.shape[1]
    hdv = wv.shape[1]
    dk = hdk // RET_HEADS
    half = dk // 2
    tm = min(tm, s)
    row = lambda b, i: (b, i, 0)
    const = lambda b, i: (0, 0)
    return pl.pallas_call(
        functools.partial(_retproj_kernel, dk=dk, chunk=chunk),
        grid=(bsz, s // tm),
        in_specs=[
            pl.BlockSpec((None, tm, d), row),
            _resident((d, hdk), const),
            _resident((hdk, d), const),
            _resident((d, hdv), const),
            _resident((d, hdv), const),
            pl.BlockSpec((tm, half), lambda b, i: (i, 0)),
            pl.BlockSpec((tm, half), lambda b, i: (i, 0)),
            pl.BlockSpec((half, tm), lambda b, i: (0, i)),
            pl.BlockSpec((half, tm), lambda b, i: (0, i)),
        ],
        out_specs=[
            pl.BlockSpec((None, tm, hdk), row),
            pl.BlockSpec((None, tm // chunk, hdk, chunk), lambda b, i: (b, i, 0, 0)),
            pl.BlockSpec((None, tm, hdv), row),
            pl.BlockSpec((None, tm, hdv), row),
        ],
        out_shape=[
            jax.ShapeDtypeStruct((bsz, s, hdk), BF16),
            jax.ShapeDtypeStruct((bsz, s // chunk, hdk, chunk), BF16),
            jax.ShapeDtypeStruct((bsz, s, hdv), BF16),
            jax.ShapeDtypeStruct((bsz, s, hdv), BF16),
        ],
        compiler_params=_params(2),
        name="ret_proj",
    )(h, wq, wkt, wv, wg, cos, sin, cos.T, sin.T)


def _retcore_kernel(cd_ref, q_ref, kt_ref, v_ref, dm_ref, qdf_ref, qdb_ref, kdf_ref, kdb_ref,
                    y_ref, rb_all_ref, rf_ref, rb_ref):
    hd = pl.program_id(1)
    nc, dk, c = kt_ref.shape
    dv = v_ref.shape[1]
    cd_f = cd_ref[0, hd]
    cd_b = cd_ref[1, hd]
    kdf = kdf_ref[...]
    kdb = kdb_ref[...]

    rb_ref[...] = jnp.zeros_like(rb_ref)

    def bwd(it, carry):
        n = nc - 1 - it
        rows = pl.ds(pl.multiple_of(n * c, c), c)
        rb_all_ref[n] = rb_ref[...].astype(rb_all_ref.dtype)
        ktb = (kt_ref[n].astype(F32) * kdb).astype(BF16)
        rb_ref[...] = rb_ref[...] * cd_b + jnp.dot(ktb, v_ref[rows, :], preferred_element_type=F32)
        return carry

    lax.fori_loop(0, nc, bwd, 0)

    rf_ref[...] = jnp.zeros_like(rf_ref)
    reps = dk // LANES

    def fwd(n, carry):
        rows = pl.ds(pl.multiple_of(n * c, c), c)
        qn = q_ref[rows, :]
        ktn = kt_ref[n]
        vn = v_ref[rows, :]
        sc = jnp.dot(qn, ktn, preferred_element_type=F32) * dm_ref[...]
        y = jnp.dot(sc.astype(BF16), vn, preferred_element_type=F32)
        qf32 = qn.astype(F32)
        qf = (qf32 * jnp.tile(qdf_ref[...], (1, reps))).astype(BF16)
        qb = (qf32 * jnp.tile(qdb_ref[...], (1, reps))).astype(BF16)
        y += jnp.dot(qf, rf_ref[...].astype(BF16), preferred_element_type=F32)
        y += jnp.dot(qb, rb_all_ref[n], preferred_element_type=F32)
        mu = jnp.mean(y, axis=-1, keepdims=True)
        yc = y - mu
        var = jnp.mean(yc * yc, axis=-1, keepdims=True)
        y_ref[rows, :] = (yc * lax.rsqrt(var + EPS)).astype(y_ref.dtype)
        ktf = (ktn.astype(F32) * kdf).astype(BF16)
        rf_ref[...] = rf_ref[...] * cd_f + jnp.dot(ktf, vn, preferred_element_type=F32)
        return carry

    lax.fori_loop(0, nc, fwd, 0)


def _ret_decay_tables(c):
    hidx = jnp.arange(RET_HEADS, dtype=F32)
    lg_f = jnp.log1p(-jnp.exp2(-5.0 - hidx))
    lg_b = jnp.flip(lg_f)
    t = jnp.arange(c, dtype=jnp.int32)
    diff = (t[:, None] - t[None, :])
    fmask = diff >= 0
    bmask = diff < 0
    ef = jnp.exp(jnp.where(fmask, diff, 0).astype(F32)[None] * lg_f[:, None, None])
    eb = jnp.exp(jnp.where(bmask, -diff, 0).astype(F32)[None] * lg_b[:, None, None])
    dm = jnp.where(fmask[None], ef, eb)
    tf = t.astype(F32)
    ones = jnp.ones((1, 1, LANES), F32)
    qdf = jnp.exp((tf + 1.0)[None, :] * lg_f[:, None])[:, :, None] * ones
    qdb = jnp.exp((c - tf)[None, :] * lg_b[:, None])[:, :, None] * ones
    kdf = jnp.exp((c - 1.0 - tf)[None, :] * lg_f[:, None])[:, None, :]
    kdb = jnp.exp(tf[None, :] * lg_b[:, None])[:, None, :]
    cd = jnp.stack([jnp.exp(c * lg_f), jnp.exp(c * lg_b)])
    return cd, dm, qdf, qdb, kdf, kdb


def _retcore(q, kt, v, tables):
    bsz, s, hdk = q.shape
    hdv = v.shape[2]
    dk, dv = hdk // RET_HEADS, hdv // RET_HEADS
    nc, c = kt.shape[1], kt.shape[3]
    cd, dm, qdf, qdb, kdf, kdb = tables
    per_head = lambda b, h: (h, 0, 0)
    return pl.pallas_call(
        _retcore_kernel,
        grid=(bsz, RET_HEADS),
        in_specs=[
            pl.BlockSpec(memory_space=pltpu.SMEM),
            pl.BlockSpec((None, s, dk), lambda b, h: (b, 0, h)),
            pl.BlockSpec((None, nc, dk, c), lambda b, h: (b, 0, h, 0)),
            pl.BlockSpec((None, s, dv), lambda b, h: (b, 0, h)),
            pl.BlockSpec((None, c, c), per_head),
            pl.BlockSpec((None, c, LANES), per_head),
            pl.BlockSpec((None, c, LANES), per_head),
            pl.BlockSpec((None, 1, c), per_head),
            pl.BlockSpec((None, 1, c), per_head),
        ],
        out_specs=pl.BlockSpec((None, s, dv), lambda b, h: (b, 0, h)),
        out_shape=jax.ShapeDtypeStruct((bsz, s, hdv), BF16),
        scratch_shapes=[pltpu.VMEM((nc, dk, dv), BF16),
                        pltpu.VMEM((dk, dv), F32),
                        pltpu.VMEM((dk, dv), F32)],
        compiler_params=_params(2),
        name="ret_core",
    )(cd, q, kt, v, dm, qdf, qdb, kdf, kdb)


def _mix_ffn_kernel(*refs, fourier):
    if fourier:
        (x_ref, g1_ref, nf_ref, sh2_ref, sc2_ref, g2_ref, yr_ref, yi_ref, cc_ref, sc_ref, wo_ref,
         win_ref, wout_ref, o_ref, h2_ref, acc_ref) = refs
        gd = cc_ref.shape[0]
        fs = []
        for g0 in range(0, yr_ref.shape[1], gd):
            f = jnp.dot(yr_ref[:, g0:g0 + gd], cc_ref[...], preferred_element_type=F32)
            f += jnp.dot(yi_ref[:, g0:g0 + gd], sc_ref[...], preferred_element_type=F32)
            fs.append(f.astype(BF16))
        z = jnp.concatenate(fs, axis=1)
    else:
        (x_ref, g1_ref, nf_ref, sh2_ref, sc2_ref, g2_ref, y_ref, sg_ref, wo_ref,
         win_ref, wout_ref, o_ref, h2_ref, acc_ref) = refs
        z = (sg_ref[...].astype(F32) * y_ref[...].astype(F32)).astype(BF16)
    m = jnp.dot(z, wo_ref[...], preferred_element_type=F32)
    x1 = x_ref[...] + g1_ref[...] * m
    h2_ref[...] = _norm_mod(x1, nf_ref[...], sh2_ref[...], sc2_ref[...]).astype(h2_ref.dtype)
    o_ref[...] = x1
    acc_ref[...] = jnp.zeros_like(acc_ref)

    def step(j, carry):
        h2 = h2_ref[...]
        gate = jnp.dot(h2, win_ref[0, j], preferred_element_type=F32)
        up = jnp.dot(h2, win_ref[1, j], preferred_element_type=F32)
        a = (_silu(gate) * up).astype(BF16)
        acc_ref[...] += jnp.dot(a, wout_ref[j], preferred_element_type=F32)
        return carry

    lax.fori_loop(0, win_ref.shape[1], step, 0)
    o_ref[...] += g2_ref[...] * acc_ref[...]


def _mix_ffn(x, g1, nf, sh2, sc2, g2, mix_inputs, mix_consts, wo, win, wout, *, fourier, tm=ROW_TILE):
    bsz, s, d = x.shape
    tm = min(tm, s)
    row = lambda b, i: (b, i, 0)
    vec = lambda a: a.reshape(bsz, 1, d)
    vspec = pl.BlockSpec((None, 1, d), lambda b, i: (b, 0, 0))
    in_specs = [pl.BlockSpec((None, tm, d), row), vspec,
                pl.BlockSpec((1, d), lambda b, i: (0, 0)), vspec, vspec, vspec]
    args = [x, vec(g1), nf.reshape(1, d), vec(sh2), vec(sc2), vec(g2)]
    for a in mix_inputs:
        in_specs.append(pl.BlockSpec((None, tm, a.shape[2]), row))
        args.append(a)
    for a in tuple(mix_consts) + (wo, win, wout):
        nd = a.ndim
        in_specs.append(_resident(a.shape, lambda b, i, nd=nd: (0,) * nd))
        args.append(a)
    return pl.pallas_call(
        functools.partial(_mix_ffn_kernel, fourier=fourier),
        grid=(bsz, s // tm),
        in_specs=in_specs,
        out_specs=pl.BlockSpec((None, tm, d), row),
        out_shape=jax.ShapeDtypeStruct((bsz, s, d), F32),
        scratch_shapes=[pltpu.VMEM((tm, d), BF16), pltpu.VMEM((tm, d), F32)],
        compiler_params=_params(2),
        name="mix_ffn_fourier" if fourier else "mix_ffn_ret",
    )(*args)


def _chan_dft_tables(gd, s):
    c = jnp.arange(gd, dtype=jnp.int32)
    j = (c[:, None] * c[None, :]) % gd
    ang = j.astype(F32) * (2.0 * jnp.pi / gd)
    scale = 1.0 / jnp.sqrt(jnp.asarray(s * gd, F32))
    return (jnp.cos(ang) * scale).astype(BF16), (jnp.sin(ang) * scale).astype(BF16)


def _rotary_tables(s, half):
    inv_freq = ROPE_BASE ** (-jnp.arange(half, dtype=F32) / half)
    ang = jnp.arange(s, dtype=F32)[:, None] * inv_freq[None, :]
    return jnp.cos(ang), jnp.sin(ang)


def kernel(x, c, w_ada, b_ada, norm_mix_g, norm_ffn_g, w_fourier_out, w_ret_in, w_ret_out,
           w_ffn_in, w_ffn_out, final_norm_g, w_ada_final, b_ada_final):
    bsz, s, d = x.shape
    depth = w_ada.shape[0]
    f = w_ffn_out.shape[1]
    nf = f // FFN_CHUNK
    gd = d // FOURIER_GROUPS
    dk = d // RET_HEADS
    hdk = RET_HEADS * dk
    hdv = w_ret_out.shape[1]
    chunk = min(RET_CHUNK, s)

    mods = _ada(c, w_ada, b_ada)
    mod_final = _ada(c, w_ada_final[None], b_ada_final[None])[0]

    seq_tab = _seqdft_tables(s)
    cc_tab, sc_tab = _chan_dft_tables(gd, s)
    cos, sin = _rotary_tables(s, dk // 2)
    ret_tabs = _ret_decay_tables(chunk)

    win_all = w_ffn_in.astype(BF16).reshape(depth, d, 2, nf, FFN_CHUNK).transpose(0, 2, 3, 1, 4)
    wout_all = w_ffn_out.astype(BF16).reshape(depth, nf, FFN_CHUNK, d)

    for i in range(depth):
        sh1, sc1, g1, sh2, sc2, g2 = jnp.split(mods[i], 6, axis=-1)
        j = i // N_MIXERS
        if i % N_MIXERS == 0:
            h = _prenorm(x, norm_mix_g[i], sh1, sc1, residue_major=True, out_dtype=BF16)
            yr, yi = _seqdft(h, seq_tab)
            x = _mix_ffn(x, g1, norm_ffn_g[i], sh2, sc2, g2, (yr, yi), (cc_tab, sc_tab),
                         w_fourier_out[j].astype(BF16), win_all[i], wout_all[i], fourier=True)
        else:
            h = _prenorm(x, norm_mix_g[i], sh1, sc1, residue_major=False, out_dtype=BF16)
            w_in = w_ret_in[j].astype(BF16)
            wq = w_in[:, :hdk]
            wkt = w_in[:, hdk:2 * hdk].T
            wv = w_in[:, 2 * hdk:2 * hdk + hdv]
            wg = w_in[:, 2 * hdk + hdv:]
            q, kt, v, sg = _retproj(h, wq, wkt, wv, wg, cos, sin, chunk)
            y = _retcore(q, kt, v, ret_tabs)
            x = _mix_ffn(x, g1, norm_ffn_g[i], sh2, sc2, g2, (y, sg), (),
                         w_ret_out[j].astype(BF16), win_all[i], wout_all[i], fourier=False)
    shf, scf = jnp.split(mod_final, 2, axis=-1)
    return _prenorm(x, final_norm_g, shf, scf, residue_major=False, out_dtype=x.dtype)
```

```python
import functools
import math

import jax
import jax.numpy as jnp
from jax import lax
from jax.experimental import pallas as pl
from jax.experimental.pallas import tpu as pltpu

EPS = 1e-6
N_MIXERS = 2
FOURIER_GROUPS = 4
RET_HEADS = 4
ROPE_BASE = 10000.0

DFT_RADIX = 16
RET_CHUNK = 256
FFN_CHUNK = 256
LANES = 128
ROW_TILE = 512
VMEM_LIMIT_BYTES = 56 * 1024 * 1024

F32 = jnp.float32
BF16 = jnp.bfloat16


def _params(n_grid_dims):
    return pltpu.CompilerParams(
        dimension_semantics=("arbitrary",) * n_grid_dims,
        vmem_limit_bytes=VMEM_LIMIT_BYTES)


def _resident(block_shape, index_map):
    return pl.BlockSpec(block_shape, index_map, pipeline_mode=pl.Buffered(1))


def _silu(v):
    return v * jax.nn.sigmoid(v)


def _norm_mod(x, g, shift, scale):
    ms = jnp.mean(x * x, axis=-1, keepdims=True)
    y = x * lax.rsqrt(ms + EPS)
    return (y * g) * (1.0 + scale) + shift


def _ada_kernel(c_ref, w_ref, b_ref, o_ref):
    ca = _silu(c_ref[...])
    o_ref[...] = jnp.dot(ca, w_ref[...], preferred_element_type=F32,
                         precision=lax.Precision.HIGHEST) + b_ref[...]


def _ada(c, w, b, tn=1024):
    n_layers, d, n = w.shape
    bsz = c.shape[0]
    tn = min(tn, n)
    return pl.pallas_call(
        _ada_kernel,
        grid=(n_layers, n // tn),
        in_specs=[
            pl.BlockSpec((bsz, d), lambda l, j: (0, 0)),
            pl.BlockSpec((None, d, tn), lambda l, j: (l, 0, j)),
            pl.BlockSpec((None, 1, tn), lambda l, j: (l, 0, j)),
        ],
        out_specs=pl.BlockSpec((None, bsz, tn), lambda l, j: (l, 0, j)),
        out_shape=jax.ShapeDtypeStruct((n_layers, bsz, n), F32),
        compiler_params=_params(2),
        name="ada_mod",
    )(c, w, b.reshape(n_layers, 1, n))


def _prenorm_kernel(x_ref, g_ref, sh_ref, sc_ref, o_ref, *scratch, residue_major):
    h = _norm_mod(x_ref[...], g_ref[...], sh_ref[...], sc_ref[...])
    if not residue_major:
        o_ref[...] = h.astype(o_ref.dtype)
        return
    tmp_ref, = scratch
    rows = tmp_ref.shape[1] // DFT_RADIX
    for l in range(tmp_ref.shape[0]):
        tmp_ref[l] = h[:, l * LANES:(l + 1) * LANES]
    for l in range(tmp_ref.shape[0]):
        for s1 in range(DFT_RADIX):
            o_ref[s1, :, l * LANES:(l + 1) * LANES] = (
                tmp_ref[l, pl.ds(s1, rows, stride=DFT_RADIX), :].astype(o_ref.dtype))


def _prenorm(x, g, shift, scale, *, residue_major, out_dtype, tm=ROW_TILE):
    bsz, s, d = x.shape
    tm = min(tm, s)
    vec = lambda a: a.reshape(bsz, 1, d)
    in_specs = [
        pl.BlockSpec((None, tm, d), lambda b, i: (b, i, 0)),
        pl.BlockSpec((1, d), lambda b, i: (0, 0)),
        pl.BlockSpec((None, 1, d), lambda b, i: (b, 0, 0)),
        pl.BlockSpec((None, 1, d), lambda b, i: (b, 0, 0)),
    ]
    if residue_major:
        out_shape = jax.ShapeDtypeStruct((bsz, DFT_RADIX, s // DFT_RADIX, d), out_dtype)
        out_spec = pl.BlockSpec((None, DFT_RADIX, tm // DFT_RADIX, d), lambda b, i: (b, 0, i, 0))
        scratch = [pltpu.VMEM((d // LANES, tm, LANES), F32)]
    else:
        out_shape = jax.ShapeDtypeStruct((bsz, s, d), out_dtype)
        out_spec = pl.BlockSpec((None, tm, d), lambda b, i: (b, i, 0))
        scratch = []
    return pl.pallas_call(
        functools.partial(_prenorm_kernel, residue_major=residue_major),
        grid=(bsz, s // tm),
        in_specs=in_specs,
        out_specs=out_spec,
        out_shape=out_shape,
        scratch_shapes=scratch,
        compiler_params=_params(2),
        name="prenorm_rm" if residue_major else "prenorm",
    )(x, g.reshape(1, d), vec(shift), vec(scale))


def _dft4(x):
    (x0r, x0i), (x1r, x1i), (x2r, x2i), (x3r, x3i) = x
    t0r, t0i = x0r + x2r, x0i + x2i
    t1r, t1i = x0r - x2r, x0i - x2i
    t2r, t2i = x1r + x3r, x1i + x3i
    t3r, t3i = x1r - x3r, x1i - x3i
    return [(t0r + t2r, t0i + t2i),
            (t1r + t3i, t1i - t3r),
            (t0r - t2r, t0i - t2i),
            (t1r - t3i, t1i + t3r)]


def _twiddle16(z, p):
    zr, zi = z
    p = p % 16
    if p == 0:
        return zr, zi
    if p == 4:
        return zi, -zr
    if p == 8:
        return -zr, -zi
    if p == 12:
        return -zi, zr
    cr = math.cos(2.0 * math.pi * p / 16.0)
    ci = -math.sin(2.0 * math.pi * p / 16.0)
    return zr * cr - zi * ci, zr * ci + zi * cr


def _seqdft_kernel(h_ref, w_ref, yr_ref, yi_ref, br_ref, bi_ref):
    n2 = h_ref.shape[1]
    td = h_ref.shape[2]
    rows = 16
    for s1 in range(DFT_RADIX):
        p = jnp.dot(w_ref[s1], h_ref[s1], preferred_element_type=F32)
        br_ref[s1] = p[:n2]
        bi_ref[s1] = p[n2:]

    def pass1(it, carry):
        r = pl.multiple_of(it * rows, rows)
        for l0 in range(0, td, LANES):
            for b in range(4):
                xs = [(br_ref[4 * a + b, pl.ds(r, rows), l0:l0 + LANES],
                       bi_ref[4 * a + b, pl.ds(r, rows), l0:l0 + LANES]) for a in range(4)]
                us = _dft4(xs)
                for c in range(4):
                    ur, ui = _twiddle16(us[c], b * c)
                    br_ref[4 * c + b, pl.ds(r, rows), l0:l0 + LANES] = ur
                    bi_ref[4 * c + b, pl.ds(r, rows), l0:l0 + LANES] = ui
        return carry

    def pass2(it, carry):
        r = pl.multiple_of(it * rows, rows)
        for l0 in range(0, td, LANES):
            for c in range(4):
                us = [(br_ref[4 * c + b, pl.ds(r, rows), l0:l0 + LANES],
                       bi_ref[4 * c + b, pl.ds(r, rows), l0:l0 + LANES]) for b in range(4)]
                ys = _dft4(us)
                for d in range(4):
                    k1 = c + 4 * d
                    yr_ref[pl.ds(k1 * n2 + r, rows), l0:l0 + LANES] = ys[d][0].astype(yr_ref.dtype)
                    yi_ref[pl.ds(k1 * n2 + r, rows), l0:l0 + LANES] = ys[d][1].astype(yi_ref.dtype)
        return carry

    lax.fori_loop(0, n2 // rows, pass1, 0)
    lax.fori_loop(0, n2 // rows, pass2, 0)


def _seqdft_tables(s):
    n2 = s // DFT_RADIX
    s1 = jnp.arange(DFT_RADIX, dtype=jnp.int32)[:, None, None]
    k2 = jnp.arange(n2, dtype=jnp.int32)[None, :, None]
    s2 = jnp.arange(n2, dtype=jnp.int32)[None, None, :]
    j = (k2 * (DFT_RADIX * s2 + s1)) % s
    ang = j.astype(F32) * (2.0 * jnp.pi / s)
    return jnp.concatenate([jnp.cos(ang), -jnp.sin(ang)], axis=1).astype(BF16)


def _seqdft(h_rm, w_tab, td=256):
    bsz, _, n2, d = h_rm.shape
    s = DFT_RADIX * n2
    td = min(td, d)
    out = jax.ShapeDtypeStruct((bsz, s, d), BF16)
    return pl.pallas_call(
        _seqdft_kernel,
        grid=(bsz, d // td),
        in_specs=[
            pl.BlockSpec((None, DFT_RADIX, n2, td), lambda b, j: (b, 0, 0, j)),
            _resident((DFT_RADIX, 2 * n2, n2), lambda b, j: (0, 0, 0)),
        ],
        out_specs=[pl.BlockSpec((None, s, td), lambda b, j: (b, 0, j)),
                   pl.BlockSpec((None, s, td), lambda b, j: (b, 0, j))],
        out_shape=[out, out],
        scratch_shapes=[pltpu.VMEM((DFT_RADIX, n2, td), F32),
                        pltpu.VMEM((DFT_RADIX, n2, td), F32)],
        compiler_params=_params(2),
        name="seq_dft",
    )(h_rm, w_tab)


def _retproj_kernel(h_ref, wq_ref, wkt_ref, wv_ref, wg_ref, cos_ref, sin_ref, cost_ref, sint_ref,
                    q_ref, kt_ref, v_ref, sg_ref, *, dk, chunk):
    h = h_ref[...]
    half = dk // 2
    tm = h.shape[0]
    cos, sin = cos_ref[...], sin_ref[...]
    cost, sint = cost_ref[...], sint_ref[...]
    k_scale = dk ** -0.5
    for hd in range(RET_HEADS):
        c0 = hd * dk
        q = jnp.dot(h, wq_ref[:, c0:c0 + dk], preferred_element_type=F32)
        q1, q2 = q[:, :half], q[:, half:]
        q_ref[:, c0:c0 + half] = (q1 * cos - q2 * sin).astype(q_ref.dtype)
        q_ref[:, c0 + half:c0 + dk] = (q1 * sin + q2 * cos).astype(q_ref.dtype)
        kt = lax.dot_general(wkt_ref[c0:c0 + dk, :], h, (((1,), (1,)), ((), ())),
                             preferred_element_type=F32)
        k1, k2 = kt[:half], kt[half:]
        r1 = (k1 * cost - k2 * sint) * k_scale
        r2 = (k1 * sint + k2 * cost) * k_scale
        for j in range(tm // chunk):
            kt_ref[j, c0:c0 + half, :] = r1[:, j * chunk:(j + 1) * chunk].astype(kt_ref.dtype)
            kt_ref[j, c0 + half:c0 + dk, :] = r2[:, j * chunk:(j + 1) * chunk].astype(kt_ref.dtype)
    dvt = wv_ref.shape[1]
    for c0 in range(0, dvt, 256):
        c1 = min(c0 + 256, dvt)
        v_ref[:, c0:c1] = jnp.dot(h, wv_ref[:, c0:c1], preferred_element_type=F32).astype(v_ref.dtype)
        g = jnp.dot(h, wg_ref[:, c0:c1], preferred_element_type=F32)
        sg_ref[:, c0:c1] = _silu(g).astype(sg_ref.dtype)


def _retproj(h, wq, wkt, wv, wg, cos, sin, chunk, tm=ROW_TILE):
    bsz, s, d = h.shape
    hdk = wq.shape[1]
    hdv = wv.shape[1]
    dk = hdk // RET_HEADS
    half = dk // 2
    tm = min(tm, s)
    row = lambda b, i: (b, i, 0)
    const = lambda b, i: (0, 0)
    return pl.pallas_call(
        functools.partial(_retproj_kernel, dk=dk, chunk=chunk),
        grid=(bsz, s // tm),
        in_specs=[
            pl.BlockSpec((None, tm, d), row),
            _resident((d, hdk), const),
            _resident((hdk, d), const),
            _resident((d, hdv), const),
            _resident((d, hdv), const),
            pl.BlockSpec((tm, half), lambda b, i: (i, 0)),
            pl.BlockSpec((tm, half), lambda b, i: (i, 0)),
            pl.BlockSpec((half, tm), lambda b, i: (0, i)),
            pl.BlockSpec((half, tm), lambda b, i: (0, i)),
        ],
        out_specs=[
            pl.BlockSpec((None, tm, hdk), row),
            pl.BlockSpec((None, tm // chunk, hdk, chunk), lambda b, i: (b, i, 0, 0)),
            pl.BlockSpec((None, tm, hdv), row),
            pl.BlockSpec((None, tm, hdv), row),
        ],
        out_shape=[
            jax.ShapeDtypeStruct((bsz, s, hdk), BF16),
            jax.ShapeDtypeStruct((bsz, s // chunk, hdk, chunk), BF16),
            jax.ShapeDtypeStruct((bsz, s, hdv), BF16),
            jax.ShapeDtypeStruct((bsz, s, hdv), BF16),
        ],
        compiler_params=_params(2),
        name="ret_proj",
    )(h, wq, wkt, wv, wg, cos, sin, cos.T, sin.T)


def _retcore_kernel(cd_ref, q_ref, kt_ref, v_ref, dm_ref, qdf_ref, qdb_ref, kdf_ref, kdb_ref,
                    y_ref, r_all_ref, rf_ref, rb_ref):
    hd = pl.program_id(1)
    nc, dk, c = kt_ref.shape
    cd_f = cd_ref[0, hd]
    cd_b = cd_ref[1, hd]
    kdf = kdf_ref[...]
    kdb = kdb_ref[...]

    rf_ref[...] = jnp.zeros_like(rf_ref)
    rb_ref[...] = jnp.zeros_like(rb_ref)
    for it in range(nc):
        nf, nb = it, nc - 1 - it
        r_all_ref[nf, :dk, :] = rf_ref[...].astype(r_all_ref.dtype)
        r_all_ref[nb, dk:, :] = rb_ref[...].astype(r_all_ref.dtype)
        if it == nc - 1:
            break
        ktf = (kt_ref[nf].astype(F32) * kdf).astype(BF16)
        ktb = (kt_ref[nb].astype(F32) * kdb).astype(BF16)
        rf_ref[...] = rf_ref[...] * cd_f + jnp.dot(
            ktf, v_ref[nf * c:(nf + 1) * c, :], preferred_element_type=F32)
        rb_ref[...] = rb_ref[...] * cd_b + jnp.dot(
            ktb, v_ref[nb * c:(nb + 1) * c, :], preferred_element_type=F32)

    reps = dk // LANES
    qdf = jnp.tile(qdf_ref[...], (1, reps))
    qdb = jnp.tile(qdb_ref[...], (1, reps))
    for n in range(nc):
        rows = slice(n * c, (n + 1) * c)
        qn = q_ref[rows, :]
        sc = jnp.dot(qn, kt_ref[n], preferred_element_type=F32) * dm_ref[...]
        y = jnp.dot(sc.astype(BF16), v_ref[rows, :], preferred_element_type=F32)
        qf32 = qn.astype(F32)
        qfb = jnp.concatenate([(qf32 * qdf).astype(BF16), (qf32 * qdb).astype(BF16)], axis=1)
        y += jnp.dot(qfb, r_all_ref[n], preferred_element_type=F32)
        mu = jnp.mean(y, axis=-1, keepdims=True)
        yc = y - mu
        var = jnp.mean(yc * yc, axis=-1, keepdims=True)
        y_ref[rows, :] = (yc * lax.rsqrt(var + EPS)).astype(y_ref.dtype)


def _ret_decay_tables(c):
    hidx = jnp.arange(RET_HEADS, dtype=F32)
    lg_f = jnp.log1p(-jnp.exp2(-5.0 - hidx))
    lg_b = jnp.flip(lg_f)
    t = jnp.arange(c, dtype=jnp.int32)
    diff = (t[:, None] - t[None, :])
    fmask = diff >= 0
    bmask = diff < 0
    ef = jnp.exp(jnp.where(fmask, diff, 0).astype(F32)[None] * lg_f[:, None, None])
    eb = jnp.exp(jnp.where(bmask, -diff, 0).astype(F32)[None] * lg_b[:, None, None])
    dm = jnp.where(fmask[None], ef, eb)
    tf = t.astype(F32)
    ones = jnp.ones((1, 1, LANES), F32)
    qdf = jnp.exp((tf + 1.0)[None, :] * lg_f[:, None])[:, :, None] * ones
    qdb = jnp.exp((c - tf)[None, :] * lg_b[:, None])[:, :, None] * ones
    kdf = jnp.exp((c - 1.0 - tf)[None, :] * lg_f[:, None])[:, None, :]
    kdb = jnp.exp(tf[None, :] * lg_b[:, None])[:, None, :]
    cd = jnp.stack([jnp.exp(c * lg_f), jnp.exp(c * lg_b)])
    return cd, dm, qdf, qdb, kdf, kdb


def _retcore(q, kt, v, tables):
    bsz, s, hdk = q.shape
    hdv = v.shape[2]
    dk, dv = hdk // RET_HEADS, hdv // RET_HEADS
    nc, c = kt.shape[1], kt.shape[3]
    cd, dm, qdf, qdb, kdf, kdb = tables
    per_head = lambda b, h: (h, 0, 0)
    return pl.pallas_call(
        _retcore_kernel,
        grid=(bsz, RET_HEADS),
        in_specs=[
            pl.BlockSpec(memory_space=pltpu.SMEM),
            pl.BlockSpec((None, s, dk), lambda b, h: (b, 0, h)),
            pl.BlockSpec((None, nc, dk, c), lambda b, h: (b, 0, h, 0)),
            pl.BlockSpec((None, s, dv), lambda b, h: (b, 0, h)),
            pl.BlockSpec((None, c, c), per_head),
            pl.BlockSpec((None, c, LANES), per_head),
            pl.BlockSpec((None, c, LANES), per_head),
            pl.BlockSpec((None, 1, c), per_head),
            pl.BlockSpec((None, 1, c), per_head),
        ],
        out_specs=pl.BlockSpec((None, s, dv), lambda b, h: (b, 0, h)),
        out_shape=jax.ShapeDtypeStruct((bsz, s, hdv), BF16),
        scratch_shapes=[pltpu.VMEM((nc, 2 * dk, dv), BF16),
                        pltpu.VMEM((dk, dv), F32),
                        pltpu.VMEM((dk, dv), F32)],
        compiler_params=_params(2),
        name="ret_core",
    )(cd, q, kt, v, dm, qdf, qdb, kdf, kdb)


def _mix_ffn_kernel(*refs, fourier):
    if fourier:
        (x_ref, g1_ref, nf_ref, sh2_ref, sc2_ref, g2_ref, yr_ref, yi_ref, cc_ref, sc_ref, wo_ref,
         win_ref, wout_ref, o_ref, h2_ref, a_ref) = refs
        gd = cc_ref.shape[0]
        fs = []
        for g0 in range(0, yr_ref.shape[1], gd):
            f = jnp.dot(yr_ref[:, g0:g0 + gd], cc_ref[...], preferred_element_type=F32)
            f += jnp.dot(yi_ref[:, g0:g0 + gd], sc_ref[...], preferred_element_type=F32)
            fs.append(f.astype(BF16))
        z = jnp.concatenate(fs, axis=1)
    else:
        (x_ref, g1_ref, nf_ref, sh2_ref, sc2_ref, g2_ref, y_ref, sg_ref, wo_ref,
         win_ref, wout_ref, o_ref, h2_ref, a_ref) = refs
        z = (sg_ref[...].astype(F32) * y_ref[...].astype(F32)).astype(BF16)
    m = jnp.dot(z, wo_ref[...], preferred_element_type=F32)
    x1 = x_ref[...] + g1_ref[...] * m
    h2_ref[...] = _norm_mod(x1, nf_ref[...], sh2_ref[...], sc2_ref[...]).astype(h2_ref.dtype)
    o_ref[...] = x1
    fc = win_ref.shape[2] // 2
    for j in range(win_ref.shape[0]):
        gu = jnp.dot(h2_ref[...], win_ref[j], preferred_element_type=F32)
        a_ref[:, j * fc:(j + 1) * fc] = (_silu(gu[:, :fc]) * gu[:, fc:]).astype(a_ref.dtype)
    ffn = jnp.dot(a_ref[...], wout_ref[...], preferred_element_type=F32)
    o_ref[...] += g2_ref[...] * ffn


def _mix_ffn(x, g1, nf, sh2, sc2, g2, mix_inputs, mix_consts, wo, win, wout, *, fourier, tm=ROW_TILE):
    bsz, s, d = x.shape
    tm = min(tm, s)
    row = lambda b, i: (b, i, 0)
    vec = lambda a: a.reshape(bsz, 1, d)
    vspec = pl.BlockSpec((None, 1, d), lambda b, i: (b, 0, 0))
    in_specs = [pl.BlockSpec((None, tm, d), row), vspec,
                pl.BlockSpec((1, d), lambda b, i: (0, 0)), vspec, vspec, vspec]
    args = [x, vec(g1), nf.reshape(1, d), vec(sh2), vec(sc2), vec(g2)]
    for a in mix_inputs:
        in_specs.append(pl.BlockSpec((None, tm, a.shape[2]), row))
        args.append(a)
    for a in tuple(mix_consts) + (wo, win, wout):
        nd = a.ndim
        in_specs.append(_resident(a.shape, lambda b, i, nd=nd: (0,) * nd))
        args.append(a)
    return pl.pallas_call(
        functools.partial(_mix_ffn_kernel, fourier=fourier),
        grid=(bsz, s // tm),
        in_specs=in_specs,
        out_specs=pl.BlockSpec((None, tm, d), row),
        out_shape=jax.ShapeDtypeStruct((bsz, s, d), F32),
        scratch_shapes=[pltpu.VMEM((tm, d), BF16), pltpu.VMEM((tm, wout.shape[0]), BF16)],
        compiler_params=_params(2),
        name="mix_ffn_fourier" if fourier else "mix_ffn_ret",
    )(*args)


def _chan_dft_tables(gd, s):
    c = jnp.arange(gd, dtype=jnp.int32)
    j = (c[:, None] * c[None, :]) % gd
    ang = j.astype(F32) * (2.0 * jnp.pi / gd)
    scale = 1.0 / jnp.sqrt(jnp.asarray(s * gd, F32))
    return (jnp.cos(ang) * scale).astype(BF16), (jnp.sin(ang) * scale).astype(BF16)


def _rotary_tables(s, half):
    inv_freq = ROPE_BASE ** (-jnp.arange(half, dtype=F32) / half)
    ang = jnp.arange(s, dtype=F32)[:, None] * inv_freq[None, :]
    return jnp.cos(ang), jnp.sin(ang)


def kernel(x, c, w_ada, b_ada, norm_mix_g, norm_ffn_g, w_fourier_out, w_ret_in, w_ret_out,
           w_ffn_in, w_ffn_out, final_norm_g, w_ada_final, b_ada_final):
    bsz, s, d = x.shape
    depth = w_ada.shape[0]
    f = w_ffn_out.shape[1]
    nf = f // FFN_CHUNK
    gd = d // FOURIER_GROUPS
    dk = d // RET_HEADS
    hdk = RET_HEADS * dk
    hdv = w_ret_out.shape[1]
    chunk = min(RET_CHUNK, s)

    mods = _ada(c, w_ada, b_ada)
    mod_final = _ada(c, w_ada_final[None], b_ada_final[None])[0]

    seq_tab = _seqdft_tables(s)
    cc_tab, sc_tab = _chan_dft_tables(gd, s)
    cos, sin = _rotary_tables(s, dk // 2)
    ret_tabs = _ret_decay_tables(chunk)

    win_all = (w_ffn_in.astype(BF16).reshape(depth, d, 2, nf, FFN_CHUNK).transpose(0, 3, 1, 2, 4)
               .reshape(depth, nf, d, 2 * FFN_CHUNK))
    wout_all = w_ffn_out.astype(BF16)

    for i in range(depth):
        sh1, sc1, g1, sh2, sc2, g2 = jnp.split(mods[i], 6, axis=-1)
        j = i // N_MIXERS
        if i % N_MIXERS == 0:
            h = _prenorm(x, norm_mix_g[i], sh1, sc1, residue_major=True, out_dtype=BF16)
            yr, yi = _seqdft(h, seq_tab)
            x = _mix_ffn(x, g1, norm_ffn_g[i], sh2, sc2, g2, (yr, yi), (cc_tab, sc_tab),
                         w_fourier_out[j].astype(BF16), win_all[i], wout_all[i], fourier=True)
        else:
            h = _prenorm(x, norm_mix_g[i], sh1, sc1, residue_major=False, out_dtype=BF16)
            w_in = w_ret_in[j].astype(BF16)
            wq = w_in[:, :hdk]
            wkt = w_in[:, hdk:2 * hdk].T
            wv = w_in[:, 2 * hdk:2 * hdk + hdv]
            wg = w_in[:, 2 * hdk + hdv:]
            q, kt, v, sg = _retproj(h, wq, wkt, wv, wg, cos, sin, chunk)
            y = _retcore(q, kt, v, ret_tabs)
            x = _mix_ffn(x, g1, norm_ffn_g[i], sh2, sc2, g2, (y, sg), (),
                         w_ret_out[j].astype(BF16), win_all[i], wout_all[i], fourier=False)
    shf, scf = jnp.split(mod_final, 2, axis=-1)
    return _prenorm(x, final_norm_g, shf, scf, residue_major=False, out_dtype=x.dtype)
```

```python
import functools
import math

import jax
import jax.numpy as jnp
from jax import lax
from jax.experimental import pallas as pl
from jax.experimental.pallas import tpu as pltpu

EPS = 1e-6
N_MIXERS = 2
FOURIER_GROUPS = 4
RET_HEADS = 4
ROPE_BASE = 10000.0

DFT_RADIX = 16
RET_CHUNK = 256
FFN_CHUNK = 256
LANES = 128
ROW_TILE = 512
VMEM_LIMIT_BYTES = 56 * 1024 * 1024

F32 = jnp.float32
BF16 = jnp.bfloat16


def _params(n_grid_dims):
    return pltpu.CompilerParams(
        dimension_semantics=("arbitrary",) * n_grid_dims,
        vmem_limit_bytes=VMEM_LIMIT_BYTES)


def _resident(block_shape, index_map):
    return pl.BlockSpec(block_shape, index_map, pipeline_mode=pl.Buffered(1))


def _silu(v):
    return v * jax.nn.sigmoid(v)


def _norm_mod(x, g, shift, scale):
    ms = jnp.mean(x * x, axis=-1, keepdims=True)
    y = x * lax.rsqrt(ms + EPS)
    return (y * g) * (1.0 + scale) + shift


def _ada_kernel(c_ref, w_ref, b_ref, o_ref):
    ca = _silu(c_ref[...])
    o_ref[...] = jnp.dot(ca, w_ref[...], preferred_element_type=F32,
                         precision=lax.Precision.HIGHEST) + b_ref[...]


def _ada(c, w, b, tn=1024):
    n_layers, d, n = w.shape
    bsz = c.shape[0]
    tn = min(tn, n)
    return pl.pallas_call(
        _ada_kernel,
        grid=(n_layers, n // tn),
        in_specs=[
            pl.BlockSpec((bsz, d), lambda l, j: (0, 0)),
            pl.BlockSpec((None, d, tn), lambda l, j: (l, 0, j)),
            pl.BlockSpec((None, 1, tn), lambda l, j: (l, 0, j)),
        ],
        out_specs=pl.BlockSpec((None, bsz, tn), lambda l, j: (l, 0, j)),
        out_shape=jax.ShapeDtypeStruct((n_layers, bsz, n), F32),
        compiler_params=_params(2),
        name="ada_mod",
    )(c, w, b.reshape(n_layers, 1, n))


PERM_ROWS = DFT_RADIX * DFT_RADIX


def _residue_perm():
    r = jnp.arange(PERM_ROWS, dtype=jnp.int32)
    src = (r % DFT_RADIX) * DFT_RADIX + r // DFT_RADIX
    return (src[:, None] == r[None, :]).astype(BF16)


def _store_residue_major(h, p_ref, o_ref):
    hb = h.astype(o_ref.dtype)
    for r0 in range(0, h.shape[0], PERM_ROWS):
        p = jnp.dot(p_ref[...], hb[r0:r0 + PERM_ROWS], preferred_element_type=F32).astype(o_ref.dtype)
        j0 = r0 // DFT_RADIX
        for s1 in range(DFT_RADIX):
            o_ref[s1, j0:j0 + DFT_RADIX, :] = p[s1 * DFT_RADIX:(s1 + 1) * DFT_RADIX]


def _prenorm_kernel(x_ref, g_ref, sh_ref, sc_ref, *rest, residue_major):
    h = _norm_mod(x_ref[...], g_ref[...], sh_ref[...], sc_ref[...])
    if residue_major:
        p_ref, o_ref = rest
        _store_residue_major(h, p_ref, o_ref)
    else:
        o_ref, = rest
        o_ref[...] = h.astype(o_ref.dtype)


def _prenorm(x, g, shift, scale, *, residue_major, out_dtype, tm=ROW_TILE):
    bsz, s, d = x.shape
    tm = min(tm, s)
    vec = lambda a: a.reshape(bsz, 1, d)
    in_specs = [
        pl.BlockSpec((None, tm, d), lambda b, i: (b, i, 0)),
        pl.BlockSpec((1, d), lambda b, i: (0, 0)),
        pl.BlockSpec((None, 1, d), lambda b, i: (b, 0, 0)),
        pl.BlockSpec((None, 1, d), lambda b, i: (b, 0, 0)),
    ]
    args = [x, g.reshape(1, d), vec(shift), vec(scale)]
    if residue_major:
        assert tm % PERM_ROWS == 0
        in_specs.append(_resident((PERM_ROWS, PERM_ROWS), lambda b, i: (0, 0)))
        args.append(_residue_perm())
        out_shape = jax.ShapeDtypeStruct((bsz, DFT_RADIX, s // DFT_RADIX, d), out_dtype)
        out_spec = pl.BlockSpec((None, DFT_RADIX, tm // DFT_RADIX, d), lambda b, i: (b, 0, i, 0))
    else:
        out_shape = jax.ShapeDtypeStruct((bsz, s, d), out_dtype)
        out_spec = pl.BlockSpec((None, tm, d), lambda b, i: (b, i, 0))
    return pl.pallas_call(
        functools.partial(_prenorm_kernel, residue_major=residue_major),
        grid=(bsz, s // tm),
        in_specs=in_specs,
        out_specs=out_spec,
        out_shape=out_shape,
        compiler_params=_params(2),
        name="prenorm_rm" if residue_major else "prenorm",
    )(*args)


def _dft4(x):
    (x0r, x0i), (x1r, x1i), (x2r, x2i), (x3r, x3i) = x
    t0r, t0i = x0r + x2r, x0i + x2i
    t1r, t1i = x0r - x2r, x0i - x2i
    t2r, t2i = x1r + x3r, x1i + x3i
    t3r, t3i = x1r - x3r, x1i - x3i
    return [(t0r + t2r, t0i + t2i),
            (t1r + t3i, t1i - t3r),
            (t0r - t2r, t0i - t2i),
            (t1r - t3i, t1i + t3r)]


def _twiddle16(z, p):
    zr, zi = z
    p = p % 16
    if p == 0:
        return zr, zi
    if p == 4:
        return zi, -zr
    if p == 8:
        return -zr, -zi
    if p == 12:
        return -zi, zr
    cr = math.cos(2.0 * math.pi * p / 16.0)
    ci = -math.sin(2.0 * math.pi * p / 16.0)
    return zr * cr - zi * ci, zr * ci + zi * cr


def _seqdft_kernel(h_ref, w_ref, yr_ref, yi_ref, br_ref, bi_ref):
    n2 = h_ref.shape[1]
    td = h_ref.shape[2]
    rows = 16
    for s1 in range(DFT_RADIX):
        p = jnp.dot(w_ref[s1], h_ref[s1], preferred_element_type=F32)
        br_ref[s1] = p[:n2]
        bi_ref[s1] = p[n2:]

    def pass1(it, carry):
        r = pl.multiple_of(it * rows, rows)
        for l0 in range(0, td, LANES):
            for b in range(4):
                xs = [(br_ref[4 * a + b, pl.ds(r, rows), l0:l0 + LANES],
                       bi_ref[4 * a + b, pl.ds(r, rows), l0:l0 + LANES]) for a in range(4)]
                us = _dft4(xs)
                for c in range(4):
                    ur, ui = _twiddle16(us[c], b * c)
                    br_ref[4 * c + b, pl.ds(r, rows), l0:l0 + LANES] = ur
                    bi_ref[4 * c + b, pl.ds(r, rows), l0:l0 + LANES] = ui
        return carry

    def pass2(it, carry):
        r = pl.multiple_of(it * rows, rows)
        for l0 in range(0, td, LANES):
            for c in range(4):
                us = [(br_ref[4 * c + b, pl.ds(r, rows), l0:l0 + LANES],
                       bi_ref[4 * c + b, pl.ds(r, rows), l0:l0 + LANES]) for b in range(4)]
                ys = _dft4(us)
                for d in range(4):
                    k1 = c + 4 * d
                    yr_ref[pl.ds(k1 * n2 + r, rows), l0:l0 + LANES] = ys[d][0].astype(yr_ref.dtype)
                    yi_ref[pl.ds(k1 * n2 + r, rows), l0:l0 + LANES] = ys[d][1].astype(yi_ref.dtype)
        return carry

    lax.fori_loop(0, n2 // rows, pass1, 0)
    lax.fori_loop(0, n2 // rows, pass2, 0)


def _seqdft_tables(s):
    n2 = s // DFT_RADIX
    s1 = jnp.arange(DFT_RADIX, dtype=jnp.int32)[:, None, None]
    k2 = jnp.arange(n2, dtype=jnp.int32)[None, :, None]
    s2 = jnp.arange(n2, dtype=jnp.int32)[None, None, :]
    j = (k2 * (DFT_RADIX * s2 + s1)) % s
    ang = j.astype(F32) * (2.0 * jnp.pi / s)
    return jnp.concatenate([jnp.cos(ang), -jnp.sin(ang)], axis=1).astype(BF16)


def _seqdft(h_rm, w_tab, td=256):
    bsz, _, n2, d = h_rm.shape
    s = DFT_RADIX * n2
    td = min(td, d)
    out = jax.ShapeDtypeStruct((bsz, s, d), BF16)
    return pl.pallas_call(
        _seqdft_kernel,
        grid=(bsz, d // td),
        in_specs=[
            pl.BlockSpec((None, DFT_RADIX, n2, td), lambda b, j: (b, 0, 0, j)),
            _resident((DFT_RADIX, 2 * n2, n2), lambda b, j: (0, 0, 0)),
        ],
        out_specs=[pl.BlockSpec((None, s, td), lambda b, j: (b, 0, j)),
                   pl.BlockSpec((None, s, td), lambda b, j: (b, 0, j))],
        out_shape=[out, out],
        scratch_shapes=[pltpu.VMEM((DFT_RADIX, n2, td), F32),
                        pltpu.VMEM((DFT_RADIX, n2, td), F32)],
        compiler_params=_params(2),
        name="seq_dft",
    )(h_rm, w_tab)


def _retproj_kernel(h_ref, w_ref, wkt_ref, cos_ref, sin_ref, cost_ref, sint_ref,
                    q_ref, kt_ref, v_ref, sg_ref, *, dk, chunk):
    h = h_ref[...]
    half = dk // 2
    tm = h.shape[0]
    hdk = RET_HEADS * dk
    dvt = v_ref.shape[1]
    cos, sin = cos_ref[...], sin_ref[...]
    cost, sint = cost_ref[...], sint_ref[...]
    k_scale = dk ** -0.5
    for hd in range(RET_HEADS):
        c0 = hd * dk
        q = jnp.dot(h, w_ref[:, c0:c0 + dk], preferred_element_type=F32)
        q1, q2 = q[:, :half], q[:, half:]
        q_ref[:, c0:c0 + half] = (q1 * cos - q2 * sin).astype(q_ref.dtype)
        q_ref[:, c0 + half:c0 + dk] = (q1 * sin + q2 * cos).astype(q_ref.dtype)
        kt = lax.dot_general(wkt_ref[c0:c0 + dk, :], h, (((1,), (1,)), ((), ())),
                             preferred_element_type=F32)
        k1, k2 = kt[:half], kt[half:]
        r1 = (k1 * cost - k2 * sint) * k_scale
        r2 = (k1 * sint + k2 * cost) * k_scale
        for j in range(tm // chunk):
            kt_ref[j, c0:c0 + half, :] = r1[:, j * chunk:(j + 1) * chunk].astype(kt_ref.dtype)
            kt_ref[j, c0 + half:c0 + dk, :] = r2[:, j * chunk:(j + 1) * chunk].astype(kt_ref.dtype)
    for c0 in range(0, dvt, 256):
        c1 = min(c0 + 256, dvt)
        v0, g0 = 2 * hdk, 2 * hdk + dvt
        v_ref[:, c0:c1] = jnp.dot(h, w_ref[:, v0 + c0:v0 + c1],
                                  preferred_element_type=F32).astype(v_ref.dtype)
        g = jnp.dot(h, w_ref[:, g0 + c0:g0 + c1], preferred_element_type=F32)
        sg_ref[:, c0:c1] = _silu(g).astype(sg_ref.dtype)


def _retproj(h, w_all, wkt_all, layer, cos, sin, chunk, tm=ROW_TILE):
    bsz, s, d = h.shape
    hdk = wkt_all.shape[1]
    hdv = (w_all.shape[2] - 2 * hdk) // 2
    dk = hdk // RET_HEADS
    half = dk // 2
    tm = min(tm, s)
    row = lambda b, i: (b, i, 0)
    return pl.pallas_call(
        functools.partial(_retproj_kernel, dk=dk, chunk=chunk),
        grid=(bsz, s // tm),
        in_specs=[
            pl.BlockSpec((None, tm, d), row),
            _resident((None, d, w_all.shape[2]), lambda b, i: (layer, 0, 0)),
            _resident((None, hdk, d), lambda b, i: (layer, 0, 0)),
            pl.BlockSpec((tm, half), lambda b, i: (i, 0)),
            pl.BlockSpec((tm, half), lambda b, i: (i, 0)),
            pl.BlockSpec((half, tm), lambda b, i: (0, i)),
            pl.BlockSpec((half, tm), lambda b, i: (0, i)),
        ],
        out_specs=[
            pl.BlockSpec((None, tm, hdk), row),
            pl.BlockSpec((None, tm // chunk, hdk, chunk), lambda b, i: (b, i, 0, 0)),
            pl.BlockSpec((None, tm, hdv), row),
            pl.BlockSpec((None, tm, hdv), row),
        ],
        out_shape=[
            jax.ShapeDtypeStruct((bsz, s, hdk), BF16),
            jax.ShapeDtypeStruct((bsz, s // chunk, hdk, chunk), BF16),
            jax.ShapeDtypeStruct((bsz, s, hdv), BF16),
            jax.ShapeDtypeStruct((bsz, s, hdv), BF16),
        ],
        compiler_params=_params(2),
        name="ret_proj",
    )(h, w_all, wkt_all, cos, sin, cos.T, sin.T)


def _retcore_kernel(cd_ref, q_ref, kt_ref, v_ref, dm_ref, qdf_ref, qdb_ref, kdf_ref, kdb_ref,
                    y_ref, r_all_ref, rf_ref, rb_ref):
    hd = pl.program_id(1)
    nc, dk, c = kt_ref.shape
    cd_f = cd_ref[0, hd]
    cd_b = cd_ref[1, hd]
    kdf = kdf_ref[...]
    kdb = kdb_ref[...]

    rf_ref[...] = jnp.zeros_like(rf_ref)
    rb_ref[...] = jnp.zeros_like(rb_ref)
    for it in range(nc):
        nf, nb = it, nc - 1 - it
        r_all_ref[nf, :dk, :] = rf_ref[...].astype(r_all_ref.dtype)
        r_all_ref[nb, dk:, :] = rb_ref[...].astype(r_all_ref.dtype)
        if it == nc - 1:
            break
        ktf = (kt_ref[nf].astype(F32) * kdf).astype(BF16)
        ktb = (kt_ref[nb].astype(F32) * kdb).astype(BF16)
        rf_ref[...] = rf_ref[...] * cd_f + jnp.dot(
            ktf, v_ref[nf * c:(nf + 1) * c, :], preferred_element_type=F32)
        rb_ref[...] = rb_ref[...] * cd_b + jnp.dot(
            ktb, v_ref[nb * c:(nb + 1) * c, :], preferred_element_type=F32)

    reps = dk // LANES
    qdf = jnp.tile(qdf_ref[...], (1, reps))
    qdb = jnp.tile(qdb_ref[...], (1, reps))
    for n in range(nc):
        rows = slice(n * c, (n + 1) * c)
        qn = q_ref[rows, :]
        sc = jnp.dot(qn, kt_ref[n], preferred_element_type=F32) * dm_ref[...]
        y = jnp.dot(sc.astype(BF16), v_ref[rows, :], preferred_element_type=F32)
        qf32 = qn.astype(F32)
        qfb = jnp.concatenate([(qf32 * qdf).astype(BF16), (qf32 * qdb).astype(BF16)], axis=1)
        y += jnp.dot(qfb, r_all_ref[n], preferred_element_type=F32)
        mu = jnp.mean(y, axis=-1, keepdims=True)
        yc = y - mu
        var = jnp.mean(yc * yc, axis=-1, keepdims=True)
        y_ref[rows, :] = (yc * lax.rsqrt(var + EPS)).astype(y_ref.dtype)


def _ret_decay_tables(c):
    hidx = jnp.arange(RET_HEADS, dtype=F32)
    lg_f = jnp.log1p(-jnp.exp2(-5.0 - hidx))
    lg_b = jnp.flip(lg_f)
    t = jnp.arange(c, dtype=jnp.int32)
    diff = (t[:, None] - t[None, :])
    fmask = diff >= 0
    bmask = diff < 0
    ef = jnp.exp(jnp.where(fmask, diff, 0).astype(F32)[None] * lg_f[:, None, None])
    eb = jnp.exp(jnp.where(bmask, -diff, 0).astype(F32)[None] * lg_b[:, None, None])
    dm = jnp.where(fmask[None], ef, eb)
    tf = t.astype(F32)
    ones = jnp.ones((1, 1, LANES), F32)
    qdf = jnp.exp((tf + 1.0)[None, :] * lg_f[:, None])[:, :, None] * ones
    qdb = jnp.exp((c - tf)[None, :] * lg_b[:, None])[:, :, None] * ones
    kdf = jnp.exp((c - 1.0 - tf)[None, :] * lg_f[:, None])[:, None, :]
    kdb = jnp.exp(tf[None, :] * lg_b[:, None])[:, None, :]
    cd = jnp.stack([jnp.exp(c * lg_f), jnp.exp(c * lg_b)])
    return cd, dm, qdf, qdb, kdf, kdb


def _retcore(q, kt, v, tables):
    bsz, s, hdk = q.shape
    hdv = v.shape[2]
    dk, dv = hdk // RET_HEADS, hdv // RET_HEADS
    nc, c = kt.shape[1], kt.shape[3]
    cd, dm, qdf, qdb, kdf, kdb = tables
    per_head = lambda b, h: (h, 0, 0)
    return pl.pallas_call(
        _retcore_kernel,
        grid=(bsz, RET_HEADS),
        in_specs=[
            pl.BlockSpec(memory_space=pltpu.SMEM),
            pl.BlockSpec((None, s, dk), lambda b, h: (b, 0, h)),
            pl.BlockSpec((None, nc, dk, c), lambda b, h: (b, 0, h, 0)),
            pl.BlockSpec((None, s, dv), lambda b, h: (b, 0, h)),
            pl.BlockSpec((None, c, c), per_head),
            pl.BlockSpec((None, c, LANES), per_head),
            pl.BlockSpec((None, c, LANES), per_head),
            pl.BlockSpec((None, 1, c), per_head),
            pl.BlockSpec((None, 1, c), per_head),
        ],
        out_specs=pl.BlockSpec((None, s, dv), lambda b, h: (b, 0, h)),
        out_shape=jax.ShapeDtypeStruct((bsz, s, hdv), BF16),
        scratch_shapes=[pltpu.VMEM((nc, 2 * dk, dv), BF16),
                        pltpu.VMEM((dk, dv), F32),
                        pltpu.VMEM((dk, dv), F32)],
        compiler_params=_params(2),
        name="ret_core",
    )(cd, q, kt, v, dm, qdf, qdb, kdf, kdb)


def _mix_ffn_kernel(*refs, fourier, emit):
    n_out = 1 if emit == "final" else 2
    ins = refs[:len(refs) - n_out - 2]
    outs = refs[len(ins):len(ins) + n_out]
    h2_ref, a_ref = refs[len(ins) + n_out:]
    o_ref = outs[0]
    if emit == "x+h_rm":
        ins, perm_ref = ins[:-1], ins[-1]
    x_ref, vec_ref, gain_ref = ins[:3]
    g1, sh2, sc2, g2 = (vec_ref[r:r + 1, :] for r in range(4))
    if fourier:
        yr_ref, yi_ref, cc_ref, sc_ref, wo_ref, win_ref, wout_ref = ins[3:]
        gd = cc_ref.shape[0]
        fs = []
        for g0 in range(0, yr_ref.shape[1], gd):
            f = jnp.dot(yr_ref[:, g0:g0 + gd], cc_ref[...], preferred_element_type=F32)
            f += jnp.dot(yi_ref[:, g0:g0 + gd], sc_ref[...], preferred_element_type=F32)
            fs.append(f.astype(BF16))
        z = jnp.concatenate(fs, axis=1)
    else:
        y_ref, sg_ref, wo_ref, win_ref, wout_ref = ins[3:]
        z = (sg_ref[...].astype(F32) * y_ref[...].astype(F32)).astype(BF16)
    m = jnp.dot(z, wo_ref[...], preferred_element_type=F32)
    x1 = x_ref[...] + g1 * m
    h2_ref[...] = _norm_mod(x1, gain_ref[0:1, :], sh2, sc2).astype(h2_ref.dtype)
    o_ref[...] = x1
    f = wout_ref.shape[0]
    for c0 in range(0, f, FFN_CHUNK):
        gate = jnp.dot(h2_ref[...], win_ref[:, c0:c0 + FFN_CHUNK], preferred_element_type=F32)
        up = jnp.dot(h2_ref[...], win_ref[:, f + c0:f + c0 + FFN_CHUNK], preferred_element_type=F32)
        a_ref[:, c0:c0 + FFN_CHUNK] = (_silu(gate) * up).astype(a_ref.dtype)
    ffn = jnp.dot(a_ref[...], wout_ref[...], preferred_element_type=F32)
    x2 = o_ref[...] + g2 * ffn
    hn = _norm_mod(x2, gain_ref[1:2, :], vec_ref[4:5, :], vec_ref[5:6, :])
    if emit == "final":
        o_ref[...] = hn
        return
    o_ref[...] = x2
    hn_ref = outs[1]
    if emit == "x+h":
        hn_ref[...] = hn.astype(hn_ref.dtype)
    else:
        _store_residue_major(hn, perm_ref, hn_ref)


def _mix_ffn(x, vecs, gains, mix_inputs, mix_consts, wo_all, wo_layer, win_all, wout_all, layer,
             *, fourier, emit, tm=ROW_TILE):
    bsz, s, d = x.shape
    tm = min(tm, s)
    row = lambda b, i: (b, i, 0)
    in_specs = [pl.BlockSpec((None, tm, d), row),
                pl.BlockSpec((None, 8, d), lambda b, i: (b, 0, 0)),
                pl.BlockSpec((8, d), lambda b, i: (0, 0))]
    args = [x, vecs, gains]
    for a in mix_inputs:
        in_specs.append(pl.BlockSpec((None, tm, a.shape[2]), row))
        args.append(a)
    for a in mix_consts:
        in_specs.append(_resident(a.shape, lambda b, i: (0, 0)))
        args.append(a)
    for a, idx in ((wo_all, wo_layer), (win_all, layer), (wout_all, layer)):
        in_specs.append(_resident((None,) + a.shape[1:], lambda b, i, idx=idx: (idx, 0, 0)))
        args.append(a)
    out_specs = [pl.BlockSpec((None, tm, d), row)]
    out_shape = [jax.ShapeDtypeStruct((bsz, s, d), F32)]
    scratch = [pltpu.VMEM((tm, d), BF16), pltpu.VMEM((tm, wout_all.shape[1]), BF16)]
    if emit == "x+h":
        out_specs.append(pl.BlockSpec((None, tm, d), row))
        out_shape.append(jax.ShapeDtypeStruct((bsz, s, d), BF16))
    elif emit == "x+h_rm":
        out_specs.append(pl.BlockSpec((None, DFT_RADIX, tm // DFT_RADIX, d), lambda b, i: (b, 0, i, 0)))
        out_shape.append(jax.ShapeDtypeStruct((bsz, DFT_RADIX, s // DFT_RADIX, d), BF16))
        assert tm % PERM_ROWS == 0
        in_specs.append(_resident((PERM_ROWS, PERM_ROWS), lambda b, i: (0, 0)))
        args.append(_residue_perm())
    return pl.pallas_call(
        functools.partial(_mix_ffn_kernel, fourier=fourier, emit=emit),
        grid=(bsz, s // tm),
        in_specs=in_specs,
        out_specs=out_specs,
        out_shape=out_shape,
        scratch_shapes=scratch,
        compiler_params=_params(2),
        name="mix_ffn_fourier" if fourier else "mix_ffn_ret",
    )(*args)


def _chan_dft_tables(gd, s):
    c = jnp.arange(gd, dtype=jnp.int32)
    j = (c[:, None] * c[None, :]) % gd
    ang = j.astype(F32) * (2.0 * jnp.pi / gd)
    scale = 1.0 / jnp.sqrt(jnp.asarray(s * gd, F32))
    return (jnp.cos(ang) * scale).astype(BF16), (jnp.sin(ang) * scale).astype(BF16)


def _rotary_tables(s, half):
    inv_freq = ROPE_BASE ** (-jnp.arange(half, dtype=F32) / half)
    ang = jnp.arange(s, dtype=F32)[:, None] * inv_freq[None, :]
    return jnp.cos(ang), jnp.sin(ang)


def kernel(x, c, w_ada, b_ada, norm_mix_g, norm_ffn_g, w_fourier_out, w_ret_in, w_ret_out,
           w_ffn_in, w_ffn_out, final_norm_g, w_ada_final, b_ada_final):
    bsz, s, d = x.shape
    depth = w_ada.shape[0]
    gd = d // FOURIER_GROUPS
    dk = d // RET_HEADS
    hdk = RET_HEADS * dk
    chunk = min(RET_CHUNK, s)

    mods = _ada(c, w_ada, b_ada).reshape(depth, bsz, 6, d)
    mod_final = _ada(c, w_ada_final[None], b_ada_final[None]).reshape(bsz, 2, d)

    seq_tab = _seqdft_tables(s)
    cc_tab, sc_tab = _chan_dft_tables(gd, s)
    cos, sin = _rotary_tables(s, dk // 2)
    ret_tabs = _ret_decay_tables(chunk)

    win_all = w_ffn_in.astype(BF16)
    wout_all = w_ffn_out.astype(BF16)
    wfo_all = w_fourier_out.astype(BF16)
    wri_all = w_ret_in.astype(BF16)
    wrkt_all = jnp.swapaxes(w_ret_in[:, :, hdk:2 * hdk], 1, 2).astype(BF16)
    wro_all = w_ret_out.astype(BF16)

    def is_fourier(i):
        return i % N_MIXERS == 0

    h = _prenorm(x, norm_mix_g[0], mods[0, :, 0], mods[0, :, 1], residue_major=is_fourier(0),
                 out_dtype=BF16)
    for i in range(depth):
        last = i == depth - 1
        if last:
            next_mod, next_gain, emit = mod_final, final_norm_g, "final"
        else:
            next_mod, next_gain = mods[i + 1, :, 0:2], norm_mix_g[i + 1]
            emit = "x+h_rm" if is_fourier(i + 1) else "x+h"
        pad = jnp.zeros((bsz, 2, d), F32)
        vecs = jnp.concatenate([mods[i, :, 2:6], next_mod, pad], axis=1)
        gains = jnp.concatenate([norm_ffn_g[i][None], next_gain[None], jnp.zeros((6, d), F32)])
        j = i // N_MIXERS
        if is_fourier(i):
            yr, yi = _seqdft(h, seq_tab)
            outs = _mix_ffn(x, vecs, gains, (yr, yi), (cc_tab, sc_tab), wfo_all, j,
                            win_all, wout_all, i, fourier=True, emit=emit)
        else:
            q, kt, v, sg = _retproj(h, wri_all, wrkt_all, j, cos, sin, chunk)
            y = _retcore(q, kt, v, ret_tabs)
            outs = _mix_ffn(x, vecs, gains, (y, sg), (), wro_all, j,
                            win_all, wout_all, i, fourier=False, emit=emit)
        if last:
            return outs[0]
        x, h = outs
```

```python
import functools
import math

import jax
import jax.numpy as jnp
from jax import lax
from jax.experimental import pallas as pl
from jax.experimental.pallas import tpu as pltpu

EPS = 1e-6
N_MIXERS = 2
FOURIER_GROUPS = 4
RET_HEADS = 4
ROPE_BASE = 10000.0

DFT_RADIX = 16
RET_CHUNK = 256
FFN_CHUNK = 256
LANES = 128
ROW_TILE = 512
MIX_SUBTILES = 2
VMEM_LIMIT_BYTES = 56 * 1024 * 1024

F32 = jnp.float32
BF16 = jnp.bfloat16


def _params(n_grid_dims):
    return pltpu.CompilerParams(
        dimension_semantics=("arbitrary",) * n_grid_dims,
        vmem_limit_bytes=VMEM_LIMIT_BYTES)


def _resident(block_shape, index_map):
    return pl.BlockSpec(block_shape, index_map, pipeline_mode=pl.Buffered(1))


def _silu(v):
    return v * jax.nn.sigmoid(v)


def _norm_mod(x, g, shift, scale):
    ms = jnp.mean(x * x, axis=-1, keepdims=True)
    y = x * lax.rsqrt(ms + EPS)
    return (y * g) * (1.0 + scale) + shift


def _ada_kernel(c_ref, w_ref, b_ref, o_ref):
    ca = _silu(c_ref[...])
    o_ref[...] = jnp.dot(ca, w_ref[...], preferred_element_type=F32,
                         precision=lax.Precision.HIGHEST) + b_ref[...]


def _ada(c, w, b, tn=1024):
    n_layers, d, n = w.shape
    bsz = c.shape[0]
    tn = min(tn, n)
    return pl.pallas_call(
        _ada_kernel,
        grid=(n_layers, n // tn),
        in_specs=[
            pl.BlockSpec((bsz, d), lambda l, j: (0, 0)),
            pl.BlockSpec((None, d, tn), lambda l, j: (l, 0, j)),
            pl.BlockSpec((None, 1, tn), lambda l, j: (l, 0, j)),
        ],
        out_specs=pl.BlockSpec((None, bsz, tn), lambda l, j: (l, 0, j)),
        out_shape=jax.ShapeDtypeStruct((n_layers, bsz, n), F32),
        compiler_params=_params(2),
        name="ada_mod",
    )(c, w, b.reshape(n_layers, 1, n))


PERM_ROWS = DFT_RADIX * DFT_RADIX


def _residue_perm():
    r = jnp.arange(PERM_ROWS, dtype=jnp.int32)
    src = (r % DFT_RADIX) * DFT_RADIX + r // DFT_RADIX
    return (src[:, None] == r[None, :]).astype(BF16)


def _store_residue_major(h, p_ref, o_ref, row0=0):
    hb = h.astype(o_ref.dtype)
    for r0 in range(0, h.shape[0], PERM_ROWS):
        p = jnp.dot(p_ref[...], hb[r0:r0 + PERM_ROWS], preferred_element_type=F32).astype(o_ref.dtype)
        j0 = (row0 + r0) // DFT_RADIX
        for s1 in range(DFT_RADIX):
            o_ref[s1, j0:j0 + DFT_RADIX, :] = p[s1 * DFT_RADIX:(s1 + 1) * DFT_RADIX]


def _prenorm_kernel(x_ref, g_ref, sh_ref, sc_ref, *rest, residue_major):
    h = _norm_mod(x_ref[...], g_ref[...], sh_ref[...], sc_ref[...])
    if residue_major:
        p_ref, o_ref = rest
        _store_residue_major(h, p_ref, o_ref)
    else:
        o_ref, = rest
        o_ref[...] = h.astype(o_ref.dtype)


def _prenorm(x, g, shift, scale, *, residue_major, out_dtype, tm=ROW_TILE):
    bsz, s, d = x.shape
    tm = min(tm, s)
    vec = lambda a: a.reshape(bsz, 1, d)
    in_specs = [
        pl.BlockSpec((None, tm, d), lambda b, i: (b, i, 0)),
        pl.BlockSpec((1, d), lambda b, i: (0, 0)),
        pl.BlockSpec((None, 1, d), lambda b, i: (b, 0, 0)),
        pl.BlockSpec((None, 1, d), lambda b, i: (b, 0, 0)),
    ]
    args = [x, g.reshape(1, d), vec(shift), vec(scale)]
    if residue_major:
        assert tm % PERM_ROWS == 0
        in_specs.append(_resident((PERM_ROWS, PERM_ROWS), lambda b, i: (0, 0)))
        args.append(_residue_perm())
        out_shape = jax.ShapeDtypeStruct((bsz, DFT_RADIX, s // DFT_RADIX, d), out_dtype)
        out_spec = pl.BlockSpec((None, DFT_RADIX, tm // DFT_RADIX, d), lambda b, i: (b, 0, i, 0))
    else:
        out_shape = jax.ShapeDtypeStruct((bsz, s, d), out_dtype)
        out_spec = pl.BlockSpec((None, tm, d), lambda b, i: (b, i, 0))
    return pl.pallas_call(
        functools.partial(_prenorm_kernel, residue_major=residue_major),
        grid=(bsz, s // tm),
        in_specs=in_specs,
        out_specs=out_spec,
        out_shape=out_shape,
        compiler_params=_params(2),
        name="prenorm_rm" if residue_major else "prenorm",
    )(*args)


def _dft4(x):
    (x0r, x0i), (x1r, x1i), (x2r, x2i), (x3r, x3i) = x
    t0r, t0i = x0r + x2r, x0i + x2i
    t1r, t1i = x0r - x2r, x0i - x2i
    t2r, t2i = x1r + x3r, x1i + x3i
    t3r, t3i = x1r - x3r, x1i - x3i
    return [(t0r + t2r, t0i + t2i),
            (t1r + t3i, t1i - t3r),
            (t0r - t2r, t0i - t2i),
            (t1r - t3i, t1i + t3r)]


def _twiddle16(z, p):
    zr, zi = z
    p = p % 16
    if p == 0:
        return zr, zi
    if p == 4:
        return zi, -zr
    if p == 8:
        return -zr, -zi
    if p == 12:
        return -zi, zr
    cr = math.cos(2.0 * math.pi * p / 16.0)
    ci = -math.sin(2.0 * math.pi * p / 16.0)
    return zr * cr - zi * ci, zr * ci + zi * cr


def _seqdft_kernel(h_ref, w_ref, yr_ref, yi_ref, br_ref, bi_ref):
    n2 = h_ref.shape[1]
    td = h_ref.shape[2]
    rows = 16
    for s1 in range(DFT_RADIX):
        p = jnp.dot(w_ref[s1], h_ref[s1], preferred_element_type=F32)
        br_ref[s1] = p[:n2]
        bi_ref[s1] = p[n2:]

    def pass1(it, carry):
        r = pl.multiple_of(it * rows, rows)
        for l0 in range(0, td, LANES):
            for b in range(4):
                xs = [(br_ref[4 * a + b, pl.ds(r, rows), l0:l0 + LANES],
                       bi_ref[4 * a + b, pl.ds(r, rows), l0:l0 + LANES]) for a in range(4)]
                us = _dft4(xs)
                for c in range(4):
                    ur, ui = _twiddle16(us[c], b * c)
                    br_ref[4 * c + b, pl.ds(r, rows), l0:l0 + LANES] = ur
                    bi_ref[4 * c + b, pl.ds(r, rows), l0:l0 + LANES] = ui
        return carry

    def pass2(it, carry):
        r = pl.multiple_of(it * rows, rows)
        for l0 in range(0, td, LANES):
            for c in range(4):
                us = [(br_ref[4 * c + b, pl.ds(r, rows), l0:l0 + LANES],
                       bi_ref[4 * c + b, pl.ds(r, rows), l0:l0 + LANES]) for b in range(4)]
                ys = _dft4(us)
                for d in range(4):
                    k1 = c + 4 * d
                    yr_ref[pl.ds(k1 * n2 + r, rows), l0:l0 + LANES] = ys[d][0].astype(yr_ref.dtype)
                    yi_ref[pl.ds(k1 * n2 + r, rows), l0:l0 + LANES] = ys[d][1].astype(yi_ref.dtype)
        return carry

    lax.fori_loop(0, n2 // rows, pass1, 0)
    lax.fori_loop(0, n2 // rows, pass2, 0)


def _seqdft_tables(s):
    n2 = s // DFT_RADIX
    s1 = jnp.arange(DFT_RADIX, dtype=jnp.int32)[:, None, None]
    k2 = jnp.arange(n2, dtype=jnp.int32)[None, :, None]
    s2 = jnp.arange(n2, dtype=jnp.int32)[None, None, :]
    j = (k2 * (DFT_RADIX * s2 + s1)) % s
    ang = j.astype(F32) * (2.0 * jnp.pi / s)
    return jnp.concatenate([jnp.cos(ang), -jnp.sin(ang)], axis=1).astype(BF16)


def _seqdft(h_rm, w_tab, td=256):
    bsz, _, n2, d = h_rm.shape
    s = DFT_RADIX * n2
    td = min(td, d)
    out = jax.ShapeDtypeStruct((bsz, s, d), BF16)
    return pl.pallas_call(
        _seqdft_kernel,
        grid=(bsz, d // td),
        in_specs=[
            pl.BlockSpec((None, DFT_RADIX, n2, td), lambda b, j: (b, 0, 0, j)),
            _resident((DFT_RADIX, 2 * n2, n2), lambda b, j: (0, 0, 0)),
        ],
        out_specs=[pl.BlockSpec((None, s, td), lambda b, j: (b, 0, j)),
                   pl.BlockSpec((None, s, td), lambda b, j: (b, 0, j))],
        out_shape=[out, out],
        scratch_shapes=[pltpu.VMEM((DFT_RADIX, n2, td), F32),
                        pltpu.VMEM((DFT_RADIX, n2, td), F32)],
        compiler_params=_params(2),
        name="seq_dft",
    )(h_rm, w_tab)


def _retproj_kernel(h_ref, w_ref, cos_ref, sin_ref, cost_ref, sint_ref,
                    q_ref, kt_ref, v_ref, sg_ref, wkt_ref, *, dk, chunk):
    h = h_ref[...]
    half = dk // 2
    tm = h.shape[0]
    hdk = RET_HEADS * dk
    dvt = v_ref.shape[1]

    @pl.when((pl.program_id(0) == 0) & (pl.program_id(1) == 0))
    def _():
        for c0 in range(0, hdk, dk):
            wk = w_ref[:, hdk + c0:hdk + c0 + dk].astype(F32)
            wkt_ref[c0:c0 + dk, :] = wk.T.astype(wkt_ref.dtype)

    cos, sin = cos_ref[...], sin_ref[...]
    cost, sint = cost_ref[...], sint_ref[...]
    k_scale = dk ** -0.5
    for hd in range(RET_HEADS):
        c0 = hd * dk
        q = jnp.dot(h, w_ref[:, c0:c0 + dk], preferred_element_type=F32)
        q1, q2 = q[:, :half], q[:, half:]
        q_ref[:, c0:c0 + half] = (q1 * cos - q2 * sin).astype(q_ref.dtype)
        q_ref[:, c0 + half:c0 + dk] = (q1 * sin + q2 * cos).astype(q_ref.dtype)
        kt = lax.dot_general(wkt_ref[c0:c0 + dk, :], h, (((1,), (1,)), ((), ())),
                             preferred_element_type=F32)
        k1, k2 = kt[:half], kt[half:]
        r1 = (k1 * cost - k2 * sint) * k_scale
        r2 = (k1 * sint + k2 * cost) * k_scale
        for j in range(tm // chunk):
            kt_ref[j, c0:c0 + half, :] = r1[:, j * chunk:(j + 1) * chunk].astype(kt_ref.dtype)
            kt_ref[j, c0 + half:c0 + dk, :] = r2[:, j * chunk:(j + 1) * chunk].astype(kt_ref.dtype)
    for c0 in range(0, dvt, 256):
        c1 = min(c0 + 256, dvt)
        v0, g0 = 2 * hdk, 2 * hdk + dvt
        v_ref[:, c0:c1] = jnp.dot(h, w_ref[:, v0 + c0:v0 + c1],
                                  preferred_element_type=F32).astype(v_ref.dtype)
        g = jnp.dot(h, w_ref[:, g0 + c0:g0 + c1], preferred_element_type=F32)
        sg_ref[:, c0:c1] = _silu(g).astype(sg_ref.dtype)


def _retproj(h, w_all, layer, cos, sin, chunk, tm=ROW_TILE):
    bsz, s, d = h.shape
    hdk = d
    hdv = (w_all.shape[2] - 2 * hdk) // 2
    dk = hdk // RET_HEADS
    half = dk // 2
    tm = min(tm, s)
    row = lambda b, i: (b, i, 0)
    return pl.pallas_call(
        functools.partial(_retproj_kernel, dk=dk, chunk=chunk),
        grid=(bsz, s // tm),
        in_specs=[
            pl.BlockSpec((None, tm, d), row),
            _resident((None, d, w_all.shape[2]), lambda b, i: (layer, 0, 0)),
            pl.BlockSpec((tm, half), lambda b, i: (i, 0)),
            pl.BlockSpec((tm, half), lambda b, i: (i, 0)),
            pl.BlockSpec((half, tm), lambda b, i: (0, i)),
            pl.BlockSpec((half, tm), lambda b, i: (0, i)),
        ],
        out_specs=[
            pl.BlockSpec((None, tm, hdk), row),
            pl.BlockSpec((None, tm // chunk, hdk, chunk), lambda b, i: (b, i, 0, 0)),
            pl.BlockSpec((None, tm, hdv), row),
            pl.BlockSpec((None, tm, hdv), row),
        ],
        out_shape=[
            jax.ShapeDtypeStruct((bsz, s, hdk), BF16),
            jax.ShapeDtypeStruct((bsz, s // chunk, hdk, chunk), BF16),
            jax.ShapeDtypeStruct((bsz, s, hdv), BF16),
            jax.ShapeDtypeStruct((bsz, s, hdv), BF16),
        ],
        scratch_shapes=[pltpu.VMEM((hdk, d), BF16)],
        compiler_params=_params(2),
        name="ret_proj",
    )(h, w_all, cos, sin, cos.T, sin.T)


def _retcore_kernel(cd_ref, q_ref, kt_ref, v_ref, dm_ref, qdf_ref, qdb_ref, kdf_ref, kdb_ref,
                    y_ref, r_all_ref, rf_ref, rb_ref):
    hd = pl.program_id(1)
    nc, dk, c = kt_ref.shape
    cd_f = cd_ref[0, hd]
    cd_b = cd_ref[1, hd]
    kdf = kdf_ref[...]
    kdb = kdb_ref[...]

    rf_ref[...] = jnp.zeros_like(rf_ref)
    rb_ref[...] = jnp.zeros_like(rb_ref)
    for it in range(nc):
        nf, nb = it, nc - 1 - it
        r_all_ref[nf, :dk, :] = rf_ref[...].astype(r_all_ref.dtype)
        r_all_ref[nb, dk:, :] = rb_ref[...].astype(r_all_ref.dtype)
        if it == nc - 1:
            break
        ktf = (kt_ref[nf].astype(F32) * kdf).astype(BF16)
        ktb = (kt_ref[nb].astype(F32) * kdb).astype(BF16)
        rf_ref[...] = rf_ref[...] * cd_f + jnp.dot(
            ktf, v_ref[nf * c:(nf + 1) * c, :], preferred_element_type=F32)
        rb_ref[...] = rb_ref[...] * cd_b + jnp.dot(
            ktb, v_ref[nb * c:(nb + 1) * c, :], preferred_element_type=F32)

    reps = dk // LANES
    qdf = jnp.tile(qdf_ref[...], (1, reps))
    qdb = jnp.tile(qdb_ref[...], (1, reps))
    for n in range(nc):
        rows = slice(n * c, (n + 1) * c)
        qn = q_ref[rows, :]
        sc = jnp.dot(qn, kt_ref[n], preferred_element_type=F32) * dm_ref[...]
        y = jnp.dot(sc.astype(BF16), v_ref[rows, :], preferred_element_type=F32)
        qf32 = qn.astype(F32)
        qfb = jnp.concatenate([(qf32 * qdf).astype(BF16), (qf32 * qdb).astype(BF16)], axis=1)
        y += jnp.dot(qfb, r_all_ref[n], preferred_element_type=F32)
        mu = jnp.mean(y, axis=-1, keepdims=True)
        yc = y - mu
        var = jnp.mean(yc * yc, axis=-1, keepdims=True)
        y_ref[rows, :] = (yc * lax.rsqrt(var + EPS)).astype(y_ref.dtype)


def _ret_decay_tables(c):
    hidx = jnp.arange(RET_HEADS, dtype=F32)
    lg_f = jnp.log1p(-jnp.exp2(-5.0 - hidx))
    lg_b = jnp.flip(lg_f)
    t = jnp.arange(c, dtype=jnp.int32)
    diff = (t[:, None] - t[None, :])
    fmask = diff >= 0
    bmask = diff < 0
    ef = jnp.exp(jnp.where(fmask, diff, 0).astype(F32)[None] * lg_f[:, None, None])
    eb = jnp.exp(jnp.where(bmask, -diff, 0).astype(F32)[None] * lg_b[:, None, None])
    dm = jnp.where(fmask[None], ef, eb)
    tf = t.astype(F32)
    ones = jnp.ones((1, 1, LANES), F32)
    qdf = jnp.exp((tf + 1.0)[None, :] * lg_f[:, None])[:, :, None] * ones
    qdb = jnp.exp((c - tf)[None, :] * lg_b[:, None])[:, :, None] * ones
    kdf = jnp.exp((c - 1.0 - tf)[None, :] * lg_f[:, None])[:, None, :]
    kdb = jnp.exp(tf[None, :] * lg_b[:, None])[:, None, :]
    cd = jnp.stack([jnp.exp(c * lg_f), jnp.exp(c * lg_b)])
    return cd, dm, qdf, qdb, kdf, kdb


def _retcore(q, kt, v, tables):
    bsz, s, hdk = q.shape
    hdv = v.shape[2]
    dk, dv = hdk // RET_HEADS, hdv // RET_HEADS
    nc, c = kt.shape[1], kt.shape[3]
    cd, dm, qdf, qdb, kdf, kdb = tables
    per_head = lambda b, h: (h, 0, 0)
    return pl.pallas_call(
        _retcore_kernel,
        grid=(bsz, RET_HEADS),
        in_specs=[
            pl.BlockSpec(memory_space=pltpu.SMEM),
            pl.BlockSpec((None, s, dk), lambda b, h: (b, 0, h)),
            pl.BlockSpec((None, nc, dk, c), lambda b, h: (b, 0, h, 0)),
            pl.BlockSpec((None, s, dv), lambda b, h: (b, 0, h)),
            pl.BlockSpec((None, c, c), per_head),
            pl.BlockSpec((None, c, LANES), per_head),
            pl.BlockSpec((None, c, LANES), per_head),
            pl.BlockSpec((None, 1, c), per_head),
            pl.BlockSpec((None, 1, c), per_head),
        ],
        out_specs=pl.BlockSpec((None, s, dv), lambda b, h: (b, 0, h)),
        out_shape=jax.ShapeDtypeStruct((bsz, s, hdv), BF16),
        scratch_shapes=[pltpu.VMEM((nc, 2 * dk, dv), BF16),
                        pltpu.VMEM((dk, dv), F32),
                        pltpu.VMEM((dk, dv), F32)],
        compiler_params=_params(2),
        name="ret_core",
    )(cd, q, kt, v, dm, qdf, qdb, kdf, kdb)


def _mix_ffn_kernel(*refs, fourier, emit):
    n_out = 1 if emit == "final" else 2
    ins = refs[:len(refs) - n_out - 2]
    outs = refs[len(ins):len(ins) + n_out]
    h2_ref, a_ref = refs[len(ins) + n_out:]
    o_ref = outs[0]
    if emit == "x+h_rm":
        ins, perm_ref = ins[:-1], ins[-1]
    x_ref, vec_ref, gain_ref = ins[:3]
    g1, sh2, sc2, g2 = (vec_ref[r:r + 1, :] for r in range(4))
    if fourier:
        yr_ref, yi_ref, cc_ref, sc_ref, wo_ref, win_ref, wout_ref = ins[3:]
    else:
        y_ref, sg_ref, wo_ref, win_ref, wout_ref = ins[3:]
    f = wout_ref.shape[0]
    tm = x_ref.shape[0]
    sub = tm // MIX_SUBTILES
    subtiles = [slice(r0, r0 + sub) for r0 in range(0, tm, sub)]
    for rs in subtiles:
        if fourier:
            gd = cc_ref.shape[0]
            fs = []
            for g0 in range(0, yr_ref.shape[1], gd):
                fg = jnp.dot(yr_ref[rs, g0:g0 + gd], cc_ref[...], preferred_element_type=F32)
                fg += jnp.dot(yi_ref[rs, g0:g0 + gd], sc_ref[...], preferred_element_type=F32)
                fs.append(fg.astype(BF16))
            z = jnp.concatenate(fs, axis=1)
        else:
            z = (sg_ref[rs, :].astype(F32) * y_ref[rs, :].astype(F32)).astype(BF16)
        m = jnp.dot(z, wo_ref[...], preferred_element_type=F32)
        x1 = x_ref[rs, :] + g1 * m
        h2_ref[rs, :] = _norm_mod(x1, gain_ref[0:1, :], sh2, sc2).astype(h2_ref.dtype)
        o_ref[rs, :] = x1
    for rs in subtiles:
        for c0 in range(0, f, FFN_CHUNK):
            gate = jnp.dot(h2_ref[rs, :], win_ref[:, c0:c0 + FFN_CHUNK], preferred_element_type=F32)
            up = jnp.dot(h2_ref[rs, :], win_ref[:, f + c0:f + c0 + FFN_CHUNK],
                         preferred_element_type=F32)
            a_ref[rs, c0:c0 + FFN_CHUNK] = (_silu(gate) * up).astype(a_ref.dtype)
    for rs in subtiles:
        ffn = jnp.dot(a_ref[rs, :], wout_ref[...], preferred_element_type=F32)
        x2 = o_ref[rs, :] + g2 * ffn
        hn = _norm_mod(x2, gain_ref[1:2, :], vec_ref[4:5, :], vec_ref[5:6, :])
        if emit == "final":
            o_ref[rs, :] = hn
            continue
        o_ref[rs, :] = x2
        hn_ref = outs[1]
        if emit == "x+h":
            hn_ref[rs, :] = hn.astype(hn_ref.dtype)
        else:
            _store_residue_major(hn, perm_ref, hn_ref, rs.start)


def _mix_ffn(x, vecs, gains, mix_inputs, mix_consts, wo_all, wo_layer, win_all, wout_all, layer,
             *, fourier, emit, tm=ROW_TILE):
    bsz, s, d = x.shape
    tm = min(tm, s)
    row = lambda b, i: (b, i, 0)
    in_specs = [pl.BlockSpec((None, tm, d), row),
                pl.BlockSpec((None, 8, d), lambda b, i: (b, 0, 0)),
                pl.BlockSpec((8, d), lambda b, i: (0, 0))]
    args = [x, vecs, gains]
    for a in mix_inputs:
        in_specs.append(pl.BlockSpec((None, tm, a.shape[2]), row))
        args.append(a)
    for a in mix_consts:
        in_specs.append(_resident(a.shape, lambda b, i: (0, 0)))
        args.append(a)
    for a, idx in ((wo_all, wo_layer), (win_all, layer), (wout_all, layer)):
        in_specs.append(_resident((None,) + a.shape[1:], lambda b, i, idx=idx: (idx, 0, 0)))
        args.append(a)
    out_specs = [pl.BlockSpec((None, tm, d), row)]
    out_shape = [jax.ShapeDtypeStruct((bsz, s, d), F32)]
    scratch = [pltpu.VMEM((tm, d), BF16), pltpu.VMEM((tm, wout_all.shape[1]), BF16)]
    if emit == "x+h":
        out_specs.append(pl.BlockSpec((None, tm, d), row))
        out_shape.append(jax.ShapeDtypeStruct((bsz, s, d), BF16))
    elif emit == "x+h_rm":
        out_specs.append(pl.BlockSpec((None, DFT_RADIX, tm // DFT_RADIX, d), lambda b, i: (b, 0, i, 0)))
        out_shape.append(jax.ShapeDtypeStruct((bsz, DFT_RADIX, s // DFT_RADIX, d), BF16))
        assert tm % PERM_ROWS == 0
        in_specs.append(_resident((PERM_ROWS, PERM_ROWS), lambda b, i: (0, 0)))
        args.append(_residue_perm())
    return pl.pallas_call(
        functools.partial(_mix_ffn_kernel, fourier=fourier, emit=emit),
        grid=(bsz, s // tm),
        in_specs=in_specs,
        out_specs=out_specs,
        out_shape=out_shape,
        scratch_shapes=scratch,
        compiler_params=_params(2),
        name="mix_ffn_fourier" if fourier else "mix_ffn_ret",
    )(*args)


def _chan_dft_tables(gd, s):
    c = jnp.arange(gd, dtype=jnp.int32)
    j = (c[:, None] * c[None, :]) % gd
    ang = j.astype(F32) * (2.0 * jnp.pi / gd)
    scale = 1.0 / jnp.sqrt(jnp.asarray(s * gd, F32))
    return (jnp.cos(ang) * scale).astype(BF16), (jnp.sin(ang) * scale).astype(BF16)


def _rotary_tables(s, half):
    inv_freq = ROPE_BASE ** (-jnp.arange(half, dtype=F32) / half)
    ang = jnp.arange(s, dtype=F32)[:, None] * inv_freq[None, :]
    return jnp.cos(ang), jnp.sin(ang)


def kernel(x, c, w_ada, b_ada, norm_mix_g, norm_ffn_g, w_fourier_out, w_ret_in, w_ret_out,
           w_ffn_in, w_ffn_out, final_norm_g, w_ada_final, b_ada_final):
    bsz, s, d = x.shape
    depth = w_ada.shape[0]
    gd = d // FOURIER_GROUPS
    dk = d // RET_HEADS
    hdk = RET_HEADS * dk
    chunk = min(RET_CHUNK, s)

    mods = _ada(c, w_ada, b_ada).reshape(depth, bsz, 6, d)
    mod_final = _ada(c, w_ada_final[None], b_ada_final[None]).reshape(bsz, 2, d)

    seq_tab = _seqdft_tables(s)
    cc_tab, sc_tab = _chan_dft_tables(gd, s)
    cos, sin = _rotary_tables(s, dk // 2)
    ret_tabs = _ret_decay_tables(chunk)

    win_all = w_ffn_in.astype(BF16)
    wout_all = w_ffn_out.astype(BF16)
    wfo_all = w_fourier_out.astype(BF16)
    wri_all = w_ret_in.astype(BF16)
    wro_all = w_ret_out.astype(BF16)

    def is_fourier(i):
        return i % N_MIXERS == 0

    h = _prenorm(x, norm_mix_g[0], mods[0, :, 0], mods[0, :, 1], residue_major=is_fourier(0),
                 out_dtype=BF16)
    for i in range(depth):
        last = i == depth - 1
        if last:
            next_mod, next_gain, emit = mod_final, final_norm_g, "final"
        else:
            next_mod, next_gain = mods[i + 1, :, 0:2], norm_mix_g[i + 1]
            emit = "x+h_rm" if is_fourier(i + 1) else "x+h"
        pad = jnp.zeros((bsz, 2, d), F32)
        vecs = jnp.concatenate([mods[i, :, 2:6], next_mod, pad], axis=1)
        gains = jnp.concatenate([norm_ffn_g[i][None], next_gain[None], jnp.zeros((6, d), F32)])
        j = i // N_MIXERS
        if is_fourier(i):
            yr, yi = _seqdft(h, seq_tab)
            outs = _mix_ffn(x, vecs, gains, (yr, yi), (cc_tab, sc_tab), wfo_all, j,
                            win_all, wout_all, i, fourier=True, emit=emit)
        else:
            q, kt, v, sg = _retproj(h, wri_all, j, cos, sin, chunk)
            y = _retcore(q, kt, v, ret_tabs)
            outs = _mix_ffn(x, vecs, gains, (y, sg), (), wro_all, j,
                            win_all, wout_all, i, fourier=False, emit=emit)
        if last:
            return outs[0]
        x, h = outs
```

```python
import functools
import math

import jax
import jax.numpy as jnp
from jax import lax
from jax.experimental import pallas as pl
from jax.experimental.pallas import tpu as pltpu

EPS = 1e-6
N_MIXERS = 2
FOURIER_GROUPS = 4
RET_HEADS = 4
ROPE_BASE = 10000.0

DFT_RADIX = 16
RET_CHUNK = 256
FFN_CHUNK = 256
LANES = 128
ROW_TILE = 512
MIX_SUBTILES = 2
VMEM_LIMIT_BYTES = 56 * 1024 * 1024

F32 = jnp.float32
BF16 = jnp.bfloat16


def _params(n_grid_dims):
    return pltpu.CompilerParams(
        dimension_semantics=("arbitrary",) * n_grid_dims,
        vmem_limit_bytes=VMEM_LIMIT_BYTES)


def _resident(block_shape, index_map):
    return pl.BlockSpec(block_shape, index_map, pipeline_mode=pl.Buffered(1))


def _silu(v):
    return v * jax.nn.sigmoid(v)


def _norm_mod(x, g, shift, scale):
    ms = jnp.mean(x * x, axis=-1, keepdims=True)
    y = x * lax.rsqrt(ms + EPS)
    return (y * g) * (1.0 + scale) + shift


def _ada_kernel(c_ref, w_ref, b_ref, o_ref):
    ca = _silu(c_ref[...])
    w = w_ref[...]
    ca_hi = ca.astype(BF16)
    ca_lo = (ca - ca_hi.astype(F32)).astype(BF16)
    w_hi = w.astype(BF16)
    w_lo = (w - w_hi.astype(F32)).astype(BF16)
    nb = ca.shape[0]
    p = jnp.dot(jnp.concatenate([ca_hi, ca_lo], axis=0), w_hi, preferred_element_type=F32)
    q = jnp.dot(ca_hi, w_lo, preferred_element_type=F32)
    o_ref[...] = (p[:nb] + p[nb:]) + q + b_ref[...]


def _ada(c, w, b, tn=1024):
    n_layers, d, n = w.shape
    bsz = c.shape[0]
    tn = min(tn, n)
    return pl.pallas_call(
        _ada_kernel,
        grid=(n_layers, n // tn),
        in_specs=[
            pl.BlockSpec((bsz, d), lambda l, j: (0, 0)),
            pl.BlockSpec((None, d, tn), lambda l, j: (l, 0, j)),
            pl.BlockSpec((None, 1, tn), lambda l, j: (l, 0, j)),
        ],
        out_specs=pl.BlockSpec((None, bsz, tn), lambda l, j: (l, 0, j)),
        out_shape=jax.ShapeDtypeStruct((n_layers, bsz, n), F32),
        compiler_params=_params(2),
        name="ada_mod",
    )(c, w, b.reshape(n_layers, 1, n))


PERM_ROWS = DFT_RADIX * DFT_RADIX


def _residue_perm():
    r = jnp.arange(PERM_ROWS, dtype=jnp.int32)
    src = (r % DFT_RADIX) * DFT_RADIX + r // DFT_RADIX
    return (src[:, None] == r[None, :]).astype(BF16)


def _store_residue_major(h, p_ref, o_ref, row0=0):
    hb = h.astype(o_ref.dtype)
    for r0 in range(0, h.shape[0], PERM_ROWS):
        p = jnp.dot(p_ref[...], hb[r0:r0 + PERM_ROWS], preferred_element_type=F32).astype(o_ref.dtype)
        j0 = (row0 + r0) // DFT_RADIX
        for s1 in range(DFT_RADIX):
            o_ref[s1, j0:j0 + DFT_RADIX, :] = p[s1 * DFT_RADIX:(s1 + 1) * DFT_RADIX]


def _prenorm_kernel(x_ref, g_ref, sh_ref, sc_ref, *rest, residue_major):
    h = _norm_mod(x_ref[...], g_ref[...], sh_ref[...], sc_ref[...])
    if residue_major:
        p_ref, o_ref = rest
        _store_residue_major(h, p_ref, o_ref)
    else:
        o_ref, = rest
        o_ref[...] = h.astype(o_ref.dtype)


def _prenorm(x, g, shift, scale, *, residue_major, out_dtype, tm=ROW_TILE):
    bsz, s, d = x.shape
    tm = min(tm, s)
    vec = lambda a: a.reshape(bsz, 1, d)
    in_specs = [
        pl.BlockSpec((None, tm, d), lambda b, i: (b, i, 0)),
        pl.BlockSpec((1, d), lambda b, i: (0, 0)),
        pl.BlockSpec((None, 1, d), lambda b, i: (b, 0, 0)),
        pl.BlockSpec((None, 1, d), lambda b, i: (b, 0, 0)),
    ]
    args = [x, g.reshape(1, d), vec(shift), vec(scale)]
    if residue_major:
        assert tm % PERM_ROWS == 0
        in_specs.append(_resident((PERM_ROWS, PERM_ROWS), lambda b, i: (0, 0)))
        args.append(_residue_perm())
        out_shape = jax.ShapeDtypeStruct((bsz, DFT_RADIX, s // DFT_RADIX, d), out_dtype)
        out_spec = pl.BlockSpec((None, DFT_RADIX, tm // DFT_RADIX, d), lambda b, i: (b, 0, i, 0))
    else:
        out_shape = jax.ShapeDtypeStruct((bsz, s, d), out_dtype)
        out_spec = pl.BlockSpec((None, tm, d), lambda b, i: (b, i, 0))
    return pl.pallas_call(
        functools.partial(_prenorm_kernel, residue_major=residue_major),
        grid=(bsz, s // tm),
        in_specs=in_specs,
        out_specs=out_spec,
        out_shape=out_shape,
        compiler_params=_params(2),
        name="prenorm_rm" if residue_major else "prenorm",
    )(*args)


def _dft4(x):
    (x0r, x0i), (x1r, x1i), (x2r, x2i), (x3r, x3i) = x
    t0r, t0i = x0r + x2r, x0i + x2i
    t1r, t1i = x0r - x2r, x0i - x2i
    t2r, t2i = x1r + x3r, x1i + x3i
    t3r, t3i = x1r - x3r, x1i - x3i
    return [(t0r + t2r, t0i + t2i),
            (t1r + t3i, t1i - t3r),
            (t0r - t2r, t0i - t2i),
            (t1r - t3i, t1i + t3r)]


def _twiddle16(z, p):
    zr, zi = z
    p = p % 16
    if p == 0:
        return zr, zi
    if p == 4:
        return zi, -zr
    if p == 8:
        return -zr, -zi
    if p == 12:
        return -zi, zr
    if p % 4 == 2:
        c = math.sqrt(0.5)
        sr = -c if p in (6, 10) else c
        si = -c if p in (2, 6) else c
        add, sub = zr + zi, zr - zi
        real = add * sr if sr == -si else sub * sr
        imag = sub * si if sr == -si else add * si
        return real, imag
    cr = math.cos(2.0 * math.pi * p / 16.0)
    ci = -math.sin(2.0 * math.pi * p / 16.0)
    return zr * cr - zi * ci, zr * ci + zi * cr


DFT_SLAB = 256


def _seqdft_kernel(h_ref, w_ref, yr_ref, yi_ref, *scratch):
    n2 = h_ref.shape[1]
    td = h_ref.shape[2]
    rows = 16
    slabs = [(c0, scratch[2 * i], scratch[2 * i + 1])
             for i, c0 in enumerate(range(0, td, DFT_SLAB))]
    for c0, br_ref, bi_ref in slabs:
        for s1 in range(DFT_RADIX):
            p = jnp.dot(w_ref[s1], h_ref[s1, :, c0:c0 + DFT_SLAB], preferred_element_type=F32)
            br_ref[s1] = p[:n2]
            bi_ref[s1] = p[n2:]

    tiles = [(slice(r, r + rows), slice(l0, l0 + LANES))
             for r in range(0, n2, rows) for l0 in range(0, DFT_SLAB, LANES)]
    for c0, br_ref, bi_ref in slabs:
        for rs, ls in tiles:
            for b in range(4):
                xs = [(br_ref[4 * a + b, rs, ls], bi_ref[4 * a + b, rs, ls]) for a in range(4)]
                us = _dft4(xs)
                for c in range(4):
                    ur, ui = _twiddle16(us[c], b * c)
                    br_ref[4 * c + b, rs, ls] = ur
                    bi_ref[4 * c + b, rs, ls] = ui
        for rs, ls in tiles:
            for c in range(4):
                us = [(br_ref[4 * c + b, rs, ls], bi_ref[4 * c + b, rs, ls]) for b in range(4)]
                ys = _dft4(us)
                for d in range(4):
                    ks = slice((c + 4 * d) * n2 + rs.start, (c + 4 * d) * n2 + rs.stop)
                    cs = slice(c0 + ls.start, c0 + ls.stop)
                    yr_ref[ks, cs] = ys[d][0].astype(yr_ref.dtype)
                    yi_ref[ks, cs] = ys[d][1].astype(yi_ref.dtype)


def _seqdft_tables(s):
    n2 = s // DFT_RADIX
    k2 = jnp.arange(n2, dtype=jnp.int32)
    s1 = jnp.arange(DFT_RADIX, dtype=jnp.int32)
    unit = 2.0 * jnp.pi / s
    a = ((k2[:, None] * (DFT_RADIX * k2[None, :])) % s).astype(F32) * unit
    b = ((s1[:, None] * k2[None, :]) % s).astype(F32) * unit
    ca, sa = jnp.cos(a)[None], jnp.sin(a)[None]
    cb, sb = jnp.cos(b)[:, :, None], jnp.sin(b)[:, :, None]
    wr = ca * cb - sa * sb
    wi = -(sa * cb + ca * sb)
    return jnp.concatenate([wr, wi], axis=1).astype(BF16)


def _seqdft(h_rm, w_tab, td=2 * DFT_SLAB):
    bsz, _, n2, d = h_rm.shape
    s = DFT_RADIX * n2
    td = min(td, d)
    out = jax.ShapeDtypeStruct((bsz, s, d), BF16)
    return pl.pallas_call(
        _seqdft_kernel,
        grid=(bsz, d // td),
        in_specs=[
            pl.BlockSpec((None, DFT_RADIX, n2, td), lambda b, j: (b, 0, 0, j)),
            _resident((DFT_RADIX, 2 * n2, n2), lambda b, j: (0, 0, 0)),
        ],
        out_specs=[pl.BlockSpec((None, s, td), lambda b, j: (b, 0, j)),
                   pl.BlockSpec((None, s, td), lambda b, j: (b, 0, j))],
        out_shape=[out, out],
        scratch_shapes=[pltpu.VMEM((DFT_RADIX, n2, DFT_SLAB), F32)] * (2 * (td // DFT_SLAB)),
        compiler_params=_params(2),
        name="seq_dft",
    )(h_rm, w_tab)


def _retproj_kernel(h_ref, w_ref, cos_ref, sin_ref,
                    q_ref, kt_ref, v_ref, sg_ref, wkt_ref, *, dk, chunk):
    h = h_ref[...]
    half = dk // 2
    tm = h.shape[0]
    hdk = RET_HEADS * dk
    dvt = v_ref.shape[1]

    @pl.when((pl.program_id(0) == 0) & (pl.program_id(1) == 0))
    def _():
        for c0 in range(0, hdk, dk):
            wk = w_ref[:, hdk + c0:hdk + c0 + dk].astype(F32)
            wkt_ref[c0:c0 + dk, :] = wk.T.astype(wkt_ref.dtype)

    cos, sin = cos_ref[...], sin_ref[...]
    cost, sint = cos.T, sin.T
    k_scale = dk ** -0.5
    for hd in range(RET_HEADS):
        c0 = hd * dk
        q = jnp.dot(h, w_ref[:, c0:c0 + dk], preferred_element_type=F32)
        q1, q2 = q[:, :half], q[:, half:]
        q_ref[:, c0:c0 + half] = (q1 * cos - q2 * sin).astype(q_ref.dtype)
        q_ref[:, c0 + half:c0 + dk] = (q1 * sin + q2 * cos).astype(q_ref.dtype)
        kt = lax.dot_general(wkt_ref[c0:c0 + dk, :], h, (((1,), (1,)), ((), ())),
                             preferred_element_type=F32)
        k1, k2 = kt[:half], kt[half:]
        r1 = (k1 * cost - k2 * sint) * k_scale
        r2 = (k1 * sint + k2 * cost) * k_scale
        for j in range(tm // chunk):
            kt_ref[j, c0:c0 + half, :] = r1[:, j * chunk:(j + 1) * chunk].astype(kt_ref.dtype)
            kt_ref[j, c0 + half:c0 + dk, :] = r2[:, j * chunk:(j + 1) * chunk].astype(kt_ref.dtype)
    for c0 in range(0, dvt, 256):
        c1 = min(c0 + 256, dvt)
        v0, g0 = 2 * hdk, 2 * hdk + dvt
        v_ref[:, c0:c1] = jnp.dot(h, w_ref[:, v0 + c0:v0 + c1],
                                  preferred_element_type=F32).astype(v_ref.dtype)
        g = jnp.dot(h, w_ref[:, g0 + c0:g0 + c1], preferred_element_type=F32)
        sg_ref[:, c0:c1] = _silu(g).astype(sg_ref.dtype)


def _retproj(h, w_all, layer, cos, sin, chunk, tm=ROW_TILE):
    bsz, s, d = h.shape
    hdk = d
    hdv = (w_all.shape[2] - 2 * hdk) // 2
    dk = hdk // RET_HEADS
    half = dk // 2
    tm = min(tm, s)
    row = lambda b, i: (b, i, 0)
    return pl.pallas_call(
        functools.partial(_retproj_kernel, dk=dk, chunk=chunk),
        grid=(bsz, s // tm),
        in_specs=[
            pl.BlockSpec((None, tm, d), row),
            _resident((None, d, w_all.shape[2]), lambda b, i: (layer, 0, 0)),
            pl.BlockSpec((tm, half), lambda b, i: (i, 0)),
            pl.BlockSpec((tm, half), lambda b, i: (i, 0)),
        ],
        out_specs=[
            pl.BlockSpec((None, tm, hdk), row),
            pl.BlockSpec((None, tm // chunk, hdk, chunk), lambda b, i: (b, i, 0, 0)),
            pl.BlockSpec((None, tm, hdv), row),
            pl.BlockSpec((None, tm, hdv), row),
        ],
        out_shape=[
            jax.ShapeDtypeStruct((bsz, s, hdk), BF16),
            jax.ShapeDtypeStruct((bsz, s // chunk, hdk, chunk), BF16),
            jax.ShapeDtypeStruct((bsz, s, hdv), BF16),
            jax.ShapeDtypeStruct((bsz, s, hdv), BF16),
        ],
        scratch_shapes=[pltpu.VMEM((hdk, d), BF16)],
        compiler_params=_params(2),
        name="ret_proj",
    )(h, w_all, cos, sin)


def _retcore_kernel(cd_ref, q_ref, kt_ref, v_ref, dm_ref, qdf_ref, qdb_ref, kdf_ref, kdb_ref,
                    y_ref, r_all_ref, rf_ref, rb_ref):
    hd = pl.program_id(1)
    nc, dk, c = kt_ref.shape
    cd_f = cd_ref[0, hd]
    cd_b = cd_ref[1, hd]
    kdf = kdf_ref[...]
    kdb = kdb_ref[...]

    rf_ref[...] = jnp.zeros_like(rf_ref)
    rb_ref[...] = jnp.zeros_like(rb_ref)
    for it in range(nc):
        nf, nb = it, nc - 1 - it
        r_all_ref[nf, :dk, :] = rf_ref[...].astype(r_all_ref.dtype)
        r_all_ref[nb, dk:, :] = rb_ref[...].astype(r_all_ref.dtype)
        if it == nc - 1:
            break
        ktf = (kt_ref[nf].astype(F32) * kdf).astype(BF16)
        ktb = (kt_ref[nb].astype(F32) * kdb).astype(BF16)
        rf_ref[...] = rf_ref[...] * cd_f + jnp.dot(
            ktf, v_ref[nf * c:(nf + 1) * c, :], preferred_element_type=F32)
        rb_ref[...] = rb_ref[...] * cd_b + jnp.dot(
            ktb, v_ref[nb * c:(nb + 1) * c, :], preferred_element_type=F32)

    reps = dk // LANES
    qdf = jnp.tile(qdf_ref[...], (1, reps))
    qdb = jnp.tile(qdb_ref[...], (1, reps))
    for n in range(nc):
        rows = slice(n * c, (n + 1) * c)
        qn = q_ref[rows, :]
        sc = jnp.dot(qn, kt_ref[n], preferred_element_type=F32) * dm_ref[...]
        y = jnp.dot(sc.astype(BF16), v_ref[rows, :], preferred_element_type=F32)
        qf32 = qn.astype(F32)
        qfb = jnp.concatenate([(qf32 * qdf).astype(BF16), (qf32 * qdb).astype(BF16)], axis=1)
        y += jnp.dot(qfb, r_all_ref[n], preferred_element_type=F32)
        mu = jnp.mean(y, axis=-1, keepdims=True)
        yc = y - mu
        var = jnp.mean(yc * yc, axis=-1, keepdims=True)
        y_ref[rows, :] = (yc * lax.rsqrt(var + EPS)).astype(y_ref.dtype)


def _ret_decay_tables(c):
    hidx = jnp.arange(RET_HEADS, dtype=F32)
    lg_f = jnp.log1p(-jnp.exp2(-5.0 - hidx))
    lg_b = jnp.flip(lg_f)
    t = jnp.arange(c, dtype=jnp.int32)
    diff = (t[:, None] - t[None, :])
    fmask = diff >= 0
    bmask = diff < 0
    ef = jnp.exp(jnp.where(fmask, diff, 0).astype(F32)[None] * lg_f[:, None, None])
    eb = jnp.exp(jnp.where(bmask, -diff, 0).astype(F32)[None] * lg_b[:, None, None])
    dm = jnp.where(fmask[None], ef, eb)
    tf = t.astype(F32)
    ones = jnp.ones((1, 1, LANES), F32)
    qdf = jnp.exp((tf + 1.0)[None, :] * lg_f[:, None])[:, :, None] * ones
    qdb = jnp.exp((c - tf)[None, :] * lg_b[:, None])[:, :, None] * ones
    kdf = jnp.exp((c - 1.0 - tf)[None, :] * lg_f[:, None])[:, None, :]
    kdb = jnp.exp(tf[None, :] * lg_b[:, None])[:, None, :]
    cd = jnp.stack([jnp.exp(c * lg_f), jnp.exp(c * lg_b)])
    return cd, dm, qdf, qdb, kdf, kdb


def _retcore(q, kt, v, tables):
    bsz, s, hdk = q.shape
    hdv = v.shape[2]
    dk, dv = hdk // RET_HEADS, hdv // RET_HEADS
    nc, c = kt.shape[1], kt.shape[3]
    cd, dm, qdf, qdb, kdf, kdb = tables
    per_head = lambda b, h: (h, 0, 0)
    return pl.pallas_call(
        _retcore_kernel,
        grid=(bsz, RET_HEADS),
        in_specs=[
            pl.BlockSpec(memory_space=pltpu.SMEM),
            pl.BlockSpec((None, s, dk), lambda b, h: (b, 0, h)),
            pl.BlockSpec((None, nc, dk, c), lambda b, h: (b, 0, h, 0)),
            pl.BlockSpec((None, s, dv), lambda b, h: (b, 0, h)),
            pl.BlockSpec((None, c, c), per_head),
            pl.BlockSpec((None, c, LANES), per_head),
            pl.BlockSpec((None, c, LANES), per_head),
            pl.BlockSpec((None, 1, c), per_head),
            pl.BlockSpec((None, 1, c), per_head),
        ],
        out_specs=pl.BlockSpec((None, s, dv), lambda b, h: (b, 0, h)),
        out_shape=jax.ShapeDtypeStruct((bsz, s, hdv), BF16),
        scratch_shapes=[pltpu.VMEM((nc, 2 * dk, dv), BF16),
                        pltpu.VMEM((dk, dv), F32),
                        pltpu.VMEM((dk, dv), F32)],
        compiler_params=_params(2),
        name="ret_core",
    )(cd, q, kt, v, dm, qdf, qdb, kdf, kdb)


def _mix_ffn_kernel(*refs, fourier, emit):
    n_out = 1 if emit == "final" else 2
    ins = refs[:len(refs) - n_out - 2]
    outs = refs[len(ins):len(ins) + n_out]
    h2_ref, a_ref = refs[len(ins) + n_out:]
    o_ref = outs[0]
    if emit == "x+h_rm":
        ins, perm_ref = ins[:-1], ins[-1]
    x_ref, vec_ref, gain_ref = ins[:3]
    g1, sh2, sc2, g2 = (vec_ref[r:r + 1, :] for r in range(4))
    if fourier:
        yr_ref, yi_ref, cc_ref, sc_ref, wo_ref, win_ref, wout_ref = ins[3:]
    else:
        y_ref, sg_ref, wo_ref, win_ref, wout_ref = ins[3:]
    f = wout_ref.shape[0]
    tm = x_ref.shape[0]
    sub = tm // MIX_SUBTILES
    subtiles = [slice(r0, r0 + sub) for r0 in range(0, tm, sub)]
    for rs in subtiles:
        if fourier:
            gd = cc_ref.shape[0]
            fs = []
            for g0 in range(0, yr_ref.shape[1], gd):
                fg = jnp.dot(yr_ref[rs, g0:g0 + gd], cc_ref[...], preferred_element_type=F32)
                fg += jnp.dot(yi_ref[rs, g0:g0 + gd], sc_ref[...], preferred_element_type=F32)
                fs.append(fg.astype(BF16))
            z = jnp.concatenate(fs, axis=1)
        else:
            z = (sg_ref[rs, :].astype(F32) * y_ref[rs, :].astype(F32)).astype(BF16)
        m = jnp.dot(z, wo_ref[...], preferred_element_type=F32)
        x1 = x_ref[rs, :] + g1 * m
        h2_ref[rs, :] = _norm_mod(x1, gain_ref[0:1, :], sh2, sc2).astype(h2_ref.dtype)
        o_ref[rs, :] = x1
    for rs in subtiles:
        for c0 in range(0, f, FFN_CHUNK):
            gate = jnp.dot(h2_ref[rs, :], win_ref[:, c0:c0 + FFN_CHUNK], preferred_element_type=F32)
            up = jnp.dot(h2_ref[rs, :], win_ref[:, f + c0:f + c0 + FFN_CHUNK],
                         preferred_element_type=F32)
            a_ref[rs, c0:c0 + FFN_CHUNK] = (_silu(gate) * up).astype(a_ref.dtype)
    for rs in subtiles:
        ffn = jnp.dot(a_ref[rs, :], wout_ref[...], preferred_element_type=F32)
        x2 = o_ref[rs, :] + g2 * ffn
        hn = _norm_mod(x2, gain_ref[1:2, :], vec_ref[4:5, :], vec_ref[5:6, :])
        if emit == "final":
            o_ref[rs, :] = hn
            continue
        o_ref[rs, :] = x2
        hn_ref = outs[1]
        if emit == "x+h":
            hn_ref[rs, :] = hn.astype(hn_ref.dtype)
        else:
            _store_residue_major(hn, perm_ref, hn_ref, rs.start)


def _mix_ffn(x, vecs, gains, mix_inputs, mix_consts, wo_all, wo_layer, win_all, wout_all, layer,
             *, fourier, emit, tm=ROW_TILE):
    bsz, s, d = x.shape
    tm = min(tm, s)
    row = lambda b, i: (b, i, 0)
    in_specs = [pl.BlockSpec((None, tm, d), row),
                pl.BlockSpec((None, 8, d), lambda b, i: (b, 0, 0)),
                pl.BlockSpec((8, d), lambda b, i: (0, 0))]
    args = [x, vecs, gains]
    for a in mix_inputs:
        in_specs.append(pl.BlockSpec((None, tm, a.shape[2]), row))
        args.append(a)
    for a in mix_consts:
        in_specs.append(_resident(a.shape, lambda b, i: (0, 0)))
        args.append(a)
    for a, idx in ((wo_all, wo_layer), (win_all, layer), (wout_all, layer)):
        in_specs.append(_resident((None,) + a.shape[1:], lambda b, i, idx=idx: (idx, 0, 0)))
        args.append(a)
    out_specs = [pl.BlockSpec((None, tm, d), row)]
    out_shape = [jax.ShapeDtypeStruct((bsz, s, d), F32)]
    scratch = [pltpu.VMEM((tm, d), BF16), pltpu.VMEM((tm, wout_all.shape[1]), BF16)]
    if emit == "x+h":
        out_specs.append(pl.BlockSpec((None, tm, d), row))
        out_shape.append(jax.ShapeDtypeStruct((bsz, s, d), BF16))
    elif emit == "x+h_rm":
        out_specs.append(pl.BlockSpec((None, DFT_RADIX, tm // DFT_RADIX, d), lambda b, i: (b, 0, i, 0)))
        out_shape.append(jax.ShapeDtypeStruct((bsz, DFT_RADIX, s // DFT_RADIX, d), BF16))
        assert tm % PERM_ROWS == 0
        in_specs.append(_resident((PERM_ROWS, PERM_ROWS), lambda b, i: (0, 0)))
        args.append(_residue_perm())
    return pl.pallas_call(
        functools.partial(_mix_ffn_kernel, fourier=fourier, emit=emit),
        grid=(bsz, s // tm),
        in_specs=in_specs,
        out_specs=out_specs,
        out_shape=out_shape,
        scratch_shapes=scratch,
        compiler_params=_params(2),
        name="mix_ffn_fourier" if fourier else "mix_ffn_ret",
    )(*args)


def _chan_dft_tables(gd, s):
    c = jnp.arange(gd, dtype=jnp.int32)
    j = (c[:, None] * c[None, :]) % gd
    ang = j.astype(F32) * (2.0 * jnp.pi / gd)
    scale = 1.0 / jnp.sqrt(jnp.asarray(s * gd, F32))
    return (jnp.cos(ang) * scale).astype(BF16), (jnp.sin(ang) * scale).astype(BF16)


def _rotary_tables(s, half):
    inv_freq = ROPE_BASE ** (-jnp.arange(half, dtype=F32) / half)
    ang = jnp.arange(s, dtype=F32)[:, None] * inv_freq[None, :]
    return jnp.cos(ang), jnp.sin(ang)


def kernel(x, c, w_ada, b_ada, norm_mix_g, norm_ffn_g, w_fourier_out, w_ret_in, w_ret_out,
           w_ffn_in, w_ffn_out, final_norm_g, w_ada_final, b_ada_final):
    bsz, s, d = x.shape
    depth = w_ada.shape[0]
    gd = d // FOURIER_GROUPS
    dk = d // RET_HEADS
    hdk = RET_HEADS * dk
    chunk = min(RET_CHUNK, s)

    mods = _ada(c, w_ada, b_ada).reshape(depth, bsz, 6, d)
    mod_final = _ada(c, w_ada_final[None], b_ada_final[None]).reshape(bsz, 2, d)

    seq_tab = _seqdft_tables(s)
    cc_tab, sc_tab = _chan_dft_tables(gd, s)
    cos, sin = _rotary_tables(s, dk // 2)
    ret_tabs = _ret_decay_tables(chunk)

    win_all = w_ffn_in.astype(BF16)
    wout_all = w_ffn_out.astype(BF16)
    wfo_all = w_fourier_out.astype(BF16)
    wri_all = w_ret_in.astype(BF16)
    wro_all = w_ret_out.astype(BF16)

    def is_fourier(i):
        return i % N_MIXERS == 0

    h = _prenorm(x, norm_mix_g[0], mods[0, :, 0], mods[0, :, 1], residue_major=is_fourier(0),
                 out_dtype=BF16, tm=2 * ROW_TILE)
    for i in range(depth):
        last = i == depth - 1
        if last:
            next_mod, next_gain, emit = mod_final, final_norm_g, "final"
        else:
            next_mod, next_gain = mods[i + 1, :, 0:2], norm_mix_g[i + 1]
            emit = "x+h_rm" if is_fourier(i + 1) else "x+h"
        pad = jnp.zeros((bsz, 2, d), F32)
        vecs = jnp.concatenate([mods[i, :, 2:6], next_mod, pad], axis=1)
        gains = jnp.concatenate([norm_ffn_g[i][None], next_gain[None], jnp.zeros((6, d), F32)])
        j = i // N_MIXERS
        if is_fourier(i):
            yr, yi = _seqdft(h, seq_tab)
            outs = _mix_ffn(x, vecs, gains, (yr, yi), (cc_tab, sc_tab), wfo_all, j,
                            win_all, wout_all, i, fourier=True, emit=emit)
        else:
            q, kt, v, sg = _retproj(h, wri_all, j, cos, sin, chunk)
            y = _retcore(q, kt, v, ret_tabs)
            outs = _mix_ffn(x, vecs, gains, (y, sg), (), wro_all, j,
                            win_all, wout_all, i, fourier=False, emit=emit)
        if last:
            return outs[0]
        x, h = outs
```

```python
import functools
import math

import jax
import jax.numpy as jnp
from jax import lax
from jax.experimental import pallas as pl
from jax.experimental.pallas import tpu as pltpu

EPS = 1e-6
N_MIXERS = 2
FOURIER_GROUPS = 4
RET_HEADS = 4
ROPE_BASE = 10000.0

DFT_RADIX = 16
RET_CHUNK = 256
FFN_CHUNK = 256
LANES = 128
ROW_TILE = 512
MIX_SUBTILES = 2
VMEM_LIMIT_BYTES = 56 * 1024 * 1024

F32 = jnp.float32
BF16 = jnp.bfloat16


def _params(n_grid_dims):
    return pltpu.CompilerParams(
        dimension_semantics=("arbitrary",) * n_grid_dims,
        vmem_limit_bytes=VMEM_LIMIT_BYTES)


def _resident(block_shape, index_map):
    return pl.BlockSpec(block_shape, index_map, pipeline_mode=pl.Buffered(1))


def _silu(v):
    return v * jax.nn.sigmoid(v)


def _norm_mod(x, g, shift, scale):
    ms = jnp.mean(x * x, axis=-1, keepdims=True)
    y = x * lax.rsqrt(ms + EPS)
    return (y * g) * (1.0 + scale) + shift


def _ada_kernel(c_ref, w_ref, b_ref, o_ref):
    ca = _silu(c_ref[...])
    w = w_ref[...]
    ca_hi = ca.astype(BF16)
    ca_lo = (ca - ca_hi.astype(F32)).astype(BF16)
    w_hi = w.astype(BF16)
    w_lo = (w - w_hi.astype(F32)).astype(BF16)
    nb = ca.shape[0]
    p = jnp.dot(jnp.concatenate([ca_hi, ca_lo], axis=0), w_hi, preferred_element_type=F32)
    q = jnp.dot(ca_hi, w_lo, preferred_element_type=F32)
    o_ref[...] = (p[:nb] + p[nb:]) + q + b_ref[...]


def _ada(c, w, b, tn=1024):
    n_layers, d, n = w.shape
    bsz = c.shape[0]
    tn = min(tn, n)
    return pl.pallas_call(
        _ada_kernel,
        grid=(n_layers, n // tn),
        in_specs=[
            pl.BlockSpec((bsz, d), lambda l, j: (0, 0)),
            pl.BlockSpec((None, d, tn), lambda l, j: (l, 0, j)),
            pl.BlockSpec((None, 1, tn), lambda l, j: (l, 0, j)),
        ],
        out_specs=pl.BlockSpec((None, bsz, tn), lambda l, j: (l, 0, j)),
        out_shape=jax.ShapeDtypeStruct((n_layers, bsz, n), F32),
        compiler_params=_params(2),
        name="ada_mod",
    )(c, w, b.reshape(n_layers, 1, n))


PERM_ROWS = DFT_RADIX * DFT_RADIX


def _residue_perm():
    r = jnp.arange(PERM_ROWS, dtype=jnp.int32)
    src = (r % DFT_RADIX) * DFT_RADIX + r // DFT_RADIX
    return (src[:, None] == r[None, :]).astype(BF16)


def _store_residue_major(h, p_ref, o_ref, row0=0):
    hb = h.astype(o_ref.dtype)
    for r0 in range(0, h.shape[0], PERM_ROWS):
        p = jnp.dot(p_ref[...], hb[r0:r0 + PERM_ROWS], preferred_element_type=F32).astype(o_ref.dtype)
        j0 = (row0 + r0) // DFT_RADIX
        for s1 in range(DFT_RADIX):
            o_ref[s1, j0:j0 + DFT_RADIX, :] = p[s1 * DFT_RADIX:(s1 + 1) * DFT_RADIX]


def _prenorm_kernel(x_ref, g_ref, sh_ref, sc_ref, *rest, residue_major):
    h = _norm_mod(x_ref[...], g_ref[...], sh_ref[...], sc_ref[...])
    if residue_major:
        p_ref, o_ref = rest
        _store_residue_major(h, p_ref, o_ref)
    else:
        o_ref, = rest
        o_ref[...] = h.astype(o_ref.dtype)


def _prenorm(x, g, shift, scale, *, residue_major, out_dtype, tm=ROW_TILE):
    bsz, s, d = x.shape
    tm = min(tm, s)
    vec = lambda a: a.reshape(bsz, 1, d)
    in_specs = [
        pl.BlockSpec((None, tm, d), lambda b, i: (b, i, 0)),
        pl.BlockSpec((1, d), lambda b, i: (0, 0)),
        pl.BlockSpec((None, 1, d), lambda b, i: (b, 0, 0)),
        pl.BlockSpec((None, 1, d), lambda b, i: (b, 0, 0)),
    ]
    args = [x, g.reshape(1, d), vec(shift), vec(scale)]
    if residue_major:
        assert tm % PERM_ROWS == 0
        in_specs.append(_resident((PERM_ROWS, PERM_ROWS), lambda b, i: (0, 0)))
        args.append(_residue_perm())
        out_shape = jax.ShapeDtypeStruct((bsz, DFT_RADIX, s // DFT_RADIX, d), out_dtype)
        out_spec = pl.BlockSpec((None, DFT_RADIX, tm // DFT_RADIX, d), lambda b, i: (b, 0, i, 0))
    else:
        out_shape = jax.ShapeDtypeStruct((bsz, s, d), out_dtype)
        out_spec = pl.BlockSpec((None, tm, d), lambda b, i: (b, i, 0))
    return pl.pallas_call(
        functools.partial(_prenorm_kernel, residue_major=residue_major),
        grid=(bsz, s // tm),
        in_specs=in_specs,
        out_specs=out_spec,
        out_shape=out_shape,
        compiler_params=_params(2),
        name="prenorm_rm" if residue_major else "prenorm",
    )(*args)


def _dft4(x):
    (x0r, x0i), (x1r, x1i), (x2r, x2i), (x3r, x3i) = x
    t0r, t0i = x0r + x2r, x0i + x2i
    t1r, t1i = x0r - x2r, x0i - x2i
    t2r, t2i = x1r + x3r, x1i + x3i
    t3r, t3i = x1r - x3r, x1i - x3i
    return [(t0r + t2r, t0i + t2i),
            (t1r + t3i, t1i - t3r),
            (t0r - t2r, t0i - t2i),
            (t1r - t3i, t1i + t3r)]


def _twiddle16(z, p):
    zr, zi = z
    p = p % 16
    if p == 0:
        return zr, zi
    if p == 4:
        return zi, -zr
    if p == 8:
        return -zr, -zi
    if p == 12:
        return -zi, zr
    if p % 4 == 2:
        c = math.sqrt(0.5)
        sr = -c if p in (6, 10) else c
        si = -c if p in (2, 6) else c
        add, sub = zr + zi, zr - zi
        real = add * sr if sr == -si else sub * sr
        imag = sub * si if sr == -si else add * si
        return real, imag
    cr = math.cos(2.0 * math.pi * p / 16.0)
    ci = -math.sin(2.0 * math.pi * p / 16.0)
    return zr * cr - zi * ci, zr * ci + zi * cr


DFT_SLAB = 256


def _fourier2d_kernel(h_ref, w_ref, cs_ref, f_ref, *scratch):
    n2 = h_ref.shape[1]
    td = h_ref.shape[2]
    rows = 16
    slabs = [(c0,) + tuple(scratch[3 * i:3 * i + 3])
             for i, c0 in enumerate(range(0, td, DFT_SLAB))]
    for c0, br_ref, bi_ref, _ in slabs:
        for s1 in range(DFT_RADIX):
            p = jnp.dot(w_ref[s1], h_ref[s1, :, c0:c0 + DFT_SLAB], preferred_element_type=F32)
            br_ref[s1] = p[:n2]
            bi_ref[s1] = p[n2:]

    tiles = [(slice(r, r + rows), slice(l0, l0 + LANES))
             for r in range(0, n2, rows) for l0 in range(0, DFT_SLAB, LANES)]
    for c0, br_ref, bi_ref, y_ref in slabs:
        for rs, ls in tiles:
            for b in range(4):
                xs = [(br_ref[4 * a + b, rs, ls], bi_ref[4 * a + b, rs, ls]) for a in range(4)]
                us = _dft4(xs)
                for c in range(4):
                    ur, ui = _twiddle16(us[c], b * c)
                    br_ref[4 * c + b, rs, ls] = ur
                    bi_ref[4 * c + b, rs, ls] = ui
        for c in range(4):
            for rs, ls in tiles:
                us = [(br_ref[4 * c + b, rs, ls], bi_ref[4 * c + b, rs, ls]) for b in range(4)]
                ys = _dft4(us)
                for d in range(4):
                    ks = slice((c + 4 * d) * n2 + rs.start, (c + 4 * d) * n2 + rs.stop)
                    y_ref[ks, ls] = ys[d][0].astype(y_ref.dtype)
                    y_ref[ks, DFT_SLAB + ls.start:DFT_SLAB + ls.stop] = ys[d][1].astype(y_ref.dtype)
            for d in range(4):
                ks = slice((c + 4 * d) * n2, (c + 4 * d + 1) * n2)
                f_ref[ks, c0:c0 + DFT_SLAB] = jnp.dot(
                    y_ref[ks, :], cs_ref[...], preferred_element_type=F32).astype(f_ref.dtype)


def _seqdft_tables(s):
    n2 = s // DFT_RADIX
    k2 = jnp.arange(n2, dtype=jnp.int32)
    s1 = jnp.arange(DFT_RADIX, dtype=jnp.int32)
    unit = 2.0 * jnp.pi / s
    a = ((k2[:, None] * (DFT_RADIX * k2[None, :])) % s).astype(F32) * unit
    b = ((s1[:, None] * k2[None, :]) % s).astype(F32) * unit
    ca, sa = jnp.cos(a)[None], jnp.sin(a)[None]
    cb, sb = jnp.cos(b)[:, :, None], jnp.sin(b)[:, :, None]
    wr = ca * cb - sa * sb
    wi = -(sa * cb + ca * sb)
    return jnp.concatenate([wr, wi], axis=1).astype(BF16)


def _chan_dft_table(gd, s):
    c = jnp.arange(gd, dtype=jnp.int32)
    j = (c[:, None] * c[None, :]) % gd
    ang = j.astype(F32) * (2.0 * jnp.pi / gd)
    scale = 1.0 / jnp.sqrt(jnp.asarray(s * gd, F32))
    eye = jnp.eye(DFT_SLAB // gd, dtype=F32)
    blocks = [jnp.kron(eye, t * scale) for t in (jnp.cos(ang), jnp.sin(ang))]
    return jnp.concatenate(blocks, axis=0).astype(BF16)


def _fourier2d(h_rm, w_tab, cs_tab, td=2 * DFT_SLAB):
    bsz, _, n2, d = h_rm.shape
    s = DFT_RADIX * n2
    td = min(td, d)
    slab_scratch = [pltpu.VMEM((DFT_RADIX, n2, DFT_SLAB), F32),
                    pltpu.VMEM((DFT_RADIX, n2, DFT_SLAB), F32),
                    pltpu.VMEM((s, 2 * DFT_SLAB), BF16)]
    return pl.pallas_call(
        _fourier2d_kernel,
        grid=(bsz, d // td),
        in_specs=[
            pl.BlockSpec((None, DFT_RADIX, n2, td), lambda b, j: (b, 0, 0, j)),
            _resident((DFT_RADIX, 2 * n2, n2), lambda b, j: (0, 0, 0)),
            _resident((2 * DFT_SLAB, DFT_SLAB), lambda b, j: (0, 0)),
        ],
        out_specs=pl.BlockSpec((None, s, td), lambda b, j: (b, 0, j)),
        out_shape=jax.ShapeDtypeStruct((bsz, s, d), BF16),
        scratch_shapes=slab_scratch * (td // DFT_SLAB),
        compiler_params=_params(2),
        name="fourier2d",
    )(h_rm, w_tab, cs_tab)


def _retproj_kernel(h_ref, w_ref, cos_ref, sin_ref,
                    q_ref, kt_ref, v_ref, sg_ref, wkt_ref, *, dk, chunk):
    h = h_ref[...]
    half = dk // 2
    tm = h.shape[0]
    hdk = RET_HEADS * dk
    dvt = v_ref.shape[1]

    @pl.when((pl.program_id(0) == 0) & (pl.program_id(1) == 0))
    def _():
        for c0 in range(0, hdk, dk):
            wk = w_ref[:, hdk + c0:hdk + c0 + dk].astype(F32)
            wkt_ref[c0:c0 + dk, :] = wk.T.astype(wkt_ref.dtype)

    cos, sin = cos_ref[...], sin_ref[...]
    cost, sint = cos.T, sin.T
    k_scale = dk ** -0.5
    for hd in range(RET_HEADS):
        c0 = hd * dk
        q = jnp.dot(h, w_ref[:, c0:c0 + dk], preferred_element_type=F32)
        q1, q2 = q[:, :half], q[:, half:]
        q_ref[:, c0:c0 + half] = (q1 * cos - q2 * sin).astype(q_ref.dtype)
        q_ref[:, c0 + half:c0 + dk] = (q1 * sin + q2 * cos).astype(q_ref.dtype)
        kt = lax.dot_general(wkt_ref[c0:c0 + dk, :], h, (((1,), (1,)), ((), ())),
                             preferred_element_type=F32)
        k1, k2 = kt[:half], kt[half:]
        r1 = (k1 * cost - k2 * sint) * k_scale
        r2 = (k1 * sint + k2 * cost) * k_scale
        for j in range(tm // chunk):
            kt_ref[j, c0:c0 + half, :] = r1[:, j * chunk:(j + 1) * chunk].astype(kt_ref.dtype)
            kt_ref[j, c0 + half:c0 + dk, :] = r2[:, j * chunk:(j + 1) * chunk].astype(kt_ref.dtype)
    for c0 in range(0, dvt, 256):
        c1 = min(c0 + 256, dvt)
        v0, g0 = 2 * hdk, 2 * hdk + dvt
        v_ref[:, c0:c1] = jnp.dot(h, w_ref[:, v0 + c0:v0 + c1],
                                  preferred_element_type=F32).astype(v_ref.dtype)
        g = jnp.dot(h, w_ref[:, g0 + c0:g0 + c1], preferred_element_type=F32)
        sg_ref[:, c0:c1] = _silu(g).astype(sg_ref.dtype)


def _retproj(h, w_all, layer, cos, sin, chunk, tm=ROW_TILE):
    bsz, s, d = h.shape
    hdk = d
    hdv = (w_all.shape[2] - 2 * hdk) // 2
    dk = hdk // RET_HEADS
    half = dk // 2
    tm = min(tm, s)
    row = lambda b, i: (b, i, 0)
    return pl.pallas_call(
        functools.partial(_retproj_kernel, dk=dk, chunk=chunk),
        grid=(bsz, s // tm),
        in_specs=[
            pl.BlockSpec((None, tm, d), row),
            _resident((None, d, w_all.shape[2]), lambda b, i: (layer, 0, 0)),
            pl.BlockSpec((tm, half), lambda b, i: (i, 0)),
            pl.BlockSpec((tm, half), lambda b, i: (i, 0)),
        ],
        out_specs=[
            pl.BlockSpec((None, tm, hdk), row),
            pl.BlockSpec((None, tm // chunk, hdk, chunk), lambda b, i: (b, i, 0, 0)),
            pl.BlockSpec((None, tm, hdv), row),
            pl.BlockSpec((None, tm, hdv), row),
        ],
        out_shape=[
            jax.ShapeDtypeStruct((bsz, s, hdk), BF16),
            jax.ShapeDtypeStruct((bsz, s // chunk, hdk, chunk), BF16),
            jax.ShapeDtypeStruct((bsz, s, hdv), BF16),
            jax.ShapeDtypeStruct((bsz, s, hdv), BF16),
        ],
        scratch_shapes=[pltpu.VMEM((hdk, d), BF16)],
        compiler_params=_params(2),
        name="ret_proj",
    )(h, w_all, cos, sin)


def _retcore_kernel(cd_ref, q_ref, kt_ref, v_ref, dm_ref, qdf_ref, qdb_ref, kdf_ref, kdb_ref,
                    y_ref, r_all_ref, rf_ref, rb_ref):
    hd = pl.program_id(1)
    nc, dk, c = kt_ref.shape
    cd_f = cd_ref[0, hd]
    cd_b = cd_ref[1, hd]
    kdf = kdf_ref[...]
    kdb = kdb_ref[...]

    rf_ref[...] = jnp.zeros_like(rf_ref)
    rb_ref[...] = jnp.zeros_like(rb_ref)
    for it in range(nc):
        nf, nb = it, nc - 1 - it
        r_all_ref[nf, :dk, :] = rf_ref[...].astype(r_all_ref.dtype)
        r_all_ref[nb, dk:, :] = rb_ref[...].astype(r_all_ref.dtype)
        if it == nc - 1:
            break
        ktf = (kt_ref[nf].astype(F32) * kdf).astype(BF16)
        ktb = (kt_ref[nb].astype(F32) * kdb).astype(BF16)
        rf_ref[...] = rf_ref[...] * cd_f + jnp.dot(
            ktf, v_ref[nf * c:(nf + 1) * c, :], preferred_element_type=F32)
        rb_ref[...] = rb_ref[...] * cd_b + jnp.dot(
            ktb, v_ref[nb * c:(nb + 1) * c, :], preferred_element_type=F32)

    reps = dk // LANES
    qdf = jnp.tile(qdf_ref[...], (1, reps))
    qdb = jnp.tile(qdb_ref[...], (1, reps))
    for n in range(nc):
        rows = slice(n * c, (n + 1) * c)
        qn = q_ref[rows, :]
        sc = jnp.dot(qn, kt_ref[n], preferred_element_type=F32) * dm_ref[...]
        y = jnp.dot(sc.astype(BF16), v_ref[rows, :], preferred_element_type=F32)
        qf32 = qn.astype(F32)
        qfb = jnp.concatenate([(qf32 * qdf).astype(BF16), (qf32 * qdb).astype(BF16)], axis=1)
        y += jnp.dot(qfb, r_all_ref[n], preferred_element_type=F32)
        mu = jnp.mean(y, axis=-1, keepdims=True)
        yc = y - mu
        var = jnp.mean(yc * yc, axis=-1, keepdims=True)
        y_ref[rows, :] = (yc * lax.rsqrt(var + EPS)).astype(y_ref.dtype)


def _ret_decay_tables(c):
    hidx = jnp.arange(RET_HEADS, dtype=F32)
    lg_f = jnp.log1p(-jnp.exp2(-5.0 - hidx))
    lg_b = jnp.flip(lg_f)
    t = jnp.arange(c, dtype=jnp.int32)
    diff = (t[:, None] - t[None, :])
    fmask = diff >= 0
    bmask = diff < 0
    ef = jnp.exp(jnp.where(fmask, diff, 0).astype(F32)[None] * lg_f[:, None, None])
    eb = jnp.exp(jnp.where(bmask, -diff, 0).astype(F32)[None] * lg_b[:, None, None])
    dm = jnp.where(fmask[None], ef, eb)
    tf = t.astype(F32)
    ones = jnp.ones((1, 1, LANES), F32)
    qdf = jnp.exp((tf + 1.0)[None, :] * lg_f[:, None])[:, :, None] * ones
    qdb = jnp.exp((c - tf)[None, :] * lg_b[:, None])[:, :, None] * ones
    kdf = jnp.exp((c - 1.0 - tf)[None, :] * lg_f[:, None])[:, None, :]
    kdb = jnp.exp(tf[None, :] * lg_b[:, None])[:, None, :]
    cd = jnp.stack([jnp.exp(c * lg_f), jnp.exp(c * lg_b)])
    return cd, dm, qdf, qdb, kdf, kdb


def _retcore(q, kt, v, tables):
    bsz, s, hdk = q.shape
    hdv = v.shape[2]
    dk, dv = hdk // RET_HEADS, hdv // RET_HEADS
    nc, c = kt.shape[1], kt.shape[3]
    cd, dm, qdf, qdb, kdf, kdb = tables
    per_head = lambda b, h: (h, 0, 0)
    return pl.pallas_call(
        _retcore_kernel,
        grid=(bsz, RET_HEADS),
        in_specs=[
            pl.BlockSpec(memory_space=pltpu.SMEM),
            pl.BlockSpec((None, s, dk), lambda b, h: (b, 0, h)),
            pl.BlockSpec((None, nc, dk, c), lambda b, h: (b, 0, h, 0)),
            pl.BlockSpec((None, s, dv), lambda b, h: (b, 0, h)),
            pl.BlockSpec((None, c, c), per_head),
            pl.BlockSpec((None, c, LANES), per_head),
            pl.BlockSpec((None, c, LANES), per_head),
            pl.BlockSpec((None, 1, c), per_head),
            pl.BlockSpec((None, 1, c), per_head),
        ],
        out_specs=pl.BlockSpec((None, s, dv), lambda b, h: (b, 0, h)),
        out_shape=jax.ShapeDtypeStruct((bsz, s, hdv), BF16),
        scratch_shapes=[pltpu.VMEM((nc, 2 * dk, dv), BF16),
                        pltpu.VMEM((dk, dv), F32),
                        pltpu.VMEM((dk, dv), F32)],
        compiler_params=_params(2),
        name="ret_core",
    )(cd, q, kt, v, dm, qdf, qdb, kdf, kdb)


def _mix_ffn_kernel(*refs, fourier, emit):
    n_out = 1 if emit == "final" else 2
    ins = refs[:len(refs) - n_out - 2]
    outs = refs[len(ins):len(ins) + n_out]
    h2_ref, a_ref = refs[len(ins) + n_out:]
    o_ref = outs[0]
    if emit == "x+h_rm":
        ins, perm_ref = ins[:-1], ins[-1]
    x_ref, vec_ref, gain_ref = ins[:3]
    g1, sh2, sc2, g2 = (vec_ref[r:r + 1, :] for r in range(4))
    if fourier:
        f_ref, wo_ref, win_ref, wout_ref = ins[3:]
    else:
        y_ref, sg_ref, wo_ref, win_ref, wout_ref = ins[3:]
    f = wout_ref.shape[0]
    tm = x_ref.shape[0]
    sub = tm // MIX_SUBTILES
    subtiles = [slice(r0, r0 + sub) for r0 in range(0, tm, sub)]
    for rs in subtiles:
        if fourier:
            z = f_ref[rs, :]
        else:
            z = (sg_ref[rs, :].astype(F32) * y_ref[rs, :].astype(F32)).astype(BF16)
        m = jnp.dot(z, wo_ref[...], preferred_element_type=F32)
        x1 = x_ref[rs, :] + g1 * m
        h2_ref[rs, :] = _norm_mod(x1, gain_ref[0:1, :], sh2, sc2).astype(h2_ref.dtype)
        o_ref[rs, :] = x1
    for rs in subtiles:
        for c0 in range(0, f, FFN_CHUNK):
            gate = jnp.dot(h2_ref[rs, :], win_ref[:, c0:c0 + FFN_CHUNK], preferred_element_type=F32)
            up = jnp.dot(h2_ref[rs, :], win_ref[:, f + c0:f + c0 + FFN_CHUNK],
                         preferred_element_type=F32)
            a_ref[rs, c0:c0 + FFN_CHUNK] = (_silu(gate) * up).astype(a_ref.dtype)
    for rs in subtiles:
        ffn = jnp.dot(a_ref[rs, :], wout_ref[...], preferred_element_type=F32)
        x2 = o_ref[rs, :] + g2 * ffn
        hn = _norm_mod(x2, gain_ref[1:2, :], vec_ref[4:5, :], vec_ref[5:6, :])
        if emit == "final":
            o_ref[rs, :] = hn
            continue
        o_ref[rs, :] = x2
        hn_ref = outs[1]
        if emit == "x+h":
            hn_ref[rs, :] = hn.astype(hn_ref.dtype)
        else:
            _store_residue_major(hn, perm_ref, hn_ref, rs.start)


def _mix_ffn(x, vecs, gains, mix_inputs, wo_all, wo_layer, win_all, wout_all, layer,
             *, fourier, emit, tm=ROW_TILE):
    bsz, s, d = x.shape
    tm = min(tm, s)
    row = lambda b, i: (b, i, 0)
    in_specs = [pl.BlockSpec((None, tm, d), row),
                pl.BlockSpec((None, 8, d), lambda b, i: (b, 0, 0)),
                pl.BlockSpec((8, d), lambda b, i: (0, 0))]
    args = [x, vecs, gains]
    for a in mix_inputs:
        in_specs.append(pl.BlockSpec((None, tm, a.shape[2]), row))
        args.append(a)
    for a, idx in ((wo_all, wo_layer), (win_all, layer), (wout_all, layer)):
        in_specs.append(_resident((None,) + a.shape[1:], lambda b, i, idx=idx: (idx, 0, 0)))
        args.append(a)
    out_specs = [pl.BlockSpec((None, tm, d), row)]
    out_shape = [jax.ShapeDtypeStruct((bsz, s, d), F32)]
    scratch = [pltpu.VMEM((tm, d), BF16), pltpu.VMEM((tm, wout_all.shape[1]), BF16)]
    if emit == "x+h":
        out_specs.append(pl.BlockSpec((None, tm, d), row))
        out_shape.append(jax.ShapeDtypeStruct((bsz, s, d), BF16))
    elif emit == "x+h_rm":
        out_specs.append(pl.BlockSpec((None, DFT_RADIX, tm // DFT_RADIX, d), lambda b, i: (b, 0, i, 0)))
        out_shape.append(jax.ShapeDtypeStruct((bsz, DFT_RADIX, s // DFT_RADIX, d), BF16))
        assert tm % PERM_ROWS == 0
        in_specs.append(_resident((PERM_ROWS, PERM_ROWS), lambda b, i: (0, 0)))
        args.append(_residue_perm())
    return pl.pallas_call(
        functools.partial(_mix_ffn_kernel, fourier=fourier, emit=emit),
        grid=(bsz, s // tm),
        in_specs=in_specs,
        out_specs=out_specs,
        out_shape=out_shape,
        scratch_shapes=scratch,
        compiler_params=_params(2),
        name="mix_ffn_fourier" if fourier else "mix_ffn_ret",
    )(*args)


def _rotary_tables(s, half):
    inv_freq = ROPE_BASE ** (-jnp.arange(half, dtype=F32) / half)
    ang = jnp.arange(s, dtype=F32)[:, None] * inv_freq[None, :]
    return jnp.cos(ang), jnp.sin(ang)


def kernel(x, c, w_ada, b_ada, norm_mix_g, norm_ffn_g, w_fourier_out, w_ret_in, w_ret_out,
           w_ffn_in, w_ffn_out, final_norm_g, w_ada_final, b_ada_final):
    bsz, s, d = x.shape
    depth = w_ada.shape[0]
    gd = d // FOURIER_GROUPS
    dk = d // RET_HEADS
    hdk = RET_HEADS * dk
    chunk = min(RET_CHUNK, s)

    mods = _ada(c, w_ada, b_ada).reshape(depth, bsz, 6, d)
    mod_final = _ada(c, w_ada_final[None], b_ada_final[None]).reshape(bsz, 2, d)

    seq_tab = _seqdft_tables(s)
    cs_tab = _chan_dft_table(gd, s)
    cos, sin = _rotary_tables(s, dk // 2)
    ret_tabs = _ret_decay_tables(chunk)

    win_all = w_ffn_in.astype(BF16)
    wout_all = w_ffn_out.astype(BF16)
    wfo_all = w_fourier_out.astype(BF16)
    wri_all = w_ret_in.astype(BF16)
    wro_all = w_ret_out.astype(BF16)

    def is_fourier(i):
        return i % N_MIXERS == 0

    h = _prenorm(x, norm_mix_g[0], mods[0, :, 0], mods[0, :, 1], residue_major=is_fourier(0),
                 out_dtype=BF16, tm=2 * ROW_TILE)
    for i in range(depth):
        last = i == depth - 1
        if last:
            next_mod, next_gain, emit = mod_final, final_norm_g, "final"
        else:
            next_mod, next_gain = mods[i + 1, :, 0:2], norm_mix_g[i + 1]
            emit = "x+h_rm" if is_fourier(i + 1) else "x+h"
        pad = jnp.zeros((bsz, 2, d), F32)
        vecs = jnp.concatenate([mods[i, :, 2:6], next_mod, pad], axis=1)
        gains = jnp.concatenate([norm_ffn_g[i][None], next_gain[None], jnp.zeros((6, d), F32)])
        j = i // N_MIXERS
        if is_fourier(i):
            f = _fourier2d(h, seq_tab, cs_tab)
            outs = _mix_ffn(x, vecs, gains, (f,), wfo_all, j,
                            win_all, wout_all, i, fourier=True, emit=emit)
        else:
            q, kt, v, sg = _retproj(h, wri_all, j, cos, sin, chunk, tm=2 * ROW_TILE)
            y = _retcore(q, kt, v, ret_tabs)
            outs = _mix_ffn(x, vecs, gains, (y, sg), wro_all, j,
                            win_all, wout_all, i, fourier=False, emit=emit)
        if last:
            return outs[0]
        x, h = outs
```

```python
import functools
import math

import jax
import jax.numpy as jnp
from jax import lax
from jax.experimental import pallas as pl
from jax.experimental.pallas import tpu as pltpu

EPS = 1e-6
N_MIXERS = 2
FOURIER_GROUPS = 4
RET_HEADS = 4
ROPE_BASE = 10000.0

DFT_RADIX = 16
RET_CHUNK = 256
FFN_CHUNK = 256
LANES = 128
ROW_TILE = 512
FOURIER_STEP_COLS = 512
MIX_SUBTILES = 2
VMEM_LIMIT_BYTES = 56 * 1024 * 1024

F32 = jnp.float32
BF16 = jnp.bfloat16


def _params(n_grid_dims):
    return pltpu.CompilerParams(
        dimension_semantics=("arbitrary",) * n_grid_dims,
        vmem_limit_bytes=VMEM_LIMIT_BYTES)


def _resident(block_shape, index_map):
    return pl.BlockSpec(block_shape, index_map, pipeline_mode=pl.Buffered(1))


def _silu(v):
    return v * jax.nn.sigmoid(v)


def _norm_mod(x, g, shift, scale):
    ms = jnp.mean(x * x, axis=-1, keepdims=True)
    y = x * lax.rsqrt(ms + EPS)
    return (y * g) * (1.0 + scale) + shift


def _ada_kernel(c_ref, w_ref, b_ref, o_ref):
    ca = _silu(c_ref[...])
    w = w_ref[...]
    ca_hi = ca.astype(BF16)
    ca_lo = (ca - ca_hi.astype(F32)).astype(BF16)
    w_hi = w.astype(BF16)
    w_lo = (w - w_hi.astype(F32)).astype(BF16)
    nb = ca.shape[0]
    p = jnp.dot(jnp.concatenate([ca_hi, ca_lo], axis=0), w_hi, preferred_element_type=F32)
    q = jnp.dot(ca_hi, w_lo, preferred_element_type=F32)
    o_ref[...] = (p[:nb] + p[nb:]) + q + b_ref[...]


def _ada(c, w, b, tn=1024):
    n_layers, d, n = w.shape
    bsz = c.shape[0]
    tn = min(tn, n)
    return pl.pallas_call(
        _ada_kernel,
        grid=(n_layers, n // tn),
        in_specs=[
            pl.BlockSpec((bsz, d), lambda l, j: (0, 0)),
            pl.BlockSpec((None, d, tn), lambda l, j: (l, 0, j)),
            pl.BlockSpec((None, 1, tn), lambda l, j: (l, 0, j)),
        ],
        out_specs=pl.BlockSpec((None, bsz, tn), lambda l, j: (l, 0, j)),
        out_shape=jax.ShapeDtypeStruct((n_layers, bsz, n), F32),
        compiler_params=_params(2),
        name="ada_mod",
    )(c, w, b.reshape(n_layers, 1, n))


PERM_ROWS = DFT_RADIX * DFT_RADIX


def _residue_perm():
    r = jnp.arange(PERM_ROWS, dtype=jnp.int32)
    src = (r % DFT_RADIX) * DFT_RADIX + r // DFT_RADIX
    return (src[:, None] == r[None, :]).astype(BF16)


def _store_residue_major(h, p_ref, o_ref, row0=0):
    hb = h.astype(o_ref.dtype)
    slab = o_ref.shape[3]
    for r0 in range(0, h.shape[0], PERM_ROWS):
        p = jnp.dot(p_ref[...], hb[r0:r0 + PERM_ROWS], preferred_element_type=F32).astype(o_ref.dtype)
        j0 = (row0 + r0) // DFT_RADIX
        for k in range(o_ref.shape[0]):
            for s1 in range(DFT_RADIX):
                o_ref[k, s1, j0:j0 + DFT_RADIX, :] = (
                    p[s1 * DFT_RADIX:(s1 + 1) * DFT_RADIX, k * slab:(k + 1) * slab])


def _rm_shape(bsz, s, d):
    slab = min(FOURIER_STEP_COLS, d)
    return (bsz, d // slab, DFT_RADIX, s // DFT_RADIX, slab)


def _rm_out_spec(tm, d):
    slab = min(FOURIER_STEP_COLS, d)
    return pl.BlockSpec((None, d // slab, DFT_RADIX, tm // DFT_RADIX, slab),
                        lambda b, i: (b, 0, 0, i, 0))


def _prenorm_kernel(x_ref, g_ref, sh_ref, sc_ref, *rest, residue_major):
    h = _norm_mod(x_ref[...], g_ref[...], sh_ref[...], sc_ref[...])
    if residue_major:
        p_ref, o_ref = rest
        _store_residue_major(h, p_ref, o_ref)
    else:
        o_ref, = rest
        o_ref[...] = h.astype(o_ref.dtype)


def _prenorm(x, g, shift, scale, *, residue_major, out_dtype, tm=ROW_TILE):
    bsz, s, d = x.shape
    tm = min(tm, s)
    vec = lambda a: a.reshape(bsz, 1, d)
    in_specs = [
        pl.BlockSpec((None, tm, d), lambda b, i: (b, i, 0)),
        pl.BlockSpec((1, d), lambda b, i: (0, 0)),
        pl.BlockSpec((None, 1, d), lambda b, i: (b, 0, 0)),
        pl.BlockSpec((None, 1, d), lambda b, i: (b, 0, 0)),
    ]
    args = [x, g.reshape(1, d), vec(shift), vec(scale)]
    if residue_major:
        assert tm % PERM_ROWS == 0
        in_specs.append(_resident((PERM_ROWS, PERM_ROWS), lambda b, i: (0, 0)))
        args.append(_residue_perm())
        out_shape = jax.ShapeDtypeStruct(_rm_shape(bsz, s, d), out_dtype)
        out_spec = _rm_out_spec(tm, d)
    else:
        out_shape = jax.ShapeDtypeStruct((bsz, s, d), out_dtype)
        out_spec = pl.BlockSpec((None, tm, d), lambda b, i: (b, i, 0))
    return pl.pallas_call(
        functools.partial(_prenorm_kernel, residue_major=residue_major),
        grid=(bsz, s // tm),
        in_specs=in_specs,
        out_specs=out_spec,
        out_shape=out_shape,
        compiler_params=_params(2),
        name="prenorm_rm" if residue_major else "prenorm",
    )(*args)


def _dft4(x):
    (x0r, x0i), (x1r, x1i), (x2r, x2i), (x3r, x3i) = x
    t0r, t0i = x0r + x2r, x0i + x2i
    t1r, t1i = x0r - x2r, x0i - x2i
    t2r, t2i = x1r + x3r, x1i + x3i
    t3r, t3i = x1r - x3r, x1i - x3i
    return [(t0r + t2r, t0i + t2i),
            (t1r + t3i, t1i - t3r),
            (t0r - t2r, t0i - t2i),
            (t1r - t3i, t1i + t3r)]


def _twiddle16(z, p):
    zr, zi = z
    p = p % 16
    if p == 0:
        return zr, zi
    if p == 4:
        return zi, -zr
    if p == 8:
        return -zr, -zi
    if p == 12:
        return -zi, zr
    if p % 4 == 2:
        c = math.sqrt(0.5)
        sr = -c if p in (6, 10) else c
        si = -c if p in (2, 6) else c
        add, sub = zr + zi, zr - zi
        real = add * sr if sr == -si else sub * sr
        imag = sub * si if sr == -si else add * si
        return real, imag
    cr = math.cos(2.0 * math.pi * p / 16.0)
    ci = -math.sin(2.0 * math.pi * p / 16.0)
    return zr * cr - zi * ci, zr * ci + zi * cr


DFT_SLAB = 256


def _fourier2d_kernel(h_ref, w_ref, cs_ref, f_ref, *scratch):
    n2 = h_ref.shape[1]
    td = h_ref.shape[2]
    rows = 8
    slabs = [(c0, scratch[2 * i], scratch[2 * i + 1])
             for i, c0 in enumerate(range(0, td, DFT_SLAB))]
    for c0, br_ref, bi_ref in slabs:
        for s1 in range(DFT_RADIX):
            p = jnp.dot(w_ref[s1], h_ref[s1, :, c0:c0 + DFT_SLAB], preferred_element_type=F32)
            br_ref[s1] = p[:n2]
            bi_ref[s1] = p[n2:]

    quarter = max(n2 // 4, rows)
    for c0, br_ref, bi_ref in slabs:
        for q0 in range(0, n2, quarter):
            for r in range(q0, q0 + quarter, rows):
                rs = slice(r, r + rows)
                for l0 in range(0, DFT_SLAB, LANES):
                    ls = slice(l0, l0 + LANES)
                    us = [None] * DFT_RADIX
                    for b in range(4):
                        xs = [(br_ref[4 * a + b, rs, ls], bi_ref[4 * a + b, rs, ls])
                              for a in range(4)]
                        ub = _dft4(xs)
                        for c in range(4):
                            us[4 * c + b] = _twiddle16(ub[c], b * c)
                    for c in range(4):
                        ys = _dft4(us[4 * c:4 * c + 4])
                        for d in range(4):
                            br_ref[c + 4 * d, rs, ls] = ys[d][0]
                            bi_ref[c + 4 * d, rs, ls] = ys[d][1]
            qs = slice(q0, q0 + quarter)
            yq = jnp.concatenate(
                [br_ref[:, qs, :].reshape(DFT_RADIX * quarter, DFT_SLAB).astype(BF16),
                 bi_ref[:, qs, :].reshape(DFT_RADIX * quarter, DFT_SLAB).astype(BF16)], axis=1)
            fq = jnp.dot(yq, cs_ref[...], preferred_element_type=F32).astype(f_ref.dtype)
            for k1 in range(DFT_RADIX):
                f_ref[k1 * n2 + q0:k1 * n2 + q0 + quarter, c0:c0 + DFT_SLAB] = (
                    fq[k1 * quarter:(k1 + 1) * quarter])


def _seqdft_tables(s):
    n2 = s // DFT_RADIX
    k2 = jnp.arange(n2, dtype=jnp.int32)
    s1 = jnp.arange(DFT_RADIX, dtype=jnp.int32)
    unit = 2.0 * jnp.pi / s
    a = ((k2[:, None] * (DFT_RADIX * k2[None, :])) % s).astype(F32) * unit
    b = ((s1[:, None] * k2[None, :]) % s).astype(F32) * unit
    ca, sa = jnp.cos(a)[None], jnp.sin(a)[None]
    cb, sb = jnp.cos(b)[:, :, None], jnp.sin(b)[:, :, None]
    wr = ca * cb - sa * sb
    wi = -(sa * cb + ca * sb)
    return jnp.concatenate([wr, wi], axis=1).astype(BF16)


def _chan_dft_table(gd, s):
    c = jnp.arange(gd, dtype=jnp.int32)
    j = (c[:, None] * c[None, :]) % gd
    ang = j.astype(F32) * (2.0 * jnp.pi / gd)
    scale = 1.0 / jnp.sqrt(jnp.asarray(s * gd, F32))
    eye = jnp.eye(DFT_SLAB // gd, dtype=F32)
    blocks = [jnp.kron(eye, t * scale) for t in (jnp.cos(ang), jnp.sin(ang))]
    return jnp.concatenate(blocks, axis=0).astype(BF16)


def _fourier2d(h_rm, w_tab, cs_tab):
    bsz, n_steps, _, n2, td = h_rm.shape
    s = DFT_RADIX * n2
    slab_scratch = [pltpu.VMEM((DFT_RADIX, n2, DFT_SLAB), F32),
                    pltpu.VMEM((DFT_RADIX, n2, DFT_SLAB), F32)]
    return pl.pallas_call(
        _fourier2d_kernel,
        grid=(bsz, n_steps),
        in_specs=[
            pl.BlockSpec((None, None, DFT_RADIX, n2, td), lambda b, j: (b, j, 0, 0, 0)),
            _resident((DFT_RADIX, 2 * n2, n2), lambda b, j: (0, 0, 0)),
            _resident((2 * DFT_SLAB, DFT_SLAB), lambda b, j: (0, 0)),
        ],
        out_specs=pl.BlockSpec((None, None, s, td), lambda b, j: (b, j, 0, 0)),
        out_shape=jax.ShapeDtypeStruct((bsz, n_steps, s, td), BF16),
        scratch_shapes=slab_scratch * (td // DFT_SLAB),
        compiler_params=_params(2),
        name="fourier2d",
    )(h_rm, w_tab, cs_tab)


def _retproj_kernel(h_ref, w_ref, cos_ref, sin_ref,
                    q_ref, kt_ref, v_ref, sg_ref, wkt_ref, *, dk, chunk):
    h = h_ref[...]
    half = dk // 2
    tm = h.shape[0]
    hdk = RET_HEADS * dk
    dv = v_ref.shape[2]
    dvt = RET_HEADS * dv

    @pl.when((pl.program_id(0) == 0) & (pl.program_id(1) == 0))
    def _():
        for c0 in range(0, hdk, dk):
            wk = w_ref[:, hdk + c0:hdk + c0 + dk].astype(F32)
            wkt_ref[c0:c0 + dk, :] = wk.T.astype(wkt_ref.dtype)

    cos, sin = cos_ref[...], sin_ref[...]
    cost, sint = cos.T, sin.T
    k_scale = dk ** -0.5
    for hd in range(RET_HEADS):
        c0 = hd * dk
        q = jnp.dot(h, w_ref[:, c0:c0 + dk], preferred_element_type=F32)
        q1, q2 = q[:, :half], q[:, half:]
        q_ref[hd, :, :half] = (q1 * cos - q2 * sin).astype(q_ref.dtype)
        q_ref[hd, :, half:] = (q1 * sin + q2 * cos).astype(q_ref.dtype)
        kt = lax.dot_general(wkt_ref[c0:c0 + dk, :], h, (((1,), (1,)), ((), ())),
                             preferred_element_type=F32)
        k1, k2 = kt[:half], kt[half:]
        r1 = (k1 * cost - k2 * sint) * k_scale
        r2 = (k1 * sint + k2 * cost) * k_scale
        for j in range(tm // chunk):
            kt_ref[j, c0:c0 + half, :] = r1[:, j * chunk:(j + 1) * chunk].astype(kt_ref.dtype)
            kt_ref[j, c0 + half:c0 + dk, :] = r2[:, j * chunk:(j + 1) * chunk].astype(kt_ref.dtype)
    step = min(256, dv)
    v0, g0 = 2 * hdk, 2 * hdk + dvt
    for hd in range(RET_HEADS):
        for o in range(0, dv, step):
            c0 = hd * dv + o
            v_ref[hd, :, o:o + step] = jnp.dot(h, w_ref[:, v0 + c0:v0 + c0 + step],
                                               preferred_element_type=F32).astype(v_ref.dtype)
            g = jnp.dot(h, w_ref[:, g0 + c0:g0 + c0 + step], preferred_element_type=F32)
            sg_ref[hd, :, o:o + step] = _silu(g).astype(sg_ref.dtype)


def _retproj(h, w_all, layer, cos, sin, chunk, tm=ROW_TILE):
    bsz, s, d = h.shape
    hdk = d
    hdv = (w_all.shape[2] - 2 * hdk) // 2
    dk = hdk // RET_HEADS
    dv = hdv // RET_HEADS
    half = dk // 2
    tm = min(tm, s)
    row = lambda b, i: (b, i, 0)
    head_row = lambda b, i: (b, 0, i, 0)
    return pl.pallas_call(
        functools.partial(_retproj_kernel, dk=dk, chunk=chunk),
        grid=(bsz, s // tm),
        in_specs=[
            pl.BlockSpec((None, tm, d), row),
            _resident((None, d, w_all.shape[2]), lambda b, i: (layer, 0, 0)),
            pl.BlockSpec((tm, half), lambda b, i: (i, 0)),
            pl.BlockSpec((tm, half), lambda b, i: (i, 0)),
        ],
        out_specs=[
            pl.BlockSpec((None, RET_HEADS, tm, dk), head_row),
            pl.BlockSpec((None, tm // chunk, hdk, chunk), lambda b, i: (b, i, 0, 0)),
            pl.BlockSpec((None, RET_HEADS, tm, dv), head_row),
            pl.BlockSpec((None, RET_HEADS, tm, dv), head_row),
        ],
        out_shape=[
            jax.ShapeDtypeStruct((bsz, RET_HEADS, s, dk), BF16),
            jax.ShapeDtypeStruct((bsz, s // chunk, hdk, chunk), BF16),
            jax.ShapeDtypeStruct((bsz, RET_HEADS, s, dv), BF16),
            jax.ShapeDtypeStruct((bsz, RET_HEADS, s, dv), BF16),
        ],
        scratch_shapes=[pltpu.VMEM((hdk, d), BF16)],
        compiler_params=_params(2),
        name="ret_proj",
    )(h, w_all, cos, sin)


def _retcore_kernel(cd_ref, q_ref, kt_ref, v_ref, dm_ref, qdf_ref, qdb_ref, kdf_ref, kdb_ref,
                    y_ref, r_all_ref, rf_ref, rb_ref):
    hd = pl.program_id(1)
    nc, dk, c = kt_ref.shape
    cd_f = cd_ref[0, hd]
    cd_b = cd_ref[1, hd]
    kdf = kdf_ref[...]
    kdb = kdb_ref[...]

    rf_ref[...] = jnp.zeros_like(rf_ref)
    rb_ref[...] = jnp.zeros_like(rb_ref)
    for it in range(nc):
        nf, nb = it, nc - 1 - it
        r_all_ref[nf, :dk, :] = rf_ref[...].astype(r_all_ref.dtype)
        r_all_ref[nb, dk:, :] = rb_ref[...].astype(r_all_ref.dtype)
        if it == nc - 1:
            break
        ktf = (kt_ref[nf].astype(F32) * kdf).astype(BF16)
        ktb = (kt_ref[nb].astype(F32) * kdb).astype(BF16)
        rf_ref[...] = rf_ref[...] * cd_f + jnp.dot(
            ktf, v_ref[nf * c:(nf + 1) * c, :], preferred_element_type=F32)
        rb_ref[...] = rb_ref[...] * cd_b + jnp.dot(
            ktb, v_ref[nb * c:(nb + 1) * c, :], preferred_element_type=F32)

    reps = dk // LANES
    qdf = jnp.tile(qdf_ref[...], (1, reps))
    qdb = jnp.tile(qdb_ref[...], (1, reps))
    for n in range(nc):
        rows = slice(n * c, (n + 1) * c)
        qn = q_ref[rows, :]
        sc = jnp.dot(qn, kt_ref[n], preferred_element_type=F32) * dm_ref[...]
        y = jnp.dot(sc.astype(BF16), v_ref[rows, :], preferred_element_type=F32)
        qf32 = qn.astype(F32)
        qfb = jnp.concatenate([(qf32 * qdf).astype(BF16), (qf32 * qdb).astype(BF16)], axis=1)
        y += jnp.dot(qfb, r_all_ref[n], preferred_element_type=F32)
        mu = jnp.mean(y, axis=-1, keepdims=True)
        yc = y - mu
        var = jnp.mean(yc * yc, axis=-1, keepdims=True)
        y_ref[rows, :] = (yc * lax.rsqrt(var + EPS)).astype(y_ref.dtype)


def _ret_decay_tables(c):
    hidx = jnp.arange(RET_HEADS, dtype=F32)
    lg_f = jnp.log1p(-jnp.exp2(-5.0 - hidx))
    lg_b = jnp.flip(lg_f)
    t = jnp.arange(c, dtype=jnp.int32)
    diff = (t[:, None] - t[None, :])
    fmask = diff >= 0
    bmask = diff < 0
    ef = jnp.exp(jnp.where(fmask, diff, 0).astype(F32)[None] * lg_f[:, None, None])
    eb = jnp.exp(jnp.where(bmask, -diff, 0).astype(F32)[None] * lg_b[:, None, None])
    dm = jnp.where(fmask[None], ef, eb)
    tf = t.astype(F32)
    ones = jnp.ones((1, 1, LANES), F32)
    qdf = jnp.exp((tf + 1.0)[None, :] * lg_f[:, None])[:, :, None] * ones
    qdb = jnp.exp((c - tf)[None, :] * lg_b[:, None])[:, :, None] * ones
    kdf = jnp.exp((c - 1.0 - tf)[None, :] * lg_f[:, None])[:, None, :]
    kdb = jnp.exp(tf[None, :] * lg_b[:, None])[:, None, :]
    cd = jnp.stack([jnp.exp(c * lg_f), jnp.exp(c * lg_b)])
    return cd, dm, qdf, qdb, kdf, kdb


def _retcore(q, kt, v, tables):
    bsz, _, s, dk = q.shape
    dv = v.shape[3]
    nc, c = kt.shape[1], kt.shape[3]
    cd, dm, qdf, qdb, kdf, kdb = tables
    per_head = lambda b, h: (h, 0, 0)
    head_block = lambda b, h: (b, h, 0, 0)
    return pl.pallas_call(
        _retcore_kernel,
        grid=(bsz, RET_HEADS),
        in_specs=[
            pl.BlockSpec(memory_space=pltpu.SMEM),
            pl.BlockSpec((None, None, s, dk), head_block),
            pl.BlockSpec((None, nc, dk, c), lambda b, h: (b, 0, h, 0)),
            pl.BlockSpec((None, None, s, dv), head_block),
            pl.BlockSpec((None, c, c), per_head),
            pl.BlockSpec((None, c, LANES), per_head),
            pl.BlockSpec((None, c, LANES), per_head),
            pl.BlockSpec((None, 1, c), per_head),
            pl.BlockSpec((None, 1, c), per_head),
        ],
        out_specs=pl.BlockSpec((None, None, s, dv), head_block),
        out_shape=jax.ShapeDtypeStruct((bsz, RET_HEADS, s, dv), BF16),
        scratch_shapes=[pltpu.VMEM((nc, 2 * dk, dv), BF16),
                        pltpu.VMEM((dk, dv), F32),
                        pltpu.VMEM((dk, dv), F32)],
        compiler_params=_params(2),
        name="ret_core",
    )(cd, q, kt, v, dm, qdf, qdb, kdf, kdb)


def _mix_ffn_kernel(*refs, fourier, emit):
    n_out = 1 if emit == "final" else 2
    ins = refs[:len(refs) - n_out - 2]
    outs = refs[len(ins):len(ins) + n_out]
    h2_ref, a_ref = refs[len(ins) + n_out:]
    o_ref = outs[0]
    if emit == "x+h_rm":
        ins, perm_ref = ins[:-1], ins[-1]
    x_ref, vec_ref, gain_ref = ins[:3]
    g1, sh2, sc2, g2 = (vec_ref[r:r + 1, :] for r in range(4))
    if fourier:
        f_ref, wo_ref, win_ref, wout_ref = ins[3:]
    else:
        y_ref, sg_ref, wo_ref, win_ref, wout_ref = ins[3:]
    f = wout_ref.shape[0]
    tm = x_ref.shape[0]
    sub = tm // MIX_SUBTILES
    subtiles = [slice(r0, r0 + sub) for r0 in range(0, tm, sub)]
    for rs in subtiles:
        if fourier:
            z = jnp.concatenate([f_ref[k, rs, :] for k in range(f_ref.shape[0])], axis=1)
        else:
            z = jnp.concatenate(
                [(sg_ref[k, rs, :].astype(F32) * y_ref[k, rs, :].astype(F32)).astype(BF16)
                 for k in range(y_ref.shape[0])], axis=1)
        m = jnp.dot(z, wo_ref[...], preferred_element_type=F32)
        x1 = x_ref[rs, :] + g1 * m
        h2_ref[rs, :] = _norm_mod(x1, gain_ref[0:1, :], sh2, sc2).astype(h2_ref.dtype)
        o_ref[rs, :] = x1
    for rs in subtiles:
        for c0 in range(0, f, FFN_CHUNK):
            gate = jnp.dot(h2_ref[rs, :], win_ref[:, c0:c0 + FFN_CHUNK], preferred_element_type=F32)
            up = jnp.dot(h2_ref[rs, :], win_ref[:, f + c0:f + c0 + FFN_CHUNK],
                         preferred_element_type=F32)
            a_ref[rs, c0:c0 + FFN_CHUNK] = (_silu(gate) * up).astype(a_ref.dtype)
    for rs in subtiles:
        ffn = jnp.dot(a_ref[rs, :], wout_ref[...], preferred_element_type=F32)
        x2 = o_ref[rs, :] + g2 * ffn
        hn = _norm_mod(x2, gain_ref[1:2, :], vec_ref[4:5, :], vec_ref[5:6, :])
        if emit == "final":
            o_ref[rs, :] = hn
            continue
        o_ref[rs, :] = x2
        hn_ref = outs[1]
        if emit == "x+h":
            hn_ref[rs, :] = hn.astype(hn_ref.dtype)
        else:
            _store_residue_major(hn, perm_ref, hn_ref, rs.start)


def _mix_ffn(x, vecs, gains, mix_inputs, wo_all, wo_layer, win_all, wout_all, layer,
             *, fourier, emit, tm=ROW_TILE):
    bsz, s, d = x.shape
    tm = min(tm, s)
    row = lambda b, i: (b, i, 0)
    in_specs = [pl.BlockSpec((None, tm, d), row),
                pl.BlockSpec((None, 8, d), lambda b, i: (b, 0, 0)),
                pl.BlockSpec((8, d), lambda b, i: (0, 0))]
    args = [x, vecs, gains]
    for a in mix_inputs:
        in_specs.append(pl.BlockSpec((None, a.shape[1], tm, a.shape[3]), lambda b, i: (b, 0, i, 0)))
        args.append(a)
    for a, idx in ((wo_all, wo_layer), (win_all, layer), (wout_all, layer)):
        in_specs.append(_resident((None,) + a.shape[1:], lambda b, i, idx=idx: (idx, 0, 0)))
        args.append(a)
    out_specs = [pl.BlockSpec((None, tm, d), row)]
    out_shape = [jax.ShapeDtypeStruct((bsz, s, d), F32)]
    scratch = [pltpu.VMEM((tm, d), BF16), pltpu.VMEM((tm, wout_all.shape[1]), BF16)]
    if emit == "x+h":
        out_specs.append(pl.BlockSpec((None, tm, d), row))
        out_shape.append(jax.ShapeDtypeStruct((bsz, s, d), BF16))
    elif emit == "x+h_rm":
        out_specs.append(_rm_out_spec(tm, d))
        out_shape.append(jax.ShapeDtypeStruct(_rm_shape(bsz, s, d), BF16))
        assert tm % PERM_ROWS == 0
        in_specs.append(_resident((PERM_ROWS, PERM_ROWS), lambda b, i: (0, 0)))
        args.append(_residue_perm())
    return pl.pallas_call(
        functools.partial(_mix_ffn_kernel, fourier=fourier, emit=emit),
        grid=(bsz, s // tm),
        in_specs=in_specs,
        out_specs=out_specs,
        out_shape=out_shape,
        scratch_shapes=scratch,
        compiler_params=_params(2),
        name="mix_ffn_fourier" if fourier else "mix_ffn_ret",
    )(*args)


def _rotary_tables(s, half):
    inv_freq = ROPE_BASE ** (-jnp.arange(half, dtype=F32) / half)
    ang = jnp.arange(s, dtype=F32)[:, None] * inv_freq[None, :]
    return jnp.cos(ang), jnp.sin(ang)


def kernel(x, c, w_ada, b_ada, norm_mix_g, norm_ffn_g, w_fourier_out, w_ret_in, w_ret_out,
           w_ffn_in, w_ffn_out, final_norm_g, w_ada_final, b_ada_final):
    bsz, s, d = x.shape
    depth = w_ada.shape[0]
    gd = d // FOURIER_GROUPS
    dk = d // RET_HEADS
    hdk = RET_HEADS * dk
    chunk = min(RET_CHUNK, s)

    mods = _ada(c, w_ada, b_ada).reshape(depth, bsz, 6, d)
    mod_final = _ada(c, w_ada_final[None], b_ada_final[None]).reshape(bsz, 2, d)

    seq_tab = _seqdft_tables(s)
    cs_tab = _chan_dft_table(gd, s)
    cos, sin = _rotary_tables(s, dk // 2)
    ret_tabs = _ret_decay_tables(chunk)

    win_all = w_ffn_in.astype(BF16)
    wout_all = w_ffn_out.astype(BF16)
    wfo_all = w_fourier_out.astype(BF16)
    wri_all = w_ret_in.astype(BF16)
    wro_all = w_ret_out.astype(BF16)

    def is_fourier(i):
        return i % N_MIXERS == 0

    h = _prenorm(x, norm_mix_g[0], mods[0, :, 0], mods[0, :, 1], residue_major=is_fourier(0),
                 out_dtype=BF16, tm=2 * ROW_TILE)
    for i in range(depth):
        last = i == depth - 1
        if last:
            next_mod, next_gain, emit = mod_final, final_norm_g, "final"
        else:
            next_mod, next_gain = mods[i + 1, :, 0:2], norm_mix_g[i + 1]
            emit = "x+h_rm" if is_fourier(i + 1) else "x+h"
        pad = jnp.zeros((bsz, 2, d), F32)
        vecs = jnp.concatenate([mods[i, :, 2:6], next_mod, pad], axis=1)
        gains = jnp.concatenate([norm_ffn_g[i][None], next_gain[None], jnp.zeros((6, d), F32)])
        j = i // N_MIXERS
        if is_fourier(i):
            f = _fourier2d(h, seq_tab, cs_tab)
            outs = _mix_ffn(x, vecs, gains, (f,), wfo_all, j,
                            win_all, wout_all, i, fourier=True, emit=emit)
        else:
            q, kt, v, sg = _retproj(h, wri_all, j, cos, sin, chunk, tm=2 * ROW_TILE)
            y = _retcore(q, kt, v, ret_tabs)
            outs = _mix_ffn(x, vecs, gains, (y, sg), wro_all, j,
                            win_all, wout_all, i, fourier=False, emit=emit)
        if last:
            return outs[0]
        x, h = outs
```

```python
import functools
import math

import jax
import jax.numpy as jnp
from jax import lax
from jax.experimental import pallas as pl
from jax.experimental.pallas import tpu as pltpu

EPS = 1e-6
N_MIXERS = 2
FOURIER_GROUPS = 4
RET_HEADS = 4
ROPE_BASE = 10000.0

DFT_RADIX = 16
RET_CHUNK = 256
FFN_CHUNK = 256
LANES = 128
ROW_TILE = 512
FOURIER_STEP_COLS = 512
MIX_SUBTILES = 2
MIX_ROUND_ROWS = 512
VMEM_LIMIT_BYTES = 56 * 1024 * 1024

F32 = jnp.float32
BF16 = jnp.bfloat16


def _params(n_grid_dims):
    return pltpu.CompilerParams(
        dimension_semantics=("arbitrary",) * n_grid_dims,
        vmem_limit_bytes=VMEM_LIMIT_BYTES)


def _resident(block_shape, index_map):
    return pl.BlockSpec(block_shape, index_map, pipeline_mode=pl.Buffered(1))


def _silu(v):
    return v * jax.nn.sigmoid(v)


def _norm_mod(x, g, shift, scale):
    ms = jnp.mean(x * x, axis=-1, keepdims=True)
    y = x * lax.rsqrt(ms + EPS)
    return (y * g) * (1.0 + scale) + shift


def _ada_kernel(c_ref, w_ref, b_ref, o_ref):
    ca = _silu(c_ref[...])
    w = w_ref[...]
    ca_hi = ca.astype(BF16)
    ca_lo = (ca - ca_hi.astype(F32)).astype(BF16)
    w_hi = w.astype(BF16)
    w_lo = (w - w_hi.astype(F32)).astype(BF16)
    nb = ca.shape[0]
    p = jnp.dot(jnp.concatenate([ca_hi, ca_lo], axis=0), w_hi, preferred_element_type=F32)
    q = jnp.dot(ca_hi, w_lo, preferred_element_type=F32)
    o_ref[...] = (p[:nb] + p[nb:]) + q + b_ref[...]


def _ada(c, w, b, tn=1024):
    n_layers, d, n = w.shape
    bsz = c.shape[0]
    tn = min(tn, n)
    return pl.pallas_call(
        _ada_kernel,
        grid=(n_layers, n // tn),
        in_specs=[
            pl.BlockSpec((bsz, d), lambda l, j: (0, 0)),
            pl.BlockSpec((None, d, tn), lambda l, j: (l, 0, j)),
            pl.BlockSpec((None, 1, tn), lambda l, j: (l, 0, j)),
        ],
        out_specs=pl.BlockSpec((None, bsz, tn), lambda l, j: (l, 0, j)),
        out_shape=jax.ShapeDtypeStruct((n_layers, bsz, n), F32),
        compiler_params=_params(2),
        name="ada_mod",
    )(c, w, b.reshape(n_layers, 1, n))


PERM_ROWS = DFT_RADIX * DFT_RADIX


def _residue_perm():
    r = jnp.arange(PERM_ROWS, dtype=jnp.int32)
    src = (r % DFT_RADIX) * DFT_RADIX + r // DFT_RADIX
    return (src[:, None] == r[None, :]).astype(BF16)


def _store_residue_major(h, p_ref, o_ref, row0=0):
    hb = h.astype(o_ref.dtype)
    slab = o_ref.shape[3]
    for r0 in range(0, h.shape[0], PERM_ROWS):
        p = jnp.dot(p_ref[...], hb[r0:r0 + PERM_ROWS], preferred_element_type=F32).astype(o_ref.dtype)
        j0 = (row0 + r0) // DFT_RADIX
        for k in range(o_ref.shape[0]):
            for s1 in range(DFT_RADIX):
                o_ref[k, s1, j0:j0 + DFT_RADIX, :] = (
                    p[s1 * DFT_RADIX:(s1 + 1) * DFT_RADIX, k * slab:(k + 1) * slab])


def _rm_shape(bsz, s, d):
    slab = min(FOURIER_STEP_COLS, d)
    return (bsz, d // slab, DFT_RADIX, s // DFT_RADIX, slab)


def _rm_out_spec(tm, d):
    slab = min(FOURIER_STEP_COLS, d)
    return pl.BlockSpec((None, d // slab, DFT_RADIX, tm // DFT_RADIX, slab),
                        lambda b, i: (b, 0, 0, i, 0))


def _prenorm_kernel(x_ref, g_ref, sh_ref, sc_ref, *rest, residue_major):
    h = _norm_mod(x_ref[...], g_ref[...], sh_ref[...], sc_ref[...])
    if residue_major:
        p_ref, o_ref = rest
        _store_residue_major(h, p_ref, o_ref)
    else:
        o_ref, = rest
        o_ref[...] = h.astype(o_ref.dtype)


def _prenorm(x, g, shift, scale, *, residue_major, out_dtype, tm=ROW_TILE):
    bsz, s, d = x.shape
    tm = min(tm, s)
    vec = lambda a: a.reshape(bsz, 1, d)
    in_specs = [
        pl.BlockSpec((None, tm, d), lambda b, i: (b, i, 0)),
        pl.BlockSpec((1, d), lambda b, i: (0, 0)),
        pl.BlockSpec((None, 1, d), lambda b, i: (b, 0, 0)),
        pl.BlockSpec((None, 1, d), lambda b, i: (b, 0, 0)),
    ]
    args = [x, g.reshape(1, d), vec(shift), vec(scale)]
    if residue_major:
        assert tm % PERM_ROWS == 0
        in_specs.append(_resident((PERM_ROWS, PERM_ROWS), lambda b, i: (0, 0)))
        args.append(_residue_perm())
        out_shape = jax.ShapeDtypeStruct(_rm_shape(bsz, s, d), out_dtype)
        out_spec = _rm_out_spec(tm, d)
    else:
        out_shape = jax.ShapeDtypeStruct((bsz, s, d), out_dtype)
        out_spec = pl.BlockSpec((None, tm, d), lambda b, i: (b, i, 0))
    return pl.pallas_call(
        functools.partial(_prenorm_kernel, residue_major=residue_major),
        grid=(bsz, s // tm),
        in_specs=in_specs,
        out_specs=out_spec,
        out_shape=out_shape,
        compiler_params=_params(2),
        name="prenorm_rm" if residue_major else "prenorm",
    )(*args)


def _dft4(x):
    (x0r, x0i), (x1r, x1i), (x2r, x2i), (x3r, x3i) = x
    t0r, t0i = x0r + x2r, x0i + x2i
    t1r, t1i = x0r - x2r, x0i - x2i
    t2r, t2i = x1r + x3r, x1i + x3i
    t3r, t3i = x1r - x3r, x1i - x3i
    return [(t0r + t2r, t0i + t2i),
            (t1r + t3i, t1i - t3r),
            (t0r - t2r, t0i - t2i),
            (t1r - t3i, t1i + t3r)]


def _twiddle16(z, p):
    zr, zi = z
    p = p % 16
    if p == 0:
        return zr, zi
    if p == 4:
        return zi, -zr
    if p == 8:
        return -zr, -zi
    if p == 12:
        return -zi, zr
    if p % 4 == 2:
        c = math.sqrt(0.5)
        sr = -c if p in (6, 10) else c
        si = -c if p in (2, 6) else c
        add, sub = zr + zi, zr - zi
        real = add * sr if sr == -si else sub * sr
        imag = sub * si if sr == -si else add * si
        return real, imag
    cr = math.cos(2.0 * math.pi * p / 16.0)
    ci = -math.sin(2.0 * math.pi * p / 16.0)
    return zr * cr - zi * ci, zr * ci + zi * cr


DFT_SLAB = 256


def _fourier2d_kernel(h_ref, w_ref, cs_ref, f_ref, *scratch):
    n2 = h_ref.shape[1]
    td = h_ref.shape[2]
    rows = 8
    slabs = [(c0, scratch[2 * i], scratch[2 * i + 1])
             for i, c0 in enumerate(range(0, td, DFT_SLAB))]
    for c0, br_ref, bi_ref in slabs:
        for s1 in range(DFT_RADIX):
            p = jnp.dot(w_ref[s1], h_ref[s1, :, c0:c0 + DFT_SLAB], preferred_element_type=F32)
            br_ref[s1] = p[:n2]
            bi_ref[s1] = p[n2:]

    quarter = max(n2 // 4, rows)
    for c0, br_ref, bi_ref in slabs:
        for q0 in range(0, n2, quarter):
            for r in range(q0, q0 + quarter, rows):
                rs = slice(r, r + rows)
                for l0 in range(0, DFT_SLAB, LANES):
                    ls = slice(l0, l0 + LANES)
                    us = [None] * DFT_RADIX
                    for b in range(4):
                        xs = [(br_ref[4 * a + b, rs, ls], bi_ref[4 * a + b, rs, ls])
                              for a in range(4)]
                        ub = _dft4(xs)
                        for c in range(4):
                            us[4 * c + b] = _twiddle16(ub[c], b * c)
                    for c in range(4):
                        ys = _dft4(us[4 * c:4 * c + 4])
                        for d in range(4):
                            br_ref[c + 4 * d, rs, ls] = ys[d][0]
                            bi_ref[c + 4 * d, rs, ls] = ys[d][1]
            qs = slice(q0, q0 + quarter)
            yq = jnp.concatenate(
                [br_ref[:, qs, :].reshape(DFT_RADIX * quarter, DFT_SLAB).astype(BF16),
                 bi_ref[:, qs, :].reshape(DFT_RADIX * quarter, DFT_SLAB).astype(BF16)], axis=1)
            fq = jnp.dot(yq, cs_ref[...], preferred_element_type=F32).astype(f_ref.dtype)
            for k1 in range(DFT_RADIX):
                f_ref[k1 * n2 + q0:k1 * n2 + q0 + quarter, c0:c0 + DFT_SLAB] = (
                    fq[k1 * quarter:(k1 + 1) * quarter])


def _seqdft_tables(s):
    n2 = s // DFT_RADIX
    k2 = jnp.arange(n2, dtype=jnp.int32)
    s1 = jnp.arange(DFT_RADIX, dtype=jnp.int32)
    unit = 2.0 * jnp.pi / s
    a = ((k2[:, None] * (DFT_RADIX * k2[None, :])) % s).astype(F32) * unit
    b = ((s1[:, None] * k2[None, :]) % s).astype(F32) * unit
    ca, sa = jnp.cos(a)[None], jnp.sin(a)[None]
    cb, sb = jnp.cos(b)[:, :, None], jnp.sin(b)[:, :, None]
    wr = ca * cb - sa * sb
    wi = -(sa * cb + ca * sb)
    return jnp.concatenate([wr, wi], axis=1).astype(BF16)


def _chan_dft_table(gd, s):
    c = jnp.arange(gd, dtype=jnp.int32)
    j = (c[:, None] * c[None, :]) % gd
    ang = j.astype(F32) * (2.0 * jnp.pi / gd)
    scale = 1.0 / jnp.sqrt(jnp.asarray(s * gd, F32))
    eye = jnp.eye(DFT_SLAB // gd, dtype=F32)
    blocks = [jnp.kron(eye, t * scale) for t in (jnp.cos(ang), jnp.sin(ang))]
    return jnp.concatenate(blocks, axis=0).astype(BF16)


def _fourier2d(h_rm, w_tab, cs_tab):
    bsz, n_steps, _, n2, td = h_rm.shape
    s = DFT_RADIX * n2
    slab_scratch = [pltpu.VMEM((DFT_RADIX, n2, DFT_SLAB), F32),
                    pltpu.VMEM((DFT_RADIX, n2, DFT_SLAB), F32)]
    return pl.pallas_call(
        _fourier2d_kernel,
        grid=(bsz, n_steps),
        in_specs=[
            pl.BlockSpec((None, None, DFT_RADIX, n2, td), lambda b, j: (b, j, 0, 0, 0)),
            _resident((DFT_RADIX, 2 * n2, n2), lambda b, j: (0, 0, 0)),
            _resident((2 * DFT_SLAB, DFT_SLAB), lambda b, j: (0, 0)),
        ],
        out_specs=pl.BlockSpec((None, None, s, td), lambda b, j: (b, j, 0, 0)),
        out_shape=jax.ShapeDtypeStruct((bsz, n_steps, s, td), BF16),
        scratch_shapes=slab_scratch * (td // DFT_SLAB),
        compiler_params=_params(2),
        name="fourier2d",
    )(h_rm, w_tab, cs_tab)


def _retproj_kernel(h_ref, w_ref, cos_ref, sin_ref,
                    q_ref, kt_ref, v_ref, sg_ref, wkt_ref, *, dk, chunk):
    h = h_ref[...]
    half = dk // 2
    tm = h.shape[0]
    hdk = RET_HEADS * dk
    dv = v_ref.shape[2]
    dvt = RET_HEADS * dv

    @pl.when((pl.program_id(0) == 0) & (pl.program_id(1) == 0))
    def _():
        for c0 in range(0, hdk, dk):
            wk = w_ref[:, hdk + c0:hdk + c0 + dk].astype(F32)
            wkt_ref[c0:c0 + dk, :] = wk.T.astype(wkt_ref.dtype)

    cos, sin = cos_ref[...], sin_ref[...]
    cost, sint = cos.T, sin.T
    k_scale = dk ** -0.5
    for hd in range(RET_HEADS):
        c0 = hd * dk
        q = jnp.dot(h, w_ref[:, c0:c0 + dk], preferred_element_type=F32)
        q1, q2 = q[:, :half], q[:, half:]
        q_ref[hd, :, :half] = (q1 * cos - q2 * sin).astype(q_ref.dtype)
        q_ref[hd, :, half:] = (q1 * sin + q2 * cos).astype(q_ref.dtype)
        kt = lax.dot_general(wkt_ref[c0:c0 + dk, :], h, (((1,), (1,)), ((), ())),
                             preferred_element_type=F32)
        k1, k2 = kt[:half], kt[half:]
        r1 = (k1 * cost - k2 * sint) * k_scale
        r2 = (k1 * sint + k2 * cost) * k_scale
        for j in range(tm // chunk):
            kt_ref[j, c0:c0 + half, :] = r1[:, j * chunk:(j + 1) * chunk].astype(kt_ref.dtype)
            kt_ref[j, c0 + half:c0 + dk, :] = r2[:, j * chunk:(j + 1) * chunk].astype(kt_ref.dtype)
    step = min(256, dv)
    v0, g0 = 2 * hdk, 2 * hdk + dvt
    for hd in range(RET_HEADS):
        for o in range(0, dv, step):
            c0 = hd * dv + o
            v_ref[hd, :, o:o + step] = jnp.dot(h, w_ref[:, v0 + c0:v0 + c0 + step],
                                               preferred_element_type=F32).astype(v_ref.dtype)
            g = jnp.dot(h, w_ref[:, g0 + c0:g0 + c0 + step], preferred_element_type=F32)
            sg_ref[hd, :, o:o + step] = _silu(g).astype(sg_ref.dtype)


def _retproj(h, w_all, layer, cos, sin, chunk, tm=ROW_TILE):
    bsz, s, d = h.shape
    hdk = d
    hdv = (w_all.shape[2] - 2 * hdk) // 2
    dk = hdk // RET_HEADS
    dv = hdv // RET_HEADS
    half = dk // 2
    tm = min(tm, s)
    row = lambda b, i: (b, i, 0)
    head_row = lambda b, i: (b, 0, i, 0)
    return pl.pallas_call(
        functools.partial(_retproj_kernel, dk=dk, chunk=chunk),
        grid=(bsz, s // tm),
        in_specs=[
            pl.BlockSpec((None, tm, d), row),
            _resident((None, d, w_all.shape[2]), lambda b, i: (layer, 0, 0)),
            pl.BlockSpec((tm, half), lambda b, i: (i, 0)),
            pl.BlockSpec((tm, half), lambda b, i: (i, 0)),
        ],
        out_specs=[
            pl.BlockSpec((None, RET_HEADS, tm, dk), head_row),
            pl.BlockSpec((None, tm // chunk, hdk, chunk), lambda b, i: (b, i, 0, 0)),
            pl.BlockSpec((None, RET_HEADS, tm, dv), head_row),
            pl.BlockSpec((None, RET_HEADS, tm, dv), head_row),
        ],
        out_shape=[
            jax.ShapeDtypeStruct((bsz, RET_HEADS, s, dk), BF16),
            jax.ShapeDtypeStruct((bsz, s // chunk, hdk, chunk), BF16),
            jax.ShapeDtypeStruct((bsz, RET_HEADS, s, dv), BF16),
            jax.ShapeDtypeStruct((bsz, RET_HEADS, s, dv), BF16),
        ],
        scratch_shapes=[pltpu.VMEM((hdk, d), BF16)],
        compiler_params=_params(2),
        name="ret_proj",
    )(h, w_all, cos, sin)


def _retcore_kernel(cd_ref, q_ref, kt_ref, v_ref, sg_ref, dm_ref, qdf_ref, qdb_ref, kdf_ref, kdb_ref,
                    z_ref, r_all_ref, rf_ref, rb_ref):
    hd = pl.program_id(1)
    nc, dk, c = kt_ref.shape
    cd_f = cd_ref[0, hd]
    cd_b = cd_ref[1, hd]
    kdf = kdf_ref[...]
    kdb = kdb_ref[...]

    rf_ref[...] = jnp.zeros_like(rf_ref)
    rb_ref[...] = jnp.zeros_like(rb_ref)
    for it in range(nc):
        nf, nb = it, nc - 1 - it
        r_all_ref[nf, :dk, :] = rf_ref[...].astype(r_all_ref.dtype)
        r_all_ref[nb, dk:, :] = rb_ref[...].astype(r_all_ref.dtype)
        if it == nc - 1:
            break
        ktf = (kt_ref[nf].astype(F32) * kdf).astype(BF16)
        ktb = (kt_ref[nb].astype(F32) * kdb).astype(BF16)
        rf_ref[...] = rf_ref[...] * cd_f + jnp.dot(
            ktf, v_ref[nf * c:(nf + 1) * c, :], preferred_element_type=F32)
        rb_ref[...] = rb_ref[...] * cd_b + jnp.dot(
            ktb, v_ref[nb * c:(nb + 1) * c, :], preferred_element_type=F32)

    reps = dk // LANES
    qdf = jnp.tile(qdf_ref[...], (1, reps))
    qdb = jnp.tile(qdb_ref[...], (1, reps))
    for n in range(nc):
        rows = slice(n * c, (n + 1) * c)
        qn = q_ref[rows, :]
        sc = jnp.dot(qn, kt_ref[n], preferred_element_type=F32) * dm_ref[...]
        y = jnp.dot(sc.astype(BF16), v_ref[rows, :], preferred_element_type=F32)
        qf32 = qn.astype(F32)
        qfb = jnp.concatenate([(qf32 * qdf).astype(BF16), (qf32 * qdb).astype(BF16)], axis=1)
        y += jnp.dot(qfb, r_all_ref[n], preferred_element_type=F32)
        mu = jnp.mean(y, axis=-1, keepdims=True)
        yc = y - mu
        var = jnp.mean(yc * yc, axis=-1, keepdims=True)
        yn = yc * lax.rsqrt(var + EPS)
        z_ref[rows, :] = (sg_ref[rows, :].astype(F32) * yn).astype(z_ref.dtype)


def _ret_decay_tables(c):
    hidx = jnp.arange(RET_HEADS, dtype=F32)
    lg_f = jnp.log1p(-jnp.exp2(-5.0 - hidx))
    lg_b = jnp.flip(lg_f)
    t = jnp.arange(c, dtype=jnp.int32)
    diff = (t[:, None] - t[None, :])
    fmask = diff >= 0
    bmask = diff < 0
    ef = jnp.exp(jnp.where(fmask, diff, 0).astype(F32)[None] * lg_f[:, None, None])
    eb = jnp.exp(jnp.where(bmask, -diff, 0).astype(F32)[None] * lg_b[:, None, None])
    dm = jnp.where(fmask[None], ef, eb)
    tf = t.astype(F32)
    ones = jnp.ones((1, 1, LANES), F32)
    qdf = jnp.exp((tf + 1.0)[None, :] * lg_f[:, None])[:, :, None] * ones
    qdb = jnp.exp((c - tf)[None, :] * lg_b[:, None])[:, :, None] * ones
    kdf = jnp.exp((c - 1.0 - tf)[None, :] * lg_f[:, None])[:, None, :]
    kdb = jnp.exp(tf[None, :] * lg_b[:, None])[:, None, :]
    cd = jnp.stack([jnp.exp(c * lg_f), jnp.exp(c * lg_b)])
    return cd, dm, qdf, qdb, kdf, kdb


def _retcore(q, kt, v, sg, tables):
    bsz, _, s, dk = q.shape
    dv = v.shape[3]
    nc, c = kt.shape[1], kt.shape[3]
    cd, dm, qdf, qdb, kdf, kdb = tables
    per_head = lambda b, h: (h, 0, 0)
    head_block = lambda b, h: (b, h, 0, 0)
    return pl.pallas_call(
        _retcore_kernel,
        grid=(bsz, RET_HEADS),
        in_specs=[
            pl.BlockSpec(memory_space=pltpu.SMEM),
            pl.BlockSpec((None, None, s, dk), head_block),
            pl.BlockSpec((None, nc, dk, c), lambda b, h: (b, 0, h, 0)),
            pl.BlockSpec((None, None, s, dv), head_block),
            pl.BlockSpec((None, None, s, dv), head_block),
            pl.BlockSpec((None, c, c), per_head),
            pl.BlockSpec((None, c, LANES), per_head),
            pl.BlockSpec((None, c, LANES), per_head),
            pl.BlockSpec((None, 1, c), per_head),
            pl.BlockSpec((None, 1, c), per_head),
        ],
        out_specs=pl.BlockSpec((None, None, s, dv), head_block),
        out_shape=jax.ShapeDtypeStruct((bsz, RET_HEADS, s, dv), BF16),
        scratch_shapes=[pltpu.VMEM((nc, 2 * dk, dv), BF16),
                        pltpu.VMEM((dk, dv), F32),
                        pltpu.VMEM((dk, dv), F32)],
        compiler_params=_params(2),
        name="ret_core",
    )(cd, q, kt, v, sg, dm, qdf, qdb, kdf, kdb)


def _mix_ffn_kernel(*refs, emit):
    n_out = 1 if emit == "final" else 2
    ins = refs[:len(refs) - n_out - 2]
    outs = refs[len(ins):len(ins) + n_out]
    h2_ref, a_ref = refs[len(ins) + n_out:]
    o_ref = outs[0]
    if emit == "x+h_rm":
        ins, perm_ref = ins[:-1], ins[-1]
    x_ref, vec_ref, gain_ref = ins[:3]
    g1, sh2, sc2, g2 = (vec_ref[r:r + 1, :] for r in range(4))
    zin_ref, wo_ref, win_ref, wout_ref = ins[3:]
    f = wout_ref.shape[0]
    tm = x_ref.shape[0]
    sub = h2_ref.shape[0] // MIX_SUBTILES
    slots = [slice(k * sub, (k + 1) * sub) for k in range(MIX_SUBTILES)]
    for round0 in range(0, tm, MIX_SUBTILES * sub):
        subtiles = [(slice(round0 + ss.start, round0 + ss.stop), ss) for ss in slots]
        for rs, ss in subtiles:
            z = jnp.concatenate([zin_ref[k, rs, :] for k in range(zin_ref.shape[0])], axis=1)
            m = jnp.dot(z, wo_ref[...], preferred_element_type=F32)
            x1 = x_ref[rs, :] + g1 * m
            h2_ref[ss, :] = _norm_mod(x1, gain_ref[0:1, :], sh2, sc2).astype(h2_ref.dtype)
            o_ref[rs, :] = x1
        for rs, ss in subtiles:
            for c0 in range(0, f, FFN_CHUNK):
                gate = jnp.dot(h2_ref[ss, :], win_ref[:, c0:c0 + FFN_CHUNK],
                               preferred_element_type=F32)
                up = jnp.dot(h2_ref[ss, :], win_ref[:, f + c0:f + c0 + FFN_CHUNK],
                             preferred_element_type=F32)
                a_ref[ss, c0:c0 + FFN_CHUNK] = (_silu(gate) * up).astype(a_ref.dtype)
        for rs, ss in subtiles:
            ffn = jnp.dot(a_ref[ss, :], wout_ref[...], preferred_element_type=F32)
            x2 = o_ref[rs, :] + g2 * ffn
            hn = _norm_mod(x2, gain_ref[1:2, :], vec_ref[4:5, :], vec_ref[5:6, :])
            if emit == "final":
                o_ref[rs, :] = hn
                continue
            o_ref[rs, :] = x2
            hn_ref = outs[1]
            if emit == "x+h":
                hn_ref[rs, :] = hn.astype(hn_ref.dtype)
            else:
                _store_residue_major(hn, perm_ref, hn_ref, rs.start)


def _mix_ffn(x, vecs, gains, mix_inputs, wo_all, wo_layer, win_all, wout_all, layer,
             *, fourier, emit, tm=ROW_TILE):
    bsz, s, d = x.shape
    tm = min(tm, s)
    row = lambda b, i: (b, i, 0)
    in_specs = [pl.BlockSpec((None, tm, d), row),
                pl.BlockSpec((None, 8, d), lambda b, i: (b, 0, 0)),
                pl.BlockSpec((8, d), lambda b, i: (0, 0))]
    args = [x, vecs, gains]
    for a in mix_inputs:
        in_specs.append(pl.BlockSpec((None, a.shape[1], tm, a.shape[3]), lambda b, i: (b, 0, i, 0)))
        args.append(a)
    for a, idx in ((wo_all, wo_layer), (win_all, layer), (wout_all, layer)):
        in_specs.append(_resident((None,) + a.shape[1:], lambda b, i, idx=idx: (idx, 0, 0)))
        args.append(a)
    out_specs = [pl.BlockSpec((None, tm, d), row)]
    out_shape = [jax.ShapeDtypeStruct((bsz, s, d), F32)]
    round_rows = min(tm, MIX_ROUND_ROWS)
    assert tm % round_rows == 0
    scratch = [pltpu.VMEM((round_rows, d), BF16),
               pltpu.VMEM((round_rows, wout_all.shape[1]), BF16)]
    if emit == "x+h":
        out_specs.append(pl.BlockSpec((None, tm, d), row))
        out_shape.append(jax.ShapeDtypeStruct((bsz, s, d), BF16))
    elif emit == "x+h_rm":
        out_specs.append(_rm_out_spec(tm, d))
        out_shape.append(jax.ShapeDtypeStruct(_rm_shape(bsz, s, d), BF16))
        assert tm % PERM_ROWS == 0
        in_specs.append(_resident((PERM_ROWS, PERM_ROWS), lambda b, i: (0, 0)))
        args.append(_residue_perm())
    return pl.pallas_call(
        functools.partial(_mix_ffn_kernel, emit=emit),
        grid=(bsz, s // tm),
        in_specs=in_specs,
        out_specs=out_specs,
        out_shape=out_shape,
        scratch_shapes=scratch,
        compiler_params=_params(2),
        name="mix_ffn_fourier" if fourier else "mix_ffn_ret",
    )(*args)


def _rotary_tables(s, half):
    inv_freq = ROPE_BASE ** (-jnp.arange(half, dtype=F32) / half)
    ang = jnp.arange(s, dtype=F32)[:, None] * inv_freq[None, :]
    return jnp.cos(ang), jnp.sin(ang)


def kernel(x, c, w_ada, b_ada, norm_mix_g, norm_ffn_g, w_fourier_out, w_ret_in, w_ret_out,
           w_ffn_in, w_ffn_out, final_norm_g, w_ada_final, b_ada_final):
    bsz, s, d = x.shape
    depth = w_ada.shape[0]
    gd = d // FOURIER_GROUPS
    dk = d // RET_HEADS
    hdk = RET_HEADS * dk
    chunk = min(RET_CHUNK, s)

    mods = _ada(c, w_ada, b_ada).reshape(depth, bsz, 6, d)
    mod_final = _ada(c, w_ada_final[None], b_ada_final[None]).reshape(bsz, 2, d)

    seq_tab = _seqdft_tables(s)
    cs_tab = _chan_dft_table(gd, s)
    cos, sin = _rotary_tables(s, dk // 2)
    ret_tabs = _ret_decay_tables(chunk)

    win_all = w_ffn_in.astype(BF16)
    wout_all = w_ffn_out.astype(BF16)
    wfo_all = w_fourier_out.astype(BF16)
    wri_all = w_ret_in.astype(BF16)
    wro_all = w_ret_out.astype(BF16)

    def is_fourier(i):
        return i % N_MIXERS == 0

    h = _prenorm(x, norm_mix_g[0], mods[0, :, 0], mods[0, :, 1], residue_major=is_fourier(0),
                 out_dtype=BF16, tm=2 * ROW_TILE)
    for i in range(depth):
        last = i == depth - 1
        if last:
            next_mod, next_gain, emit = mod_final, final_norm_g, "final"
        else:
            next_mod, next_gain = mods[i + 1, :, 0:2], norm_mix_g[i + 1]
            emit = "x+h_rm" if is_fourier(i + 1) else "x+h"
        pad = jnp.zeros((bsz, 2, d), F32)
        vecs = jnp.concatenate([mods[i, :, 2:6], next_mod, pad], axis=1)
        gains = jnp.concatenate([norm_ffn_g[i][None], next_gain[None], jnp.zeros((6, d), F32)])
        j = i // N_MIXERS
        if is_fourier(i):
            f = _fourier2d(h, seq_tab, cs_tab)
            outs = _mix_ffn(x, vecs, gains, (f,), wfo_all, j,
                            win_all, wout_all, i, fourier=True, emit=emit, tm=2 * ROW_TILE)
        else:
            q, kt, v, sg = _retproj(h, wri_all, j, cos, sin, chunk, tm=2 * ROW_TILE)
            z = _retcore(q, kt, v, sg, ret_tabs)
            outs = _mix_ffn(x, vecs, gains, (z,), wro_all, j,
                            win_all, wout_all, i, fourier=False, emit=emit, tm=2 * ROW_TILE)
        if last:
            return outs[0]
        x, h = outs
```

```python
import functools
import math

import jax
import jax.numpy as jnp
from jax import lax
from jax.experimental import pallas as pl
from jax.experimental.pallas import tpu as pltpu

EPS = 1e-6
N_MIXERS = 2
FOURIER_GROUPS = 4
RET_HEADS = 4
ROPE_BASE = 10000.0

DFT_RADIX = 16
RET_CHUNK = 256
FFN_CHUNK = 256
LANES = 128
ROW_TILE = 512
FOURIER_STEP_COLS = 512
MIX_SUBTILES = 2
VMEM_LIMIT_BYTES = 56 * 1024 * 1024

F32 = jnp.float32
BF16 = jnp.bfloat16


def _params(n_grid_dims):
    return pltpu.CompilerParams(
        dimension_semantics=("arbitrary",) * n_grid_dims,
        vmem_limit_bytes=VMEM_LIMIT_BYTES)


def _resident(block_shape, index_map):
    return pl.BlockSpec(block_shape, index_map, pipeline_mode=pl.Buffered(1))


def _silu(v):
    return v * jax.nn.sigmoid(v)


def _norm_mod(x, g, shift, scale):
    ms = jnp.mean(x * x, axis=-1, keepdims=True)
    y = x * lax.rsqrt(ms + EPS)
    return (y * g) * (1.0 + scale) + shift


def _ada_kernel(c_ref, w_ref, b_ref, o_ref):
    ca = _silu(c_ref[...])
    w = w_ref[...]
    ca_hi = ca.astype(BF16)
    ca_lo = (ca - ca_hi.astype(F32)).astype(BF16)
    w_hi = w.astype(BF16)
    w_lo = (w - w_hi.astype(F32)).astype(BF16)
    nb = ca.shape[0]
    p = jnp.dot(jnp.concatenate([ca_hi, ca_lo], axis=0), w_hi, preferred_element_type=F32)
    q = jnp.dot(ca_hi, w_lo, preferred_element_type=F32)
    o_ref[...] = (p[:nb] + p[nb:]) + q + b_ref[...]


def _ada(c, w, b, tn=1024):
    n_layers, d, n = w.shape
    bsz = c.shape[0]
    tn = min(tn, n)
    return pl.pallas_call(
        _ada_kernel,
        grid=(n_layers, n // tn),
        in_specs=[
            pl.BlockSpec((bsz, d), lambda l, j: (0, 0)),
            pl.BlockSpec((None, d, tn), lambda l, j: (l, 0, j)),
            pl.BlockSpec((None, 1, tn), lambda l, j: (l, 0, j)),
        ],
        out_specs=pl.BlockSpec((None, bsz, tn), lambda l, j: (l, 0, j)),
        out_shape=jax.ShapeDtypeStruct((n_layers, bsz, n), F32),
        compiler_params=_params(2),
        name="ada_mod",
    )(c, w, b.reshape(n_layers, 1, n))


PERM_ROWS = DFT_RADIX * DFT_RADIX


def _residue_perm():
    r = jnp.arange(PERM_ROWS, dtype=jnp.int32)
    src = (r % DFT_RADIX) * DFT_RADIX + r // DFT_RADIX
    return (src[:, None] == r[None, :]).astype(BF16)


def _store_residue_major(h, p_ref, o_ref, row0=0):
    hb = h.astype(o_ref.dtype)
    slab = o_ref.shape[3]
    for r0 in range(0, h.shape[0], PERM_ROWS):
        p = jnp.dot(p_ref[...], hb[r0:r0 + PERM_ROWS], preferred_element_type=F32).astype(o_ref.dtype)
        j0 = (row0 + r0) // DFT_RADIX
        for k in range(o_ref.shape[0]):
            for s1 in range(DFT_RADIX):
                o_ref[k, s1, j0:j0 + DFT_RADIX, :] = (
                    p[s1 * DFT_RADIX:(s1 + 1) * DFT_RADIX, k * slab:(k + 1) * slab])


def _rm_shape(bsz, s, d):
    slab = min(FOURIER_STEP_COLS, d)
    return (bsz, d // slab, DFT_RADIX, s // DFT_RADIX, slab)


def _rm_out_spec(tm, d):
    slab = min(FOURIER_STEP_COLS, d)
    return pl.BlockSpec((None, d // slab, DFT_RADIX, tm // DFT_RADIX, slab),
                        lambda b, i: (b, 0, 0, i, 0))


def _prenorm_kernel(x_ref, g_ref, sh_ref, sc_ref, *rest, residue_major):
    h = _norm_mod(x_ref[...], g_ref[...], sh_ref[...], sc_ref[...])
    if residue_major:
        p_ref, o_ref = rest
        _store_residue_major(h, p_ref, o_ref)
    else:
        o_ref, = rest
        o_ref[...] = h.astype(o_ref.dtype)


def _prenorm(x, g, shift, scale, *, residue_major, out_dtype, tm=ROW_TILE):
    bsz, s, d = x.shape
    tm = min(tm, s)
    vec = lambda a: a.reshape(bsz, 1, d)
    in_specs = [
        pl.BlockSpec((None, tm, d), lambda b, i: (b, i, 0)),
        pl.BlockSpec((1, d), lambda b, i: (0, 0)),
        pl.BlockSpec((None, 1, d), lambda b, i: (b, 0, 0)),
        pl.BlockSpec((None, 1, d), lambda b, i: (b, 0, 0)),
    ]
    args = [x, g.reshape(1, d), vec(shift), vec(scale)]
    if residue_major:
        assert tm % PERM_ROWS == 0
        in_specs.append(_resident((PERM_ROWS, PERM_ROWS), lambda b, i: (0, 0)))
        args.append(_residue_perm())
        out_shape = jax.ShapeDtypeStruct(_rm_shape(bsz, s, d), out_dtype)
        out_spec = _rm_out_spec(tm, d)
    else:
        out_shape = jax.ShapeDtypeStruct((bsz, s, d), out_dtype)
        out_spec = pl.BlockSpec((None, tm, d), lambda b, i: (b, i, 0))
    return pl.pallas_call(
        functools.partial(_prenorm_kernel, residue_major=residue_major),
        grid=(bsz, s // tm),
        in_specs=in_specs,
        out_specs=out_spec,
        out_shape=out_shape,
        compiler_params=_params(2),
        name="prenorm_rm" if residue_major else "prenorm",
    )(*args)


def _dft4(x):
    (x0r, x0i), (x1r, x1i), (x2r, x2i), (x3r, x3i) = x
    t0r, t0i = x0r + x2r, x0i + x2i
    t1r, t1i = x0r - x2r, x0i - x2i
    t2r, t2i = x1r + x3r, x1i + x3i
    t3r, t3i = x1r - x3r, x1i - x3i
    return [(t0r + t2r, t0i + t2i),
            (t1r + t3i, t1i - t3r),
            (t0r - t2r, t0i - t2i),
            (t1r - t3i, t1i + t3r)]


def _twiddle16(z, p):
    zr, zi = z
    p = p % 16
    if p == 0:
        return zr, zi
    if p == 4:
        return zi, -zr
    if p == 8:
        return -zr, -zi
    if p == 12:
        return -zi, zr
    if p % 4 == 2:
        c = math.sqrt(0.5)
        sr = -c if p in (6, 10) else c
        si = -c if p in (2, 6) else c
        add, sub = zr + zi, zr - zi
        real = add * sr if sr == -si else sub * sr
        imag = sub * si if sr == -si else add * si
        return real, imag
    cr = math.cos(2.0 * math.pi * p / 16.0)
    ci = -math.sin(2.0 * math.pi * p / 16.0)
    return zr * cr - zi * ci, zr * ci + zi * cr


DFT_SLAB = 256


def _fourier2d_kernel(h_ref, w_ref, cs_ref, f_ref, *scratch):
    n2 = h_ref.shape[1]
    td = h_ref.shape[2]
    rows = 8
    slabs = [(c0, scratch[2 * i], scratch[2 * i + 1])
             for i, c0 in enumerate(range(0, td, DFT_SLAB))]
    for c0, br_ref, bi_ref in slabs:
        for s1 in range(DFT_RADIX):
            p = jnp.dot(w_ref[s1], h_ref[s1, :, c0:c0 + DFT_SLAB], preferred_element_type=F32)
            br_ref[s1] = p[:n2]
            bi_ref[s1] = p[n2:]

    quarter = max(n2 // 4, rows)
    for c0, br_ref, bi_ref in slabs:
        for q0 in range(0, n2, quarter):
            for r in range(q0, q0 + quarter, rows):
                rs = slice(r, r + rows)
                for l0 in range(0, DFT_SLAB, LANES):
                    ls = slice(l0, l0 + LANES)
                    us = [None] * DFT_RADIX
                    for b in range(4):
                        xs = [(br_ref[4 * a + b, rs, ls], bi_ref[4 * a + b, rs, ls])
                              for a in range(4)]
                        ub = _dft4(xs)
                        for c in range(4):
                            us[4 * c + b] = _twiddle16(ub[c], b * c)
                    for c in range(4):
                        ys = _dft4(us[4 * c:4 * c + 4])
                        for d in range(4):
                            br_ref[c + 4 * d, rs, ls] = ys[d][0]
                            bi_ref[c + 4 * d, rs, ls] = ys[d][1]
            qs = slice(q0, q0 + quarter)
            yq = jnp.concatenate(
                [br_ref[:, qs, :].reshape(DFT_RADIX * quarter, DFT_SLAB).astype(BF16),
                 bi_ref[:, qs, :].reshape(DFT_RADIX * quarter, DFT_SLAB).astype(BF16)], axis=1)
            fq = jnp.dot(yq, cs_ref[...], preferred_element_type=F32).astype(f_ref.dtype)
            for k1 in range(DFT_RADIX):
                f_ref[k1 * n2 + q0:k1 * n2 + q0 + quarter, c0:c0 + DFT_SLAB] = (
                    fq[k1 * quarter:(k1 + 1) * quarter])


def _seqdft_tables(s):
    n2 = s // DFT_RADIX
    k2 = jnp.arange(n2, dtype=jnp.int32)
    s1 = jnp.arange(DFT_RADIX, dtype=jnp.int32)
    unit = 2.0 * jnp.pi / s
    a = ((k2[:, None] * (DFT_RADIX * k2[None, :])) % s).astype(F32) * unit
    b = ((s1[:, None] * k2[None, :]) % s).astype(F32) * unit
    ca, sa = jnp.cos(a)[None], jnp.sin(a)[None]
    cb, sb = jnp.cos(b)[:, :, None], jnp.sin(b)[:, :, None]
    wr = ca * cb - sa * sb
    wi = -(sa * cb + ca * sb)
    return jnp.concatenate([wr, wi], axis=1).astype(BF16)


def _chan_dft_table(gd, s):
    c = jnp.arange(gd, dtype=jnp.int32)
    j = (c[:, None] * c[None, :]) % gd
    ang = j.astype(F32) * (2.0 * jnp.pi / gd)
    scale = 1.0 / jnp.sqrt(jnp.asarray(s * gd, F32))
    eye = jnp.eye(DFT_SLAB // gd, dtype=F32)
    blocks = [jnp.kron(eye, t * scale) for t in (jnp.cos(ang), jnp.sin(ang))]
    return jnp.concatenate(blocks, axis=0).astype(BF16)


def _fourier2d(h_rm, w_tab, cs_tab):
    bsz, n_steps, _, n2, td = h_rm.shape
    s = DFT_RADIX * n2
    slab_scratch = [pltpu.VMEM((DFT_RADIX, n2, DFT_SLAB), F32),
                    pltpu.VMEM((DFT_RADIX, n2, DFT_SLAB), F32)]
    return pl.pallas_call(
        _fourier2d_kernel,
        grid=(bsz, n_steps),
        in_specs=[
            pl.BlockSpec((None, None, DFT_RADIX, n2, td), lambda b, j: (b, j, 0, 0, 0)),
            _resident((DFT_RADIX, 2 * n2, n2), lambda b, j: (0, 0, 0)),
            _resident((2 * DFT_SLAB, DFT_SLAB), lambda b, j: (0, 0)),
        ],
        out_specs=pl.BlockSpec((None, None, s, td), lambda b, j: (b, j, 0, 0)),
        out_shape=jax.ShapeDtypeStruct((bsz, n_steps, s, td), BF16),
        scratch_shapes=slab_scratch * (td // DFT_SLAB),
        compiler_params=_params(2),
        name="fourier2d",
    )(h_rm, w_tab, cs_tab)


def _retproj_kernel(h_ref, w_ref, cos_ref, sin_ref,
                    q_ref, kt_ref, v_ref, sg_ref, wkt_ref, *, dk, chunk):
    h = h_ref[...]
    half = dk // 2
    tm = h.shape[0]
    hdk = RET_HEADS * dk
    dv = v_ref.shape[2]
    dvt = RET_HEADS * dv

    @pl.when((pl.program_id(0) == 0) & (pl.program_id(1) == 0))
    def _():
        for c0 in range(0, hdk, dk):
            wk = w_ref[:, hdk + c0:hdk + c0 + dk].astype(F32)
            wkt_ref[c0:c0 + dk, :] = wk.T.astype(wkt_ref.dtype)

    cos, sin = cos_ref[...], sin_ref[...]
    cost, sint = cos.T, sin.T
    k_scale = dk ** -0.5
    for hd in range(RET_HEADS):
        c0 = hd * dk
        q = jnp.dot(h, w_ref[:, c0:c0 + dk], preferred_element_type=F32)
        q1, q2 = q[:, :half], q[:, half:]
        q_ref[hd, :, :half] = (q1 * cos - q2 * sin).astype(q_ref.dtype)
        q_ref[hd, :, half:] = (q1 * sin + q2 * cos).astype(q_ref.dtype)
        kt = lax.dot_general(wkt_ref[c0:c0 + dk, :], h, (((1,), (1,)), ((), ())),
                             preferred_element_type=F32)
        k1, k2 = kt[:half], kt[half:]
        r1 = (k1 * cost - k2 * sint) * k_scale
        r2 = (k1 * sint + k2 * cost) * k_scale
        for j in range(tm // chunk):
            kt_ref[j, c0:c0 + half, :] = r1[:, j * chunk:(j + 1) * chunk].astype(kt_ref.dtype)
            kt_ref[j, c0 + half:c0 + dk, :] = r2[:, j * chunk:(j + 1) * chunk].astype(kt_ref.dtype)
    step = min(256, dv)
    v0, g0 = 2 * hdk, 2 * hdk + dvt
    for hd in range(RET_HEADS):
        for o in range(0, dv, step):
            c0 = hd * dv + o
            v_ref[hd, :, o:o + step] = jnp.dot(h, w_ref[:, v0 + c0:v0 + c0 + step],
                                               preferred_element_type=F32).astype(v_ref.dtype)
            g = jnp.dot(h, w_ref[:, g0 + c0:g0 + c0 + step], preferred_element_type=F32)
            sg_ref[hd, :, o:o + step] = _silu(g).astype(sg_ref.dtype)


def _retproj(h, w, cos, sin, chunk, tm=ROW_TILE):
    bsz, s, d = h.shape
    hdk = d
    hdv = (w.shape[1] - 2 * hdk) // 2
    dk = hdk // RET_HEADS
    dv = hdv // RET_HEADS
    half = dk // 2
    tm = min(tm, s)
    row = lambda b, i: (b, i, 0)
    head_row = lambda b, i: (b, 0, i, 0)
    return pl.pallas_call(
        functools.partial(_retproj_kernel, dk=dk, chunk=chunk),
        grid=(bsz, s // tm),
        in_specs=[
            pl.BlockSpec((None, tm, d), row),
            _resident(w.shape, lambda b, i: (0, 0)),
            pl.BlockSpec((tm, half), lambda b, i: (i, 0)),
            pl.BlockSpec((tm, half), lambda b, i: (i, 0)),
        ],
        out_specs=[
            pl.BlockSpec((None, RET_HEADS, tm, dk), head_row),
            pl.BlockSpec((None, tm // chunk, hdk, chunk), lambda b, i: (b, i, 0, 0)),
            pl.BlockSpec((None, RET_HEADS, tm, dv), head_row),
            pl.BlockSpec((None, RET_HEADS, tm, dv), head_row),
        ],
        out_shape=[
            jax.ShapeDtypeStruct((bsz, RET_HEADS, s, dk), BF16),
            jax.ShapeDtypeStruct((bsz, s // chunk, hdk, chunk), BF16),
            jax.ShapeDtypeStruct((bsz, RET_HEADS, s, dv), BF16),
            jax.ShapeDtypeStruct((bsz, RET_HEADS, s, dv), BF16),
        ],
        scratch_shapes=[pltpu.VMEM((hdk, d), BF16)],
        compiler_params=_params(2),
        name="ret_proj",
    )(h, w, cos, sin)


def _retcore_kernel(cd_ref, q_ref, kt_ref, v_ref, dm_ref, qdf_ref, qdb_ref, kdf_ref, kdb_ref,
                    y_ref, r_all_ref, rf_ref, rb_ref):
    hd = pl.program_id(1)
    nc, dk, c = kt_ref.shape
    cd_f = cd_ref[0, hd]
    cd_b = cd_ref[1, hd]
    kdf = kdf_ref[...]
    kdb = kdb_ref[...]

    rf_ref[...] = jnp.zeros_like(rf_ref)
    rb_ref[...] = jnp.zeros_like(rb_ref)
    for it in range(nc):
        nf, nb = it, nc - 1 - it
        r_all_ref[nf, :dk, :] = rf_ref[...].astype(r_all_ref.dtype)
        r_all_ref[nb, dk:, :] = rb_ref[...].astype(r_all_ref.dtype)
        if it == nc - 1:
            break
        ktf = (kt_ref[nf].astype(F32) * kdf).astype(BF16)
        ktb = (kt_ref[nb].astype(F32) * kdb).astype(BF16)
        rf_ref[...] = rf_ref[...] * cd_f + jnp.dot(
            ktf, v_ref[nf * c:(nf + 1) * c, :], preferred_element_type=F32)
        rb_ref[...] = rb_ref[...] * cd_b + jnp.dot(
            ktb, v_ref[nb * c:(nb + 1) * c, :], preferred_element_type=F32)

    reps = dk // LANES
    qdf = jnp.tile(qdf_ref[...], (1, reps))
    qdb = jnp.tile(qdb_ref[...], (1, reps))
    for n in range(nc):
        rows = slice(n * c, (n + 1) * c)
        qn = q_ref[rows, :]
        sc = jnp.dot(qn, kt_ref[n], preferred_element_type=F32) * dm_ref[...]
        y = jnp.dot(sc.astype(BF16), v_ref[rows, :], preferred_element_type=F32)
        qf32 = qn.astype(F32)
        qfb = jnp.concatenate([(qf32 * qdf).astype(BF16), (qf32 * qdb).astype(BF16)], axis=1)
        y += jnp.dot(qfb, r_all_ref[n], preferred_element_type=F32)
        mu = jnp.mean(y, axis=-1, keepdims=True)
        yc = y - mu
        var = jnp.mean(yc * yc, axis=-1, keepdims=True)
        y_ref[rows, :] = (yc * lax.rsqrt(var + EPS)).astype(y_ref.dtype)


def _ret_decay_tables(c):
    hidx = jnp.arange(RET_HEADS, dtype=F32)
    lg_f = jnp.log1p(-jnp.exp2(-5.0 - hidx))
    lg_b = jnp.flip(lg_f)
    t = jnp.arange(c, dtype=jnp.int32)
    diff = (t[:, None] - t[None, :])
    fmask = diff >= 0
    bmask = diff < 0
    ef = jnp.exp(jnp.where(fmask, diff, 0).astype(F32)[None] * lg_f[:, None, None])
    eb = jnp.exp(jnp.where(bmask, -diff, 0).astype(F32)[None] * lg_b[:, None, None])
    dm = jnp.where(fmask[None], ef, eb)
    tf = t.astype(F32)
    ones = jnp.ones((1, 1, LANES), F32)
    qdf = jnp.exp((tf + 1.0)[None, :] * lg_f[:, None])[:, :, None] * ones
    qdb = jnp.exp((c - tf)[None, :] * lg_b[:, None])[:, :, None] * ones
    kdf = jnp.exp((c - 1.0 - tf)[None, :] * lg_f[:, None])[:, None, :]
    kdb = jnp.exp(tf[None, :] * lg_b[:, None])[:, None, :]
    cd = jnp.stack([jnp.exp(c * lg_f), jnp.exp(c * lg_b)])
    return cd, dm, qdf, qdb, kdf, kdb


def _retcore(q, kt, v, tables):
    bsz, _, s, dk = q.shape
    dv = v.shape[3]
    nc, c = kt.shape[1], kt.shape[3]
    cd, dm, qdf, qdb, kdf, kdb = tables
    per_head = lambda b, h: (h, 0, 0)
    head_block = lambda b, h: (b, h, 0, 0)
    return pl.pallas_call(
        _retcore_kernel,
        grid=(bsz, RET_HEADS),
        in_specs=[
            pl.BlockSpec(memory_space=pltpu.SMEM),
            pl.BlockSpec((None, None, s, dk), head_block),
            pl.BlockSpec((None, nc, dk, c), lambda b, h: (b, 0, h, 0)),
            pl.BlockSpec((None, None, s, dv), head_block),
            pl.BlockSpec((None, c, c), per_head),
            pl.BlockSpec((None, c, LANES), per_head),
            pl.BlockSpec((None, c, LANES), per_head),
            pl.BlockSpec((None, 1, c), per_head),
            pl.BlockSpec((None, 1, c), per_head),
        ],
        out_specs=pl.BlockSpec((None, None, s, dv), head_block),
        out_shape=jax.ShapeDtypeStruct((bsz, RET_HEADS, s, dv), BF16),
        scratch_shapes=[pltpu.VMEM((nc, 2 * dk, dv), BF16),
                        pltpu.VMEM((dk, dv), F32),
                        pltpu.VMEM((dk, dv), F32)],
        compiler_params=_params(2),
        name="ret_core",
    )(cd, q, kt, v, dm, qdf, qdb, kdf, kdb)


def _mix_ffn_kernel(*refs, fourier, emit, cast_reps):
    n_cast = len(cast_reps)
    n_out = (1 if emit == "final" else 2) + n_cast
    ins = refs[:len(refs) - n_out - 2]
    outs = refs[len(ins):len(ins) + n_out]
    h2_ref, a_ref = refs[len(ins) + n_out:]
    o_ref = outs[0]
    flat_step = pl.program_id(0) * pl.num_programs(1) + pl.program_id(1)
    for src_ref, dst_ref, rep in zip(ins[len(ins) - n_cast:], outs[len(outs) - n_cast:], cast_reps):
        if rep == 1:
            dst_ref[...] = src_ref[...].astype(dst_ref.dtype)
        else:
            @pl.when(flat_step % rep == 0)
            def _(src_ref=src_ref, dst_ref=dst_ref):
                dst_ref[...] = src_ref[...].astype(dst_ref.dtype)
    ins = ins[:len(ins) - n_cast]
    if emit == "x+h_rm":
        ins, perm_ref = ins[:-1], ins[-1]
    x_ref, vec_ref, gain_ref = ins[:3]
    g1, sh2, sc2, g2 = (vec_ref[r:r + 1, :] for r in range(4))
    if fourier:
        f_ref, wo_ref, win_ref, wout_ref = ins[3:]
    else:
        y_ref, sg_ref, wo_ref, win_ref, wout_ref = ins[3:]
    f = wout_ref.shape[0]
    tm = x_ref.shape[0]
    sub = tm // MIX_SUBTILES
    subtiles = [slice(r0, r0 + sub) for r0 in range(0, tm, sub)]
    for rs in subtiles:
        if fourier:
            z = jnp.concatenate([f_ref[k, rs, :] for k in range(f_ref.shape[0])], axis=1)
        else:
            z = jnp.concatenate(
                [(sg_ref[k, rs, :].astype(F32) * y_ref[k, rs, :].astype(F32)).astype(BF16)
                 for k in range(y_ref.shape[0])], axis=1)
        m = jnp.dot(z, wo_ref[...], preferred_element_type=F32)
        x1 = x_ref[rs, :] + g1 * m
        h2_ref[rs, :] = _norm_mod(x1, gain_ref[0:1, :], sh2, sc2).astype(h2_ref.dtype)
        o_ref[rs, :] = x1
    for rs in subtiles:
        for c0 in range(0, f, FFN_CHUNK):
            gate = jnp.dot(h2_ref[rs, :], win_ref[:, c0:c0 + FFN_CHUNK], preferred_element_type=F32)
            up = jnp.dot(h2_ref[rs, :], win_ref[:, f + c0:f + c0 + FFN_CHUNK],
                         preferred_element_type=F32)
            a_ref[rs, c0:c0 + FFN_CHUNK] = (_silu(gate) * up).astype(a_ref.dtype)
    for rs in subtiles:
        ffn = jnp.dot(a_ref[rs, :], wout_ref[...], preferred_element_type=F32)
        x2 = o_ref[rs, :] + g2 * ffn
        hn = _norm_mod(x2, gain_ref[1:2, :], vec_ref[4:5, :], vec_ref[5:6, :])
        if emit == "final":
            o_ref[rs, :] = hn
            continue
        o_ref[rs, :] = x2
        hn_ref = outs[1]
        if emit == "x+h":
            hn_ref[rs, :] = hn.astype(hn_ref.dtype)
        else:
            _store_residue_major(hn, perm_ref, hn_ref, rs.start)


def _cast_rows(rows, n_steps):
    n = n_steps
    while n > 1 and (rows % n or (rows // n) % 16):
        n //= 2
    return rows // n


def _mix_ffn(x, vecs, gains, mix_inputs, wo, win, wout, *, fourier, emit, cast_jobs=(),
             tm=ROW_TILE):
    bsz, s, d = x.shape
    tm = min(tm, s)
    n_tiles = s // tm
    row = lambda b, i: (b, i, 0)
    in_specs = [pl.BlockSpec((None, tm, d), row),
                pl.BlockSpec((None, 8, d), lambda b, i: (b, 0, 0)),
                pl.BlockSpec((8, d), lambda b, i: (0, 0))]
    args = [x, vecs, gains]
    for a in mix_inputs:
        in_specs.append(pl.BlockSpec((None, a.shape[1], tm, a.shape[3]), lambda b, i: (b, 0, i, 0)))
        args.append(a)
    for a in (wo, win, wout):
        in_specs.append(_resident(a.shape, lambda b, i: (0, 0)))
        args.append(a)
    out_specs = [pl.BlockSpec((None, tm, d), row)]
    out_shape = [jax.ShapeDtypeStruct((bsz, s, d), F32)]
    scratch = [pltpu.VMEM((tm, d), BF16), pltpu.VMEM((tm, wout.shape[0]), BF16)]
    if emit == "x+h":
        out_specs.append(pl.BlockSpec((None, tm, d), row))
        out_shape.append(jax.ShapeDtypeStruct((bsz, s, d), BF16))
    elif emit == "x+h_rm":
        out_specs.append(_rm_out_spec(tm, d))
        out_shape.append(jax.ShapeDtypeStruct(_rm_shape(bsz, s, d), BF16))
        assert tm % PERM_ROWS == 0
        in_specs.append(_resident((PERM_ROWS, PERM_ROWS), lambda b, i: (0, 0)))
        args.append(_residue_perm())
    cast_reps = []
    for stack, layer in cast_jobs:
        _, rows, cols = stack.shape
        rp = _cast_rows(rows, bsz * n_tiles)
        rep = bsz * n_tiles // (rows // rp)
        cast_reps.append(rep)
        in_specs.append(pl.BlockSpec(
            (None, rp, cols), lambda b, i, layer=layer, rep=rep: (layer, (b * n_tiles + i) // rep, 0)))
        args.append(stack)
        out_specs.append(pl.BlockSpec((rp, cols), lambda b, i, rep=rep: ((b * n_tiles + i) // rep, 0)))
        out_shape.append(jax.ShapeDtypeStruct((rows, cols), BF16))
    return pl.pallas_call(
        functools.partial(_mix_ffn_kernel, fourier=fourier, emit=emit, cast_reps=tuple(cast_reps)),
        grid=(bsz, s // tm),
        in_specs=in_specs,
        out_specs=out_specs,
        out_shape=out_shape,
        scratch_shapes=scratch,
        compiler_params=_params(2),
        name="mix_ffn_fourier" if fourier else "mix_ffn_ret",
    )(*args)


def _rotary_tables(s, half):
    inv_freq = ROPE_BASE ** (-jnp.arange(half, dtype=F32) / half)
    ang = jnp.arange(s, dtype=F32)[:, None] * inv_freq[None, :]
    return jnp.cos(ang), jnp.sin(ang)


def kernel(x, c, w_ada, b_ada, norm_mix_g, norm_ffn_g, w_fourier_out, w_ret_in, w_ret_out,
           w_ffn_in, w_ffn_out, final_norm_g, w_ada_final, b_ada_final):
    bsz, s, d = x.shape
    depth = w_ada.shape[0]
    gd = d // FOURIER_GROUPS
    dk = d // RET_HEADS
    hdk = RET_HEADS * dk
    chunk = min(RET_CHUNK, s)

    mods = _ada(c, w_ada, b_ada).reshape(depth, bsz, 6, d)
    mod_final = _ada(c, w_ada_final[None], b_ada_final[None]).reshape(bsz, 2, d)

    seq_tab = _seqdft_tables(s)
    cs_tab = _chan_dft_table(gd, s)
    cos, sin = _rotary_tables(s, dk // 2)
    ret_tabs = _ret_decay_tables(chunk)

    def is_fourier(i):
        return i % N_MIXERS == 0

    def mixer_weights(i):
        j = i // N_MIXERS
        return [(w_fourier_out, j)] if is_fourier(i) else [(w_ret_in, j), (w_ret_out, j)]

    w_mix = [stack[j].astype(BF16) for stack, j in mixer_weights(0)]
    w_in, w_out = w_ffn_in[0].astype(BF16), w_ffn_out[0].astype(BF16)

    h = _prenorm(x, norm_mix_g[0], mods[0, :, 0], mods[0, :, 1], residue_major=is_fourier(0),
                 out_dtype=BF16, tm=2 * ROW_TILE)
    for i in range(depth):
        last = i == depth - 1
        if last:
            next_mod, next_gain, emit = mod_final, final_norm_g, "final"
        else:
            next_mod, next_gain = mods[i + 1, :, 0:2], norm_mix_g[i + 1]
            emit = "x+h_rm" if is_fourier(i + 1) else "x+h"
        pad = jnp.zeros((bsz, 2, d), F32)
        vecs = jnp.concatenate([mods[i, :, 2:6], next_mod, pad], axis=1)
        gains = jnp.concatenate([norm_ffn_g[i][None], next_gain[None], jnp.zeros((6, d), F32)])
        jobs = () if last else tuple(
            mixer_weights(i + 1) + [(w_ffn_in, i + 1), (w_ffn_out, i + 1)])
        if is_fourier(i):
            mix_in = (_fourier2d(h, seq_tab, cs_tab),)
        else:
            q, kt, v, sg = _retproj(h, w_mix[0], cos, sin, chunk, tm=2 * ROW_TILE)
            mix_in = (_retcore(q, kt, v, ret_tabs), sg)
        outs = _mix_ffn(x, vecs, gains, mix_in, w_mix[-1], w_in, w_out,
                        fourier=is_fourier(i), emit=emit, cast_jobs=jobs)
        if last:
            return outs[0]
        x, h = outs[:2]
        *w_mix, w_in, w_out = outs[2:]
```

```python
import functools
import math

import jax
import jax.numpy as jnp
from jax import lax
from jax.experimental import pallas as pl
from jax.experimental.pallas import tpu as pltpu

EPS = 1e-6
N_MIXERS = 2
FOURIER_GROUPS = 4
RET_HEADS = 4
ROPE_BASE = 10000.0

DFT_RADIX = 16
RET_CHUNK = 256
FFN_CHUNK = 256
LANES = 128
ROW_TILE = 512
FOURIER_STEP_COLS = 512
MIX_SUBTILES = 2
VMEM_LIMIT_BYTES = 56 * 1024 * 1024

F32 = jnp.float32
BF16 = jnp.bfloat16


def _params(n_grid_dims):
    return pltpu.CompilerParams(
        dimension_semantics=("arbitrary",) * n_grid_dims,
        vmem_limit_bytes=VMEM_LIMIT_BYTES)


def _resident(block_shape, index_map):
    return pl.BlockSpec(block_shape, index_map, pipeline_mode=pl.Buffered(1))


def _silu(v):
    return v * jax.nn.sigmoid(v)


def _norm_mod(x, g, shift, scale):
    ms = jnp.mean(x * x, axis=-1, keepdims=True)
    y = x * lax.rsqrt(ms + EPS)
    return (y * g) * (1.0 + scale) + shift


def _ada_kernel(c_ref, w_ref, b_ref, o_ref):
    ca = _silu(c_ref[...])
    w = w_ref[...]
    ca_hi = ca.astype(BF16)
    ca_lo = (ca - ca_hi.astype(F32)).astype(BF16)
    w_hi = w.astype(BF16)
    w_lo = (w - w_hi.astype(F32)).astype(BF16)
    nb = ca.shape[0]
    p = jnp.dot(jnp.concatenate([ca_hi, ca_lo], axis=0), w_hi, preferred_element_type=F32)
    q = jnp.dot(ca_hi, w_lo, preferred_element_type=F32)
    o_ref[...] = (p[:nb] + p[nb:]) + q + b_ref[...]


def _ada(c, w, b, tn=2048):
    n_layers, d, n = w.shape
    bsz = c.shape[0]
    tn = min(tn, n)
    return pl.pallas_call(
        _ada_kernel,
        grid=(n_layers, n // tn),
        in_specs=[
            pl.BlockSpec((bsz, d), lambda l, j: (0, 0)),
            pl.BlockSpec((None, d, tn), lambda l, j: (l, 0, j)),
            pl.BlockSpec((None, 1, tn), lambda l, j: (l, 0, j)),
        ],
        out_specs=pl.BlockSpec((None, bsz, tn), lambda l, j: (l, 0, j)),
        out_shape=jax.ShapeDtypeStruct((n_layers, bsz, n), F32),
        compiler_params=_params(2),
        name="ada_mod",
    )(c, w, b.reshape(n_layers, 1, n))


PERM_ROWS = DFT_RADIX * DFT_RADIX


def _residue_perm():
    r = jnp.arange(PERM_ROWS, dtype=jnp.int32)
    src = (r % DFT_RADIX) * DFT_RADIX + r // DFT_RADIX
    return (src[:, None] == r[None, :]).astype(BF16)


def _store_residue_major(h, p_ref, o_ref, row0=0):
    hb = h.astype(o_ref.dtype)
    slab = o_ref.shape[3]
    for r0 in range(0, h.shape[0], PERM_ROWS):
        p = jnp.dot(p_ref[...], hb[r0:r0 + PERM_ROWS], preferred_element_type=F32).astype(o_ref.dtype)
        j0 = (row0 + r0) // DFT_RADIX
        for k in range(o_ref.shape[0]):
            for s1 in range(DFT_RADIX):
                o_ref[k, s1, j0:j0 + DFT_RADIX, :] = (
                    p[s1 * DFT_RADIX:(s1 + 1) * DFT_RADIX, k * slab:(k + 1) * slab])


def _rm_shape(bsz, s, d):
    slab = min(FOURIER_STEP_COLS, d)
    return (bsz, d // slab, DFT_RADIX, s // DFT_RADIX, slab)


def _rm_out_spec(tm, d):
    slab = min(FOURIER_STEP_COLS, d)
    return pl.BlockSpec((None, d // slab, DFT_RADIX, tm // DFT_RADIX, slab),
                        lambda b, i: (b, 0, 0, i, 0))


def _prenorm_kernel(x_ref, g_ref, sh_ref, sc_ref, *rest, residue_major):
    h = _norm_mod(x_ref[...], g_ref[...], sh_ref[...], sc_ref[...])
    if residue_major:
        p_ref, o_ref = rest
        _store_residue_major(h, p_ref, o_ref)
    else:
        o_ref, = rest
        o_ref[...] = h.astype(o_ref.dtype)


def _prenorm(x, g, shift, scale, *, residue_major, out_dtype, tm=ROW_TILE):
    bsz, s, d = x.shape
    tm = min(tm, s)
    vec = lambda a: a.reshape(bsz, 1, d)
    in_specs = [
        pl.BlockSpec((None, tm, d), lambda b, i: (b, i, 0)),
        pl.BlockSpec((1, d), lambda b, i: (0, 0)),
        pl.BlockSpec((None, 1, d), lambda b, i: (b, 0, 0)),
        pl.BlockSpec((None, 1, d), lambda b, i: (b, 0, 0)),
    ]
    args = [x, g.reshape(1, d), vec(shift), vec(scale)]
    if residue_major:
        assert tm % PERM_ROWS == 0
        in_specs.append(_resident((PERM_ROWS, PERM_ROWS), lambda b, i: (0, 0)))
        args.append(_residue_perm())
        out_shape = jax.ShapeDtypeStruct(_rm_shape(bsz, s, d), out_dtype)
        out_spec = _rm_out_spec(tm, d)
    else:
        out_shape = jax.ShapeDtypeStruct((bsz, s, d), out_dtype)
        out_spec = pl.BlockSpec((None, tm, d), lambda b, i: (b, i, 0))
    return pl.pallas_call(
        functools.partial(_prenorm_kernel, residue_major=residue_major),
        grid=(bsz, s // tm),
        in_specs=in_specs,
        out_specs=out_spec,
        out_shape=out_shape,
        compiler_params=_params(2),
        name="prenorm_rm" if residue_major else "prenorm",
    )(*args)


def _dft4(x):
    (x0r, x0i), (x1r, x1i), (x2r, x2i), (x3r, x3i) = x
    t0r, t0i = x0r + x2r, x0i + x2i
    t1r, t1i = x0r - x2r, x0i - x2i
    t2r, t2i = x1r + x3r, x1i + x3i
    t3r, t3i = x1r - x3r, x1i - x3i
    return [(t0r + t2r, t0i + t2i),
            (t1r + t3i, t1i - t3r),
            (t0r - t2r, t0i - t2i),
            (t1r - t3i, t1i + t3r)]


def _twiddle16(z, p):
    zr, zi = z
    p = p % 16
    if p == 0:
        return zr, zi
    if p == 4:
        return zi, -zr
    if p == 8:
        return -zr, -zi
    if p == 12:
        return -zi, zr
    if p % 4 == 2:
        c = math.sqrt(0.5)
        sr = -c if p in (6, 10) else c
        si = -c if p in (2, 6) else c
        add, sub = zr + zi, zr - zi
        real = add * sr if sr == -si else sub * sr
        imag = sub * si if sr == -si else add * si
        return real, imag
    cr = math.cos(2.0 * math.pi * p / 16.0)
    ci = -math.sin(2.0 * math.pi * p / 16.0)
    return zr * cr - zi * ci, zr * ci + zi * cr


DFT_SLAB = 256


def _fourier2d_kernel(h_ref, w_ref, cs_ref, f_ref, *scratch):
    n2 = h_ref.shape[1]
    td = h_ref.shape[2]
    rows = 8
    slabs = [(c0, scratch[2 * i], scratch[2 * i + 1])
             for i, c0 in enumerate(range(0, td, DFT_SLAB))]
    for c0, br_ref, bi_ref in slabs:
        for s1 in range(DFT_RADIX):
            p = jnp.dot(w_ref[s1], h_ref[s1, :, c0:c0 + DFT_SLAB], preferred_element_type=F32)
            br_ref[s1] = p[:n2]
            bi_ref[s1] = p[n2:]

    quarter = max(n2 // 4, rows)
    for c0, br_ref, bi_ref in slabs:
        for q0 in range(0, n2, quarter):
            for r in range(q0, q0 + quarter, rows):
                rs = slice(r, r + rows)
                for l0 in range(0, DFT_SLAB, LANES):
                    ls = slice(l0, l0 + LANES)
                    us = [None] * DFT_RADIX
                    for b in range(4):
                        xs = [(br_ref[4 * a + b, rs, ls], bi_ref[4 * a + b, rs, ls])
                              for a in range(4)]
                        ub = _dft4(xs)
                        for c in range(4):
                            us[4 * c + b] = _twiddle16(ub[c], b * c)
                    for c in range(4):
                        ys = _dft4(us[4 * c:4 * c + 4])
                        for d in range(4):
                            br_ref[c + 4 * d, rs, ls] = ys[d][0]
                            bi_ref[c + 4 * d, rs, ls] = ys[d][1]
            qs = slice(q0, q0 + quarter)
            yq = jnp.concatenate(
                [br_ref[:, qs, :].reshape(DFT_RADIX * quarter, DFT_SLAB).astype(BF16),
                 bi_ref[:, qs, :].reshape(DFT_RADIX * quarter, DFT_SLAB).astype(BF16)], axis=1)
            fq = jnp.dot(yq, cs_ref[...], preferred_element_type=F32).astype(f_ref.dtype)
            for k1 in range(DFT_RADIX):
                f_ref[k1 * n2 + q0:k1 * n2 + q0 + quarter, c0:c0 + DFT_SLAB] = (
                    fq[k1 * quarter:(k1 + 1) * quarter])


def _seqdft_tables(s):
    n2 = s // DFT_RADIX
    k2 = jnp.arange(n2, dtype=jnp.int32)
    s1 = jnp.arange(DFT_RADIX, dtype=jnp.int32)
    unit = 2.0 * jnp.pi / s
    a = ((k2[:, None] * (DFT_RADIX * k2[None, :])) % s).astype(F32) * unit
    b = ((s1[:, None] * k2[None, :]) % s).astype(F32) * unit
    ca, sa = jnp.cos(a)[None], jnp.sin(a)[None]
    cb, sb = jnp.cos(b)[:, :, None], jnp.sin(b)[:, :, None]
    wr = ca * cb - sa * sb
    wi = -(sa * cb + ca * sb)
    return jnp.concatenate([wr, wi], axis=1).astype(BF16)


def _chan_dft_table(gd, s):
    c = jnp.arange(gd, dtype=jnp.int32)
    j = (c[:, None] * c[None, :]) % gd
    ang = j.astype(F32) * (2.0 * jnp.pi / gd)
    scale = 1.0 / jnp.sqrt(jnp.asarray(s * gd, F32))
    eye = jnp.eye(DFT_SLAB // gd, dtype=F32)
    blocks = [jnp.kron(eye, t * scale) for t in (jnp.cos(ang), jnp.sin(ang))]
    return jnp.concatenate(blocks, axis=0).astype(BF16)


def _fourier2d(h_rm, w_tab, cs_tab):
    bsz, n_steps, _, n2, td = h_rm.shape
    s = DFT_RADIX * n2
    slab_scratch = [pltpu.VMEM((DFT_RADIX, n2, DFT_SLAB), F32),
                    pltpu.VMEM((DFT_RADIX, n2, DFT_SLAB), F32)]
    return pl.pallas_call(
        _fourier2d_kernel,
        grid=(bsz, n_steps),
        in_specs=[
            pl.BlockSpec((None, None, DFT_RADIX, n2, td), lambda b, j: (b, j, 0, 0, 0)),
            _resident((DFT_RADIX, 2 * n2, n2), lambda b, j: (0, 0, 0)),
            _resident((2 * DFT_SLAB, DFT_SLAB), lambda b, j: (0, 0)),
        ],
        out_specs=pl.BlockSpec((None, None, s, td), lambda b, j: (b, j, 0, 0)),
        out_shape=jax.ShapeDtypeStruct((bsz, n_steps, s, td), BF16),
        scratch_shapes=slab_scratch * (td // DFT_SLAB),
        compiler_params=_params(2),
        name="fourier2d",
    )(h_rm, w_tab, cs_tab)


def _retproj_kernel(h_ref, w_ref, cos_ref, sin_ref,
                    q_ref, kt_ref, v_ref, sg_ref, wkt_ref, *, dk, chunk):
    h = h_ref[...]
    half = dk // 2
    tm = h.shape[0]
    hdk = RET_HEADS * dk
    dv = v_ref.shape[2]
    dvt = RET_HEADS * dv

    @pl.when((pl.program_id(0) == 0) & (pl.program_id(1) == 0))
    def _():
        for c0 in range(0, hdk, dk):
            wk = w_ref[:, hdk + c0:hdk + c0 + dk].astype(F32)
            wkt_ref[c0:c0 + dk, :] = wk.T.astype(wkt_ref.dtype)

    cos, sin = cos_ref[...], sin_ref[...]
    cost, sint = cos.T, sin.T
    k_scale = dk ** -0.5
    for hd in range(RET_HEADS):
        c0 = hd * dk
        q = jnp.dot(h, w_ref[:, c0:c0 + dk], preferred_element_type=F32)
        q1, q2 = q[:, :half], q[:, half:]
        q_ref[hd, :, :half] = (q1 * cos - q2 * sin).astype(q_ref.dtype)
        q_ref[hd, :, half:] = (q1 * sin + q2 * cos).astype(q_ref.dtype)
        kt = lax.dot_general(wkt_ref[c0:c0 + dk, :], h, (((1,), (1,)), ((), ())),
                             preferred_element_type=F32)
        k1, k2 = kt[:half], kt[half:]
        r1 = (k1 * cost - k2 * sint) * k_scale
        r2 = (k1 * sint + k2 * cost) * k_scale
        for j in range(tm // chunk):
            kt_ref[j, c0:c0 + half, :] = r1[:, j * chunk:(j + 1) * chunk].astype(kt_ref.dtype)
            kt_ref[j, c0 + half:c0 + dk, :] = r2[:, j * chunk:(j + 1) * chunk].astype(kt_ref.dtype)
    step = min(256, dv)
    v0, g0 = 2 * hdk, 2 * hdk + dvt
    for hd in range(RET_HEADS):
        for o in range(0, dv, step):
            c0 = hd * dv + o
            v_ref[hd, :, o:o + step] = jnp.dot(h, w_ref[:, v0 + c0:v0 + c0 + step],
                                               preferred_element_type=F32).astype(v_ref.dtype)
            g = jnp.dot(h, w_ref[:, g0 + c0:g0 + c0 + step], preferred_element_type=F32)
            sg_ref[hd, :, o:o + step] = _silu(g).astype(sg_ref.dtype)


def _retproj(h, w, cos, sin, chunk, tm=ROW_TILE):
    bsz, s, d = h.shape
    hdk = d
    hdv = (w.shape[1] - 2 * hdk) // 2
    dk = hdk // RET_HEADS
    dv = hdv // RET_HEADS
    half = dk // 2
    tm = min(tm, s)
    row = lambda b, i: (b, i, 0)
    head_row = lambda b, i: (b, 0, i, 0)
    return pl.pallas_call(
        functools.partial(_retproj_kernel, dk=dk, chunk=chunk),
        grid=(bsz, s // tm),
        in_specs=[
            pl.BlockSpec((None, tm, d), row),
            _resident(w.shape, lambda b, i: (0, 0)),
            pl.BlockSpec((tm, half), lambda b, i: (i, 0)),
            pl.BlockSpec((tm, half), lambda b, i: (i, 0)),
        ],
        out_specs=[
            pl.BlockSpec((None, RET_HEADS, tm, dk), head_row),
            pl.BlockSpec((None, tm // chunk, hdk, chunk), lambda b, i: (b, i, 0, 0)),
            pl.BlockSpec((None, RET_HEADS, tm, dv), head_row),
            pl.BlockSpec((None, RET_HEADS, tm, dv), head_row),
        ],
        out_shape=[
            jax.ShapeDtypeStruct((bsz, RET_HEADS, s, dk), BF16),
            jax.ShapeDtypeStruct((bsz, s // chunk, hdk, chunk), BF16),
            jax.ShapeDtypeStruct((bsz, RET_HEADS, s, dv), BF16),
            jax.ShapeDtypeStruct((bsz, RET_HEADS, s, dv), BF16),
        ],
        scratch_shapes=[pltpu.VMEM((hdk, d), BF16)],
        compiler_params=_params(2),
        name="ret_proj",
    )(h, w, cos, sin)


def _retcore_kernel(cd_ref, q_ref, kt_ref, v_ref, dm_ref, qdf_ref, qdb_ref, kdf_ref, kdb_ref,
                    y_ref, r_all_ref, rf_ref, rb_ref):
    hd = pl.program_id(1)
    nc, dk, c = kt_ref.shape
    cd_f = cd_ref[0, hd]
    cd_b = cd_ref[1, hd]
    kdf = kdf_ref[...]
    kdb = kdb_ref[...]

    rf_ref[...] = jnp.zeros_like(rf_ref)
    rb_ref[...] = jnp.zeros_like(rb_ref)
    for it in range(nc):
        nf, nb = it, nc - 1 - it
        r_all_ref[nf, :dk, :] = rf_ref[...].astype(r_all_ref.dtype)
        r_all_ref[nb, dk:, :] = rb_ref[...].astype(r_all_ref.dtype)
        if it == nc - 1:
            break
        ktf = (kt_ref[nf].astype(F32) * kdf).astype(BF16)
        ktb = (kt_ref[nb].astype(F32) * kdb).astype(BF16)
        rf_ref[...] = rf_ref[...] * cd_f + jnp.dot(
            ktf, v_ref[nf * c:(nf + 1) * c, :], preferred_element_type=F32)
        rb_ref[...] = rb_ref[...] * cd_b + jnp.dot(
            ktb, v_ref[nb * c:(nb + 1) * c, :], preferred_element_type=F32)

    reps = dk // LANES
    qdf = jnp.tile(qdf_ref[...], (1, reps))
    qdb = jnp.tile(qdb_ref[...], (1, reps))
    for n in range(nc):
        rows = slice(n * c, (n + 1) * c)
        qn = q_ref[rows, :]
        sc = jnp.dot(qn, kt_ref[n], preferred_element_type=F32) * dm_ref[...]
        y = jnp.dot(sc.astype(BF16), v_ref[rows, :], preferred_element_type=F32)
        qf32 = qn.astype(F32)
        qfb = jnp.concatenate([(qf32 * qdf).astype(BF16), (qf32 * qdb).astype(BF16)], axis=1)
        y += jnp.dot(qfb, r_all_ref[n], preferred_element_type=F32)
        mu = jnp.mean(y, axis=-1, keepdims=True)
        yc = y - mu
        var = jnp.mean(yc * yc, axis=-1, keepdims=True)
        y_ref[rows, :] = (yc * lax.rsqrt(var + EPS)).astype(y_ref.dtype)


def _ret_decay_tables(c):
    hidx = jnp.arange(RET_HEADS, dtype=F32)
    lg_f = jnp.log1p(-jnp.exp2(-5.0 - hidx))
    lg_b = jnp.flip(lg_f)
    t = jnp.arange(c, dtype=jnp.int32)
    diff = (t[:, None] - t[None, :])
    fmask = diff >= 0
    bmask = diff < 0
    ef = jnp.exp(jnp.where(fmask, diff, 0).astype(F32)[None] * lg_f[:, None, None])
    eb = jnp.exp(jnp.where(bmask, -diff, 0).astype(F32)[None] * lg_b[:, None, None])
    dm = jnp.where(fmask[None], ef, eb)
    tf = t.astype(F32)
    ones = jnp.ones((1, 1, LANES), F32)
    qdf = jnp.exp((tf + 1.0)[None, :] * lg_f[:, None])[:, :, None] * ones
    qdb = jnp.exp((c - tf)[None, :] * lg_b[:, None])[:, :, None] * ones
    kdf = jnp.exp((c - 1.0 - tf)[None, :] * lg_f[:, None])[:, None, :]
    kdb = jnp.exp(tf[None, :] * lg_b[:, None])[:, None, :]
    cd = jnp.stack([jnp.exp(c * lg_f), jnp.exp(c * lg_b)])
    return cd, dm, qdf, qdb, kdf, kdb


def _retcore(q, kt, v, tables):
    bsz, _, s, dk = q.shape
    dv = v.shape[3]
    nc, c = kt.shape[1], kt.shape[3]
    cd, dm, qdf, qdb, kdf, kdb = tables
    per_head = lambda b, h: (h, 0, 0)
    head_block = lambda b, h: (b, h, 0, 0)
    return pl.pallas_call(
        _retcore_kernel,
        grid=(bsz, RET_HEADS),
        in_specs=[
            pl.BlockSpec(memory_space=pltpu.SMEM),
            pl.BlockSpec((None, None, s, dk), head_block),
            pl.BlockSpec((None, nc, dk, c), lambda b, h: (b, 0, h, 0)),
            pl.BlockSpec((None, None, s, dv), head_block),
            pl.BlockSpec((None, c, c), per_head),
            pl.BlockSpec((None, c, LANES), per_head),
            pl.BlockSpec((None, c, LANES), per_head),
            pl.BlockSpec((None, 1, c), per_head),
            pl.BlockSpec((None, 1, c), per_head),
        ],
        out_specs=pl.BlockSpec((None, None, s, dv), head_block),
        out_shape=jax.ShapeDtypeStruct((bsz, RET_HEADS, s, dv), BF16),
        scratch_shapes=[pltpu.VMEM((nc, 2 * dk, dv), BF16),
                        pltpu.VMEM((dk, dv), F32),
                        pltpu.VMEM((dk, dv), F32)],
        compiler_params=_params(2),
        name="ret_core",
    )(cd, q, kt, v, dm, qdf, qdb, kdf, kdb)


def _mix_ffn_kernel(*refs, fourier, emit, cast_reps):
    n_cast = len(cast_reps)
    n_out = (1 if emit == "final" else 2) + n_cast
    ins = refs[:len(refs) - n_out - 2]
    outs = refs[len(ins):len(ins) + n_out]
    h2_ref, a_ref = refs[len(ins) + n_out:]
    o_ref = outs[0]
    flat_step = pl.program_id(0) * pl.num_programs(1) + pl.program_id(1)
    for src_ref, dst_ref, rep in zip(ins[len(ins) - n_cast:], outs[len(outs) - n_cast:], cast_reps):
        if rep == 1:
            dst_ref[...] = src_ref[...].astype(dst_ref.dtype)
        else:
            @pl.when(flat_step % rep == 0)
            def _(src_ref=src_ref, dst_ref=dst_ref):
                dst_ref[...] = src_ref[...].astype(dst_ref.dtype)
    ins = ins[:len(ins) - n_cast]
    if emit == "x+h_rm":
        ins, perm_ref = ins[:-1], ins[-1]
    x_ref, vec_ref, gain_ref = ins[:3]
    g1, sh2, sc2, g2 = (vec_ref[r:r + 1, :] for r in range(4))
    if fourier:
        f_ref, wo_ref, win_ref, wout_ref = ins[3:]
    else:
        y_ref, sg_ref, wo_ref, win_ref, wout_ref = ins[3:]
    f = wout_ref.shape[0]
    tm = x_ref.shape[0]
    sub = tm // MIX_SUBTILES
    subtiles = [slice(r0, r0 + sub) for r0 in range(0, tm, sub)]
    for rs in subtiles:
        if fourier:
            z = jnp.concatenate([f_ref[k, rs, :] for k in range(f_ref.shape[0])], axis=1)
        else:
            z = jnp.concatenate(
                [(sg_ref[k, rs, :].astype(F32) * y_ref[k, rs, :].astype(F32)).astype(BF16)
                 for k in range(y_ref.shape[0])], axis=1)
        m = jnp.dot(z, wo_ref[...], preferred_element_type=F32)
        x1 = x_ref[rs, :] + g1 * m
        h2_ref[rs, :] = _norm_mod(x1, gain_ref[0:1, :], sh2, sc2).astype(h2_ref.dtype)
        o_ref[rs, :] = x1
    for rs in subtiles:
        for c0 in range(0, f, FFN_CHUNK):
            gate = jnp.dot(h2_ref[rs, :], win_ref[:, c0:c0 + FFN_CHUNK], preferred_element_type=F32)
            up = jnp.dot(h2_ref[rs, :], win_ref[:, f + c0:f + c0 + FFN_CHUNK],
                         preferred_element_type=F32)
            a_ref[rs, c0:c0 + FFN_CHUNK] = (_silu(gate) * up).astype(a_ref.dtype)
    for rs in subtiles:
        ffn = jnp.dot(a_ref[rs, :], wout_ref[...], preferred_element_type=F32)
        x2 = o_ref[rs, :] + g2 * ffn
        hn = _norm_mod(x2, gain_ref[1:2, :], vec_ref[4:5, :], vec_ref[5:6, :])
        if emit == "final":
            o_ref[rs, :] = hn
            continue
        o_ref[rs, :] = x2
        hn_ref = outs[1]
        if emit == "x+h":
            hn_ref[rs, :] = hn.astype(hn_ref.dtype)
        else:
            _store_residue_major(hn, perm_ref, hn_ref, rs.start)


def _cast_rows(rows, n_steps):
    n = n_steps
    while n > 1 and (rows % n or (rows // n) % 16):
        n //= 2
    return rows // n


def _mix_ffn(x, vecs, gains, mix_inputs, wo, win, wout, *, fourier, emit, cast_jobs=(),
             tm=ROW_TILE):
    bsz, s, d = x.shape
    tm = min(tm, s)
    n_tiles = s // tm
    row = lambda b, i: (b, i, 0)
    in_specs = [pl.BlockSpec((None, tm, d), row),
                pl.BlockSpec((None, 8, d), lambda b, i: (b, 0, 0)),
                pl.BlockSpec((8, d), lambda b, i: (0, 0))]
    args = [x, vecs, gains]
    for a in mix_inputs:
        in_specs.append(pl.BlockSpec((None, a.shape[1], tm, a.shape[3]), lambda b, i: (b, 0, i, 0)))
        args.append(a)
    for a in (wo, win, wout):
        in_specs.append(_resident(a.shape, lambda b, i: (0, 0)))
        args.append(a)
    out_specs = [pl.BlockSpec((None, tm, d), row)]
    out_shape = [jax.ShapeDtypeStruct((bsz, s, d), F32)]
    scratch = [pltpu.VMEM((tm, d), BF16), pltpu.VMEM((tm, wout.shape[0]), BF16)]
    if emit == "x+h":
        out_specs.append(pl.BlockSpec((None, tm, d), row))
        out_shape.append(jax.ShapeDtypeStruct((bsz, s, d), BF16))
    elif emit == "x+h_rm":
        out_specs.append(_rm_out_spec(tm, d))
        out_shape.append(jax.ShapeDtypeStruct(_rm_shape(bsz, s, d), BF16))
        assert tm % PERM_ROWS == 0
        in_specs.append(_resident((PERM_ROWS, PERM_ROWS), lambda b, i: (0, 0)))
        args.append(_residue_perm())
    cast_reps = []
    for stack, layer in cast_jobs:
        _, rows, cols = stack.shape
        rp = _cast_rows(rows, bsz * n_tiles)
        rep = bsz * n_tiles // (rows // rp)
        cast_reps.append(rep)
        in_specs.append(pl.BlockSpec(
            (None, rp, cols), lambda b, i, layer=layer, rep=rep: (layer, (b * n_tiles + i) // rep, 0)))
        args.append(stack)
        out_specs.append(pl.BlockSpec((rp, cols), lambda b, i, rep=rep: ((b * n_tiles + i) // rep, 0)))
        out_shape.append(jax.ShapeDtypeStruct((rows, cols), BF16))
    return pl.pallas_call(
        functools.partial(_mix_ffn_kernel, fourier=fourier, emit=emit, cast_reps=tuple(cast_reps)),
        grid=(bsz, s // tm),
        in_specs=in_specs,
        out_specs=out_specs,
        out_shape=out_shape,
        scratch_shapes=scratch,
        compiler_params=_params(2),
        name="mix_ffn_fourier" if fourier else "mix_ffn_ret",
    )(*args)


def _rotary_tables(s, half):
    inv_freq = ROPE_BASE ** (-jnp.arange(half, dtype=F32) / half)
    ang = jnp.arange(s, dtype=F32)[:, None] * inv_freq[None, :]
    return jnp.cos(ang), jnp.sin(ang)


def kernel(x, c, w_ada, b_ada, norm_mix_g, norm_ffn_g, w_fourier_out, w_ret_in, w_ret_out,
           w_ffn_in, w_ffn_out, final_norm_g, w_ada_final, b_ada_final):
    bsz, s, d = x.shape
    depth = w_ada.shape[0]
    gd = d // FOURIER_GROUPS
    dk = d // RET_HEADS
    chunk = min(RET_CHUNK, s)

    mods = _ada(c, w_ada, b_ada).reshape(depth, bsz, 6, d)
    mod_final = _ada(c, w_ada_final[None], b_ada_final[None]).reshape(bsz, 2, d)

    seq_tab = _seqdft_tables(s)
    cs_tab = _chan_dft_table(gd, s)
    cos, sin = _rotary_tables(s, dk // 2)
    ret_tabs = _ret_decay_tables(chunk)

    def is_fourier(i):
        return i % N_MIXERS == 0

    def mixer_weights(i):
        j = i // N_MIXERS
        return [(w_fourier_out, j)] if is_fourier(i) else [(w_ret_in, j), (w_ret_out, j)]

    w_mix = [stack[j].astype(BF16) for stack, j in mixer_weights(0)]
    w_in, w_out = w_ffn_in[0].astype(BF16), w_ffn_out[0].astype(BF16)

    h = _prenorm(x, norm_mix_g[0], mods[0, :, 0], mods[0, :, 1], residue_major=is_fourier(0),
                 out_dtype=BF16, tm=4 * ROW_TILE)
    for i in range(depth):
        last = i == depth - 1
        if last:
            next_mod, next_gain, emit = mod_final, final_norm_g, "final"
        else:
            next_mod, next_gain = mods[i + 1, :, 0:2], norm_mix_g[i + 1]
            emit = "x+h_rm" if is_fourier(i + 1) else "x+h"
        pad = jnp.zeros((bsz, 2, d), F32)
        vecs = jnp.concatenate([mods[i, :, 2:6], next_mod, pad], axis=1)
        gains = jnp.concatenate([norm_ffn_g[i][None], next_gain[None], jnp.zeros((6, d), F32)])
        jobs = () if last else tuple(
            mixer_weights(i + 1) + [(w_ffn_in, i + 1), (w_ffn_out, i + 1)])
        if is_fourier(i):
            mix_in = (_fourier2d(h, seq_tab, cs_tab),)
        else:
            q, kt, v, sg = _retproj(h, w_mix[0], cos, sin, chunk, tm=2 * ROW_TILE)
            mix_in = (_retcore(q, kt, v, ret_tabs), sg)
        outs = _mix_ffn(x, vecs, gains, mix_in, w_mix[-1], w_in, w_out,
                        fourier=is_fourier(i), emit=emit, cast_jobs=jobs)
        if last:
            return outs[0]
        x, h = outs[:2]
        *w_mix, w_in, w_out = outs[2:]
```

```python
import functools
import math

import jax
import jax.numpy as jnp
from jax import lax
from jax.experimental import pallas as pl
from jax.experimental.pallas import tpu as pltpu

EPS = 1e-6
N_MIXERS = 2
FOURIER_GROUPS = 4
RET_HEADS = 4
ROPE_BASE = 10000.0

DFT_RADIX = 16
RET_CHUNK = 256
FFN_CHUNK = 256
LANES = 128
ROW_TILE = 512
FOURIER_STEP_COLS = 512
MIX_SUBTILES = 2
VMEM_LIMIT_BYTES = 56 * 1024 * 1024

F32 = jnp.float32
BF16 = jnp.bfloat16


def _params(n_grid_dims):
    return pltpu.CompilerParams(
        dimension_semantics=("arbitrary",) * n_grid_dims,
        vmem_limit_bytes=VMEM_LIMIT_BYTES)


def _resident(block_shape, index_map):
    return pl.BlockSpec(block_shape, index_map, pipeline_mode=pl.Buffered(1))


def _silu(v):
    return v * jax.nn.sigmoid(v)


def _norm_mod(x, g, shift, scale):
    ms = jnp.mean(x * x, axis=-1, keepdims=True)
    y = x * lax.rsqrt(ms + EPS)
    return (y * g) * (1.0 + scale) + shift


def _ada_kernel(c_ref, w_ref, b_ref, o_ref):
    ca = _silu(c_ref[...])
    w = w_ref[...]
    ca_hi = ca.astype(BF16)
    ca_lo = (ca - ca_hi.astype(F32)).astype(BF16)
    w_hi = w.astype(BF16)
    w_lo = (w - w_hi.astype(F32)).astype(BF16)
    nb = ca.shape[0]
    p = jnp.dot(jnp.concatenate([ca_hi, ca_lo], axis=0), w_hi, preferred_element_type=F32)
    q = jnp.dot(ca_hi, w_lo, preferred_element_type=F32)
    o_ref[...] = (p[:nb] + p[nb:]) + q + b_ref[...]


def _ada(c, w, b, tn=2048):
    n_layers, d, n = w.shape
    bsz = c.shape[0]
    tn = min(tn, n)
    while n % tn:
        tn //= 2
    return pl.pallas_call(
        _ada_kernel,
        grid=(n_layers, n // tn),
        in_specs=[
            pl.BlockSpec((bsz, d), lambda l, j: (0, 0)),
            pl.BlockSpec((None, d, tn), lambda l, j: (l, 0, j)),
            pl.BlockSpec((None, 1, tn), lambda l, j: (l, 0, j)),
        ],
        out_specs=pl.BlockSpec((None, bsz, tn), lambda l, j: (l, 0, j)),
        out_shape=jax.ShapeDtypeStruct((n_layers, bsz, n), F32),
        compiler_params=_params(2),
        name="ada_mod",
    )(c, w, b.reshape(n_layers, 1, n))


PERM_ROWS = DFT_RADIX * DFT_RADIX


def _residue_perm():
    r = jnp.arange(PERM_ROWS, dtype=jnp.int32)
    src = (r % DFT_RADIX) * DFT_RADIX + r // DFT_RADIX
    return (src[:, None] == r[None, :]).astype(BF16)


def _store_residue_major(h, p_ref, o_ref, row0=0):
    hb = h.astype(o_ref.dtype)
    slab = o_ref.shape[3]
    for r0 in range(0, h.shape[0], PERM_ROWS):
        p = jnp.dot(p_ref[...], hb[r0:r0 + PERM_ROWS], preferred_element_type=F32).astype(o_ref.dtype)
        j0 = (row0 + r0) // DFT_RADIX
        for k in range(o_ref.shape[0]):
            for s1 in range(DFT_RADIX):
                o_ref[k, s1, j0:j0 + DFT_RADIX, :] = (
                    p[s1 * DFT_RADIX:(s1 + 1) * DFT_RADIX, k * slab:(k + 1) * slab])


def _rm_shape(bsz, s, d):
    slab = min(FOURIER_STEP_COLS, d)
    return (bsz, d // slab, DFT_RADIX, s // DFT_RADIX, slab)


def _rm_out_spec(tm, d):
    slab = min(FOURIER_STEP_COLS, d)
    return pl.BlockSpec((None, d // slab, DFT_RADIX, tm // DFT_RADIX, slab),
                        lambda b, i: (b, 0, 0, i, 0))


def _prenorm_kernel(x_ref, g_ref, sh_ref, sc_ref, *rest, residue_major):
    h = _norm_mod(x_ref[...], g_ref[...], sh_ref[...], sc_ref[...])
    if residue_major:
        p_ref, o_ref = rest
        _store_residue_major(h, p_ref, o_ref)
    else:
        o_ref, = rest
        o_ref[...] = h.astype(o_ref.dtype)


def _prenorm(x, g, shift, scale, *, residue_major, out_dtype, tm=ROW_TILE):
    bsz, s, d = x.shape
    tm = min(tm, s)
    vec = lambda a: a.reshape(bsz, 1, d)
    in_specs = [
        pl.BlockSpec((None, tm, d), lambda b, i: (b, i, 0)),
        pl.BlockSpec((1, d), lambda b, i: (0, 0)),
        pl.BlockSpec((None, 1, d), lambda b, i: (b, 0, 0)),
        pl.BlockSpec((None, 1, d), lambda b, i: (b, 0, 0)),
    ]
    args = [x, g.reshape(1, d), vec(shift), vec(scale)]
    if residue_major:
        assert tm % PERM_ROWS == 0
        in_specs.append(_resident((PERM_ROWS, PERM_ROWS), lambda b, i: (0, 0)))
        args.append(_residue_perm())
        out_shape = jax.ShapeDtypeStruct(_rm_shape(bsz, s, d), out_dtype)
        out_spec = _rm_out_spec(tm, d)
    else:
        out_shape = jax.ShapeDtypeStruct((bsz, s, d), out_dtype)
        out_spec = pl.BlockSpec((None, tm, d), lambda b, i: (b, i, 0))
    return pl.pallas_call(
        functools.partial(_prenorm_kernel, residue_major=residue_major),
        grid=(bsz, s // tm),
        in_specs=in_specs,
        out_specs=out_spec,
        out_shape=out_shape,
        compiler_params=_params(2),
        name="prenorm_rm" if residue_major else "prenorm",
    )(*args)


def _dft4(x):
    (x0r, x0i), (x1r, x1i), (x2r, x2i), (x3r, x3i) = x
    t0r, t0i = x0r + x2r, x0i + x2i
    t1r, t1i = x0r - x2r, x0i - x2i
    t2r, t2i = x1r + x3r, x1i + x3i
    t3r, t3i = x1r - x3r, x1i - x3i
    return [(t0r + t2r, t0i + t2i),
            (t1r + t3i, t1i - t3r),
            (t0r - t2r, t0i - t2i),
            (t1r - t3i, t1i + t3r)]


def _twiddle16(z, p):
    zr, zi = z
    p = p % 16
    if p == 0:
        return zr, zi
    if p == 4:
        return zi, -zr
    if p == 8:
        return -zr, -zi
    if p == 12:
        return -zi, zr
    if p % 4 == 2:
        c = math.sqrt(0.5)
        sr = -c if p in (6, 10) else c
        si = -c if p in (2, 6) else c
        add, sub = zr + zi, zr - zi
        real = add * sr if sr == -si else sub * sr
        imag = sub * si if sr == -si else add * si
        return real, imag
    cr = math.cos(2.0 * math.pi * p / 16.0)
    ci = -math.sin(2.0 * math.pi * p / 16.0)
    return zr * cr - zi * ci, zr * ci + zi * cr


DFT_SLAB = 256


def _fourier2d_kernel(h_ref, w_ref, cs_ref, f_ref, *scratch):
    n2 = h_ref.shape[1]
    td = h_ref.shape[2]
    rows = 8
    slabs = [(c0, scratch[2 * i], scratch[2 * i + 1])
             for i, c0 in enumerate(range(0, td, DFT_SLAB))]
    for c0, br_ref, bi_ref in slabs:
        for s1 in range(DFT_RADIX):
            p = jnp.dot(w_ref[s1], h_ref[s1, :, c0:c0 + DFT_SLAB], preferred_element_type=F32)
            br_ref[s1] = p[:n2]
            bi_ref[s1] = p[n2:]

    quarter = max(n2 // 4, rows)
    for c0, br_ref, bi_ref in slabs:
        for q0 in range(0, n2, quarter):
            for r in range(q0, q0 + quarter, rows):
                rs = slice(r, r + rows)
                for l0 in range(0, DFT_SLAB, LANES):
                    ls = slice(l0, l0 + LANES)
                    us = [None] * DFT_RADIX
                    for b in range(4):
                        xs = [(br_ref[4 * a + b, rs, ls], bi_ref[4 * a + b, rs, ls])
                              for a in range(4)]
                        ub = _dft4(xs)
                        for c in range(4):
                            us[4 * c + b] = _twiddle16(ub[c], b * c)
                    for c in range(4):
                        ys = _dft4(us[4 * c:4 * c + 4])
                        for d in range(4):
                            br_ref[c + 4 * d, rs, ls] = ys[d][0]
                            bi_ref[c + 4 * d, rs, ls] = ys[d][1]
            qs = slice(q0, q0 + quarter)
            yq = jnp.concatenate(
                [br_ref[:, qs, :].reshape(DFT_RADIX * quarter, DFT_SLAB).astype(BF16),
                 bi_ref[:, qs, :].reshape(DFT_RADIX * quarter, DFT_SLAB).astype(BF16)], axis=1)
            fq = jnp.dot(yq, cs_ref[...], preferred_element_type=F32).astype(f_ref.dtype)
            for k1 in range(DFT_RADIX):
                f_ref[k1 * n2 + q0:k1 * n2 + q0 + quarter, c0:c0 + DFT_SLAB] = (
                    fq[k1 * quarter:(k1 + 1) * quarter])


def _seqdft_tables(s):
    n2 = s // DFT_RADIX
    k2 = jnp.arange(n2, dtype=jnp.int32)
    s1 = jnp.arange(DFT_RADIX, dtype=jnp.int32)
    unit = 2.0 * jnp.pi / s
    a = ((k2[:, None] * (DFT_RADIX * k2[None, :])) % s).astype(F32) * unit
    b = ((s1[:, None] * k2[None, :]) % s).astype(F32) * unit
    ca, sa = jnp.cos(a)[None], jnp.sin(a)[None]
    cb, sb = jnp.cos(b)[:, :, None], jnp.sin(b)[:, :, None]
    wr = ca * cb - sa * sb
    wi = -(sa * cb + ca * sb)
    return jnp.concatenate([wr, wi], axis=1).astype(BF16)


def _chan_dft_table(gd, s):
    c = jnp.arange(gd, dtype=jnp.int32)
    j = (c[:, None] * c[None, :]) % gd
    ang = j.astype(F32) * (2.0 * jnp.pi / gd)
    scale = 1.0 / jnp.sqrt(jnp.asarray(s * gd, F32))
    eye = jnp.eye(DFT_SLAB // gd, dtype=F32)
    blocks = [jnp.kron(eye, t * scale) for t in (jnp.cos(ang), jnp.sin(ang))]
    return jnp.concatenate(blocks, axis=0).astype(BF16)


def _fourier2d(h_rm, w_tab, cs_tab):
    bsz, n_steps, _, n2, td = h_rm.shape
    s = DFT_RADIX * n2
    slab_scratch = [pltpu.VMEM((DFT_RADIX, n2, DFT_SLAB), F32),
                    pltpu.VMEM((DFT_RADIX, n2, DFT_SLAB), F32)]
    return pl.pallas_call(
        _fourier2d_kernel,
        grid=(bsz, n_steps),
        in_specs=[
            pl.BlockSpec((None, None, DFT_RADIX, n2, td), lambda b, j: (b, j, 0, 0, 0)),
            _resident((DFT_RADIX, 2 * n2, n2), lambda b, j: (0, 0, 0)),
            _resident((2 * DFT_SLAB, DFT_SLAB), lambda b, j: (0, 0)),
        ],
        out_specs=pl.BlockSpec((None, None, s, td), lambda b, j: (b, j, 0, 0)),
        out_shape=jax.ShapeDtypeStruct((bsz, n_steps, s, td), BF16),
        scratch_shapes=slab_scratch * (td // DFT_SLAB),
        compiler_params=_params(2),
        name="fourier2d",
    )(h_rm, w_tab, cs_tab)


def _retproj_kernel(h_ref, w_ref, cos_ref, sin_ref,
                    q_ref, kt_ref, v_ref, sg_ref, wkt_ref, *, dk, chunk):
    h = h_ref[...]
    half = dk // 2
    tm = h.shape[0]
    hdk = RET_HEADS * dk
    dv = v_ref.shape[2]
    dvt = RET_HEADS * dv

    @pl.when((pl.program_id(0) == 0) & (pl.program_id(1) == 0))
    def _():
        for c0 in range(0, hdk, dk):
            wk = w_ref[:, hdk + c0:hdk + c0 + dk].astype(F32)
            wkt_ref[c0:c0 + dk, :] = wk.T.astype(wkt_ref.dtype)

    cos, sin = cos_ref[...], sin_ref[...]
    cost, sint = cos.T, sin.T
    k_scale = dk ** -0.5
    for hd in range(RET_HEADS):
        c0 = hd * dk
        q = jnp.dot(h, w_ref[:, c0:c0 + dk], preferred_element_type=F32)
        q1, q2 = q[:, :half], q[:, half:]
        q_ref[hd, :, :half] = (q1 * cos - q2 * sin).astype(q_ref.dtype)
        q_ref[hd, :, half:] = (q1 * sin + q2 * cos).astype(q_ref.dtype)
        kt = lax.dot_general(wkt_ref[c0:c0 + dk, :], h, (((1,), (1,)), ((), ())),
                             preferred_element_type=F32)
        k1, k2 = kt[:half], kt[half:]
        r1 = (k1 * cost - k2 * sint) * k_scale
        r2 = (k1 * sint + k2 * cost) * k_scale
        for j in range(tm // chunk):
            kt_ref[j, c0:c0 + half, :] = r1[:, j * chunk:(j + 1) * chunk].astype(kt_ref.dtype)
            kt_ref[j, c0 + half:c0 + dk, :] = r2[:, j * chunk:(j + 1) * chunk].astype(kt_ref.dtype)
    step = min(256, dv)
    v0, g0 = 2 * hdk, 2 * hdk + dvt
    for hd in range(RET_HEADS):
        for o in range(0, dv, step):
            c0 = hd * dv + o
            v_ref[hd, :, o:o + step] = jnp.dot(h, w_ref[:, v0 + c0:v0 + c0 + step],
                                               preferred_element_type=F32).astype(v_ref.dtype)
            g = jnp.dot(h, w_ref[:, g0 + c0:g0 + c0 + step], preferred_element_type=F32)
            sg_ref[hd, :, o:o + step] = _silu(g).astype(sg_ref.dtype)


def _retproj(h, w, cos, sin, chunk, tm=ROW_TILE):
    bsz, s, d = h.shape
    hdk = d
    hdv = (w.shape[1] - 2 * hdk) // 2
    dk = hdk // RET_HEADS
    dv = hdv // RET_HEADS
    half = dk // 2
    tm = min(tm, s)
    row = lambda b, i: (b, i, 0)
    head_row = lambda b, i: (b, 0, i, 0)
    return pl.pallas_call(
        functools.partial(_retproj_kernel, dk=dk, chunk=chunk),
        grid=(bsz, s // tm),
        in_specs=[
            pl.BlockSpec((None, tm, d), row),
            _resident(w.shape, lambda b, i: (0, 0)),
            pl.BlockSpec((tm, half), lambda b, i: (i, 0)),
            pl.BlockSpec((tm, half), lambda b, i: (i, 0)),
        ],
        out_specs=[
            pl.BlockSpec((None, RET_HEADS, tm, dk), head_row),
            pl.BlockSpec((None, tm // chunk, hdk, chunk), lambda b, i: (b, i, 0, 0)),
            pl.BlockSpec((None, RET_HEADS, tm, dv), head_row),
            pl.BlockSpec((None, RET_HEADS, tm, dv), head_row),
        ],
        out_shape=[
            jax.ShapeDtypeStruct((bsz, RET_HEADS, s, dk), BF16),
            jax.ShapeDtypeStruct((bsz, s // chunk, hdk, chunk), BF16),
            jax.ShapeDtypeStruct((bsz, RET_HEADS, s, dv), BF16),
            jax.ShapeDtypeStruct((bsz, RET_HEADS, s, dv), BF16),
        ],
        scratch_shapes=[pltpu.VMEM((hdk, d), BF16)],
        compiler_params=_params(2),
        name="ret_proj",
    )(h, w, cos, sin)


def _retcore_kernel(cd_ref, q_ref, kt_ref, v_ref, dm_ref, qdf_ref, qdb_ref, kdf_ref, kdb_ref,
                    y_ref, r_all_ref, rf_ref, rb_ref):
    hd = pl.program_id(1)
    nc, dk, c = kt_ref.shape
    cd_f = cd_ref[0, hd]
    cd_b = cd_ref[1, hd]
    kdf = kdf_ref[...]
    kdb = kdb_ref[...]

    rf_ref[...] = jnp.zeros_like(rf_ref)
    rb_ref[...] = jnp.zeros_like(rb_ref)
    for it in range(nc):
        nf, nb = it, nc - 1 - it
        r_all_ref[nf, :dk, :] = rf_ref[...].astype(r_all_ref.dtype)
        r_all_ref[nb, dk:, :] = rb_ref[...].astype(r_all_ref.dtype)
        if it == nc - 1:
            break
        ktf = (kt_ref[nf].astype(F32) * kdf).astype(BF16)
        ktb = (kt_ref[nb].astype(F32) * kdb).astype(BF16)
        rf_ref[...] = rf_ref[...] * cd_f + jnp.dot(
            ktf, v_ref[nf * c:(nf + 1) * c, :], preferred_element_type=F32)
        rb_ref[...] = rb_ref[...] * cd_b + jnp.dot(
            ktb, v_ref[nb * c:(nb + 1) * c, :], preferred_element_type=F32)

    reps = dk // LANES
    qdf = jnp.tile(qdf_ref[...], (1, reps))
    qdb = jnp.tile(qdb_ref[...], (1, reps))
    for n in range(nc):
        rows = slice(n * c, (n + 1) * c)
        qn = q_ref[rows, :]
        sc = jnp.dot(qn, kt_ref[n], preferred_element_type=F32) * dm_ref[...]
        y = jnp.dot(sc.astype(BF16), v_ref[rows, :], preferred_element_type=F32)
        qf32 = qn.astype(F32)
        qfb = jnp.concatenate([(qf32 * qdf).astype(BF16), (qf32 * qdb).astype(BF16)], axis=1)
        y += jnp.dot(qfb, r_all_ref[n], preferred_element_type=F32)
        mu = jnp.mean(y, axis=-1, keepdims=True)
        yc = y - mu
        var = jnp.mean(yc * yc, axis=-1, keepdims=True)
        y_ref[rows, :] = (yc * lax.rsqrt(var + EPS)).astype(y_ref.dtype)


def _ret_decay_tables(c):
    hidx = jnp.arange(RET_HEADS, dtype=F32)
    lg_f = jnp.log1p(-jnp.exp2(-5.0 - hidx))
    lg_b = jnp.flip(lg_f)
    t = jnp.arange(c, dtype=jnp.int32)
    diff = (t[:, None] - t[None, :])
    fmask = diff >= 0
    bmask = diff < 0
    ef = jnp.exp(jnp.where(fmask, diff, 0).astype(F32)[None] * lg_f[:, None, None])
    eb = jnp.exp(jnp.where(bmask, -diff, 0).astype(F32)[None] * lg_b[:, None, None])
    dm = jnp.where(fmask[None], ef, eb)
    tf = t.astype(F32)
    ones = jnp.ones((1, 1, LANES), F32)
    qdf = jnp.exp((tf + 1.0)[None, :] * lg_f[:, None])[:, :, None] * ones
    qdb = jnp.exp((c - tf)[None, :] * lg_b[:, None])[:, :, None] * ones
    kdf = jnp.exp((c - 1.0 - tf)[None, :] * lg_f[:, None])[:, None, :]
    kdb = jnp.exp(tf[None, :] * lg_b[:, None])[:, None, :]
    cd = jnp.stack([jnp.exp(c * lg_f), jnp.exp(c * lg_b)])
    return cd, dm, qdf, qdb, kdf, kdb


def _retcore(q, kt, v, tables):
    bsz, _, s, dk = q.shape
    dv = v.shape[3]
    nc, c = kt.shape[1], kt.shape[3]
    cd, dm, qdf, qdb, kdf, kdb = tables
    per_head = lambda b, h: (h, 0, 0)
    head_block = lambda b, h: (b, h, 0, 0)
    return pl.pallas_call(
        _retcore_kernel,
        grid=(bsz, RET_HEADS),
        in_specs=[
            pl.BlockSpec(memory_space=pltpu.SMEM),
            pl.BlockSpec((None, None, s, dk), head_block),
            pl.BlockSpec((None, nc, dk, c), lambda b, h: (b, 0, h, 0)),
            pl.BlockSpec((None, None, s, dv), head_block),
            pl.BlockSpec((None, c, c), per_head),
            pl.BlockSpec((None, c, LANES), per_head),
            pl.BlockSpec((None, c, LANES), per_head),
            pl.BlockSpec((None, 1, c), per_head),
            pl.BlockSpec((None, 1, c), per_head),
        ],
        out_specs=pl.BlockSpec((None, None, s, dv), head_block),
        out_shape=jax.ShapeDtypeStruct((bsz, RET_HEADS, s, dv), BF16),
        scratch_shapes=[pltpu.VMEM((nc, 2 * dk, dv), BF16),
                        pltpu.VMEM((dk, dv), F32),
                        pltpu.VMEM((dk, dv), F32)],
        compiler_params=_params(2),
        name="ret_core",
    )(cd, q, kt, v, dm, qdf, qdb, kdf, kdb)


def _mix_ffn_kernel(*refs, fourier, emit, cast_reps):
    n_cast = len(cast_reps)
    n_out = (1 if emit == "final" else 2) + n_cast
    ins = refs[:len(refs) - n_out - 2]
    outs = refs[len(ins):len(ins) + n_out]
    h2_ref, a_ref = refs[len(ins) + n_out:]
    o_ref = outs[0]
    flat_step = pl.program_id(0) * pl.num_programs(1) + pl.program_id(1)
    for src_ref, dst_ref, rep in zip(ins[len(ins) - n_cast:], outs[len(outs) - n_cast:], cast_reps):
        if rep == 1:
            dst_ref[...] = src_ref[...].astype(dst_ref.dtype)
        else:
            @pl.when(flat_step % rep == 0)
            def _(src_ref=src_ref, dst_ref=dst_ref):
                dst_ref[...] = src_ref[...].astype(dst_ref.dtype)
    ins = ins[:len(ins) - n_cast]
    if emit == "x+h_rm":
        ins, perm_ref = ins[:-1], ins[-1]
    x_ref, vec_ref, gain_ref = ins[:3]
    g1, sh2, sc2, g2 = (vec_ref[r:r + 1, :] for r in range(4))
    if fourier:
        f_ref, wo_ref, win_ref, wout_ref = ins[3:]
    else:
        y_ref, sg_ref, wo_ref, win_ref, wout_ref = ins[3:]
    f = wout_ref.shape[0]
    tm = x_ref.shape[0]
    sub = tm // MIX_SUBTILES
    subtiles = [slice(r0, r0 + sub) for r0 in range(0, tm, sub)]
    for rs in subtiles:
        if fourier:
            z = jnp.concatenate([f_ref[k, rs, :] for k in range(f_ref.shape[0])], axis=1)
        else:
            z = jnp.concatenate(
                [sg_ref[k, rs, :] * y_ref[k, rs, :] for k in range(y_ref.shape[0])], axis=1)
        m = jnp.dot(z, wo_ref[...], preferred_element_type=F32)
        x1 = x_ref[rs, :] + g1 * m
        h2_ref[rs, :] = _norm_mod(x1, gain_ref[0:1, :], sh2, sc2).astype(h2_ref.dtype)
        o_ref[rs, :] = x1
    for rs in subtiles:
        for c0 in range(0, f, FFN_CHUNK):
            gate = jnp.dot(h2_ref[rs, :], win_ref[:, c0:c0 + FFN_CHUNK], preferred_element_type=F32)
            up = jnp.dot(h2_ref[rs, :], win_ref[:, f + c0:f + c0 + FFN_CHUNK],
                         preferred_element_type=F32)
            a_ref[rs, c0:c0 + FFN_CHUNK] = (_silu(gate) * up).astype(a_ref.dtype)
    for rs in subtiles:
        ffn = jnp.dot(a_ref[rs, :], wout_ref[...], preferred_element_type=F32)
        x2 = o_ref[rs, :] + g2 * ffn
        hn = _norm_mod(x2, gain_ref[1:2, :], vec_ref[4:5, :], vec_ref[5:6, :])
        if emit == "final":
            o_ref[rs, :] = hn
            continue
        o_ref[rs, :] = x2
        hn_ref = outs[1]
        if emit == "x+h":
            hn_ref[rs, :] = hn.astype(hn_ref.dtype)
        else:
            _store_residue_major(hn, perm_ref, hn_ref, rs.start)


def _cast_rows(rows, n_steps):
    n = n_steps
    while n > 1 and (rows % n or (rows // n) % 16):
        n //= 2
    return rows // n


def _mix_ffn(x, vecs, gains, mix_inputs, wo, win, wout, *, fourier, emit, cast_jobs=(),
             tm=ROW_TILE):
    bsz, s, d = x.shape
    tm = min(tm, s)
    n_tiles = s // tm
    row = lambda b, i: (b, i, 0)
    in_specs = [pl.BlockSpec((None, tm, d), row),
                pl.BlockSpec((None, 8, d), lambda b, i: (b, 0, 0)),
                pl.BlockSpec((8, d), lambda b, i: (0, 0))]
    args = [x, vecs, gains]
    for a in mix_inputs:
        in_specs.append(pl.BlockSpec((None, a.shape[1], tm, a.shape[3]), lambda b, i: (b, 0, i, 0)))
        args.append(a)
    for a in (wo, win, wout):
        in_specs.append(_resident(a.shape, lambda b, i: (0, 0)))
        args.append(a)
    out_specs = [pl.BlockSpec((None, tm, d), row)]
    out_shape = [jax.ShapeDtypeStruct((bsz, s, d), F32)]
    scratch = [pltpu.VMEM((tm, d), BF16), pltpu.VMEM((tm, wout.shape[0]), BF16)]
    if emit == "x+h":
        out_specs.append(pl.BlockSpec((None, tm, d), row))
        out_shape.append(jax.ShapeDtypeStruct((bsz, s, d), BF16))
    elif emit == "x+h_rm":
        out_specs.append(_rm_out_spec(tm, d))
        out_shape.append(jax.ShapeDtypeStruct(_rm_shape(bsz, s, d), BF16))
        assert tm % PERM_ROWS == 0
        in_specs.append(_resident((PERM_ROWS, PERM_ROWS), lambda b, i: (0, 0)))
        args.append(_residue_perm())
    cast_reps = []
    for stack, layer in cast_jobs:
        _, rows, cols = stack.shape
        rp = _cast_rows(rows, bsz * n_tiles)
        rep = bsz * n_tiles // (rows // rp)
        cast_reps.append(rep)
        in_specs.append(pl.BlockSpec(
            (None, rp, cols), lambda b, i, layer=layer, rep=rep: (layer, (b * n_tiles + i) // rep, 0)))
        args.append(stack)
        out_specs.append(pl.BlockSpec((rp, cols), lambda b, i, rep=rep: ((b * n_tiles + i) // rep, 0)))
        out_shape.append(jax.ShapeDtypeStruct((rows, cols), BF16))
    return pl.pallas_call(
        functools.partial(_mix_ffn_kernel, fourier=fourier, emit=emit, cast_reps=tuple(cast_reps)),
        grid=(bsz, s // tm),
        in_specs=in_specs,
        out_specs=out_specs,
        out_shape=out_shape,
        scratch_shapes=scratch,
        compiler_params=_params(2),
        name="mix_ffn_fourier" if fourier else "mix_ffn_ret",
    )(*args)


def _rotary_tables(s, half):
    inv_freq = ROPE_BASE ** (-jnp.arange(half, dtype=F32) / half)
    ang = jnp.arange(s, dtype=F32)[:, None] * inv_freq[None, :]
    return jnp.cos(ang), jnp.sin(ang)


def kernel(x, c, w_ada, b_ada, norm_mix_g, norm_ffn_g, w_fourier_out, w_ret_in, w_ret_out,
           w_ffn_in, w_ffn_out, final_norm_g, w_ada_final, b_ada_final):
    bsz, s, d = x.shape
    depth = w_ada.shape[0]
    gd = d // FOURIER_GROUPS
    dk = d // RET_HEADS
    chunk = min(RET_CHUNK, s)

    mods = _ada(c, w_ada, b_ada).reshape(depth, bsz, 6, d)
    mod_final = _ada(c, w_ada_final[None], b_ada_final[None]).reshape(bsz, 2, d)

    seq_tab = _seqdft_tables(s)
    cs_tab = _chan_dft_table(gd, s)
    cos, sin = _rotary_tables(s, dk // 2)
    ret_tabs = _ret_decay_tables(chunk)

    def is_fourier(i):
        return i % N_MIXERS == 0

    def mixer_weights(i):
        j = i // N_MIXERS
        return [(w_fourier_out, j)] if is_fourier(i) else [(w_ret_in, j), (w_ret_out, j)]

    w_mix = [stack[j].astype(BF16) for stack, j in mixer_weights(0)]
    w_in, w_out = w_ffn_in[0].astype(BF16), w_ffn_out[0].astype(BF16)

    h = _prenorm(x, norm_mix_g[0], mods[0, :, 0], mods[0, :, 1], residue_major=is_fourier(0),
                 out_dtype=BF16, tm=4 * ROW_TILE)
    for i in range(depth):
        last = i == depth - 1
        if last:
            next_mod, next_gain, emit = mod_final, final_norm_g, "final"
        else:
            next_mod, next_gain = mods[i + 1, :, 0:2], norm_mix_g[i + 1]
            emit = "x+h_rm" if is_fourier(i + 1) else "x+h"
        pad = jnp.zeros((bsz, 2, d), F32)
        vecs = jnp.concatenate([mods[i, :, 2:6], next_mod, pad], axis=1)
        gains = jnp.concatenate([norm_ffn_g[i][None], next_gain[None], jnp.zeros((6, d), F32)])
        jobs = () if last else tuple(
            mixer_weights(i + 1) + [(w_ffn_in, i + 1), (w_ffn_out, i + 1)])
        if is_fourier(i):
            mix_in = (_fourier2d(h, seq_tab, cs_tab),)
        else:
            q, kt, v, sg = _retproj(h, w_mix[0], cos, sin, chunk, tm=2 * ROW_TILE)
            mix_in = (_retcore(q, kt, v, ret_tabs), sg)
        outs = _mix_ffn(x, vecs, gains, mix_in, w_mix[-1], w_in, w_out,
                        fourier=is_fourier(i), emit=emit, cast_jobs=jobs)
        if last:
            return outs[0]
        x, h = outs[:2]
        *w_mix, w_in, w_out = outs[2:]
```

```python
import functools
import math

import jax
import jax.numpy as jnp
from jax import lax
from jax.experimental import pallas as pl
from jax.experimental.pallas import tpu as pltpu

EPS = 1e-6
N_MIXERS = 2
FOURIER_GROUPS = 4
RET_HEADS = 4
ROPE_BASE = 10000.0

DFT_RADIX = 16
RET_CHUNK = 256
FFN_CHUNK = 256
LANES = 128
ROW_TILE = 512
FOURIER_STEP_COLS = 512
MIX_SUBTILES = 2
VMEM_LIMIT_BYTES = 56 * 1024 * 1024

F32 = jnp.float32
BF16 = jnp.bfloat16


def _params(n_grid_dims):
    return pltpu.CompilerParams(
        dimension_semantics=("arbitrary",) * n_grid_dims,
        vmem_limit_bytes=VMEM_LIMIT_BYTES)


def _resident(block_shape, index_map):
    return pl.BlockSpec(block_shape, index_map, pipeline_mode=pl.Buffered(1))


def _silu(v):
    return v * jax.nn.sigmoid(v)


def _norm_mod(x, g, shift, scale):
    ms = jnp.mean(x * x, axis=-1, keepdims=True)
    y = x * lax.rsqrt(ms + EPS)
    return (y * g) * (1.0 + scale) + shift


def _ada_kernel(c_ref, w_ref, b_ref, o_ref):
    ca = _silu(c_ref[...])
    w = w_ref[...]
    ca_hi = ca.astype(BF16)
    ca_lo = (ca - ca_hi.astype(F32)).astype(BF16)
    w_hi = w.astype(BF16)
    w_lo = (w - w_hi.astype(F32)).astype(BF16)
    nb = ca.shape[0]
    p = jnp.dot(jnp.concatenate([ca_hi, ca_lo], axis=0), w_hi, preferred_element_type=F32)
    q = jnp.dot(ca_hi, w_lo, preferred_element_type=F32)
    o_ref[...] = (p[:nb] + p[nb:]) + q + b_ref[...]


def _ada(c, w, b, tn=2048):
    n_layers, d, n = w.shape
    bsz = c.shape[0]
    tn = min(tn, n)
    while n % tn:
        tn //= 2
    return pl.pallas_call(
        _ada_kernel,
        grid=(n_layers, n // tn),
        in_specs=[
            pl.BlockSpec((bsz, d), lambda l, j: (0, 0)),
            pl.BlockSpec((None, d, tn), lambda l, j: (l, 0, j)),
            pl.BlockSpec((None, 1, tn), lambda l, j: (l, 0, j)),
        ],
        out_specs=pl.BlockSpec((None, bsz, tn), lambda l, j: (l, 0, j)),
        out_shape=jax.ShapeDtypeStruct((n_layers, bsz, n), F32),
        compiler_params=_params(2),
        name="ada_mod",
    )(c, w, b.reshape(n_layers, 1, n))


PERM_ROWS = DFT_RADIX * DFT_RADIX


def _residue_perm():
    r = jnp.arange(PERM_ROWS, dtype=jnp.int32)
    src = (r % DFT_RADIX) * DFT_RADIX + r // DFT_RADIX
    return (src[:, None] == r[None, :]).astype(BF16)


def _store_residue_major(h, p_ref, o_ref, row0=0):
    hb = h.astype(o_ref.dtype)
    slab = o_ref.shape[3]
    for r0 in range(0, h.shape[0], PERM_ROWS):
        p = jnp.dot(p_ref[...], hb[r0:r0 + PERM_ROWS], preferred_element_type=F32).astype(o_ref.dtype)
        j0 = (row0 + r0) // DFT_RADIX
        for k in range(o_ref.shape[0]):
            for s1 in range(DFT_RADIX):
                o_ref[k, s1, j0:j0 + DFT_RADIX, :] = (
                    p[s1 * DFT_RADIX:(s1 + 1) * DFT_RADIX, k * slab:(k + 1) * slab])


def _rm_shape(bsz, s, d):
    slab = min(FOURIER_STEP_COLS, d)
    return (bsz, d // slab, DFT_RADIX, s // DFT_RADIX, slab)


def _rm_out_spec(tm, d):
    slab = min(FOURIER_STEP_COLS, d)
    return pl.BlockSpec((None, d // slab, DFT_RADIX, tm // DFT_RADIX, slab),
                        lambda b, i: (b, 0, 0, i, 0))


def _prenorm_kernel(x_ref, g_ref, sh_ref, sc_ref, *rest, residue_major):
    h = _norm_mod(x_ref[...], g_ref[...], sh_ref[...], sc_ref[...])
    if residue_major:
        p_ref, o_ref = rest
        _store_residue_major(h, p_ref, o_ref)
    else:
        o_ref, = rest
        o_ref[...] = h.astype(o_ref.dtype)


def _prenorm(x, g, shift, scale, *, residue_major, out_dtype, tm=ROW_TILE):
    bsz, s, d = x.shape
    tm = min(tm, s)
    vec = lambda a: a.reshape(bsz, 1, d)
    in_specs = [
        pl.BlockSpec((None, tm, d), lambda b, i: (b, i, 0)),
        pl.BlockSpec((1, d), lambda b, i: (0, 0)),
        pl.BlockSpec((None, 1, d), lambda b, i: (b, 0, 0)),
        pl.BlockSpec((None, 1, d), lambda b, i: (b, 0, 0)),
    ]
    args = [x, g.reshape(1, d), vec(shift), vec(scale)]
    if residue_major:
        assert tm % PERM_ROWS == 0
        in_specs.append(_resident((PERM_ROWS, PERM_ROWS), lambda b, i: (0, 0)))
        args.append(_residue_perm())
        out_shape = jax.ShapeDtypeStruct(_rm_shape(bsz, s, d), out_dtype)
        out_spec = _rm_out_spec(tm, d)
    else:
        out_shape = jax.ShapeDtypeStruct((bsz, s, d), out_dtype)
        out_spec = pl.BlockSpec((None, tm, d), lambda b, i: (b, i, 0))
    return pl.pallas_call(
        functools.partial(_prenorm_kernel, residue_major=residue_major),
        grid=(bsz, s // tm),
        in_specs=in_specs,
        out_specs=out_spec,
        out_shape=out_shape,
        compiler_params=_params(2),
        name="prenorm_rm" if residue_major else "prenorm",
    )(*args)


def _dft4(x):
    (x0r, x0i), (x1r, x1i), (x2r, x2i), (x3r, x3i) = x
    t0r, t0i = x0r + x2r, x0i + x2i
    t1r, t1i = x0r - x2r, x0i - x2i
    t2r, t2i = x1r + x3r, x1i + x3i
    t3r, t3i = x1r - x3r, x1i - x3i
    return [(t0r + t2r, t0i + t2i),
            (t1r + t3i, t1i - t3r),
            (t0r - t2r, t0i - t2i),
            (t1r - t3i, t1i + t3r)]


def _twiddle16(z, p):
    zr, zi = z
    p = p % 16
    if p == 0:
        return zr, zi
    if p == 4:
        return zi, -zr
    if p == 8:
        return -zr, -zi
    if p == 12:
        return -zi, zr
    if p % 4 == 2:
        c = math.sqrt(0.5)
        sr = -c if p in (6, 10) else c
        si = -c if p in (2, 6) else c
        add, sub = zr + zi, zr - zi
        real = add * sr if sr == -si else sub * sr
        imag = sub * si if sr == -si else add * si
        return real, imag
    cr = math.cos(2.0 * math.pi * p / 16.0)
    ci = -math.sin(2.0 * math.pi * p / 16.0)
    return zr * cr - zi * ci, zr * ci + zi * cr


DFT_SLAB = 256


def _fourier2d_kernel(h_ref, w_ref, cs_ref, f_ref, *scratch):
    n2 = h_ref.shape[1]
    td = h_ref.shape[2]
    rows = 8
    slabs = [(c0, scratch[2 * i], scratch[2 * i + 1])
             for i, c0 in enumerate(range(0, td, DFT_SLAB))]
    for c0, br_ref, bi_ref in slabs:
        for s1 in range(DFT_RADIX):
            p = jnp.dot(w_ref[s1], h_ref[s1, :, c0:c0 + DFT_SLAB], preferred_element_type=F32)
            br_ref[s1] = p[:n2]
            bi_ref[s1] = p[n2:]

    quarter = max(n2 // 4, rows)
    for c0, br_ref, bi_ref in slabs:
        for q0 in range(0, n2, quarter):
            for r in range(q0, q0 + quarter, rows):
                rs = slice(r, r + rows)
                for l0 in range(0, DFT_SLAB, LANES):
                    ls = slice(l0, l0 + LANES)
                    us = [None] * DFT_RADIX
                    for b in range(4):
                        xs = [(br_ref[4 * a + b, rs, ls], bi_ref[4 * a + b, rs, ls])
                              for a in range(4)]
                        ub = _dft4(xs)
                        for c in range(4):
                            us[4 * c + b] = _twiddle16(ub[c], b * c)
                    for c in range(4):
                        ys = _dft4(us[4 * c:4 * c + 4])
                        for d in range(4):
                            br_ref[c + 4 * d, rs, ls] = ys[d][0]
                            bi_ref[c + 4 * d, rs, ls] = ys[d][1]
            qs = slice(q0, q0 + quarter)
            yq = jnp.concatenate(
                [br_ref[:, qs, :].reshape(DFT_RADIX * quarter, DFT_SLAB).astype(BF16),
                 bi_ref[:, qs, :].reshape(DFT_RADIX * quarter, DFT_SLAB).astype(BF16)], axis=1)
            fq = jnp.dot(yq, cs_ref[...], preferred_element_type=F32).astype(f_ref.dtype)
            for k1 in range(DFT_RADIX):
                f_ref[k1 * n2 + q0:k1 * n2 + q0 + quarter, c0:c0 + DFT_SLAB] = (
                    fq[k1 * quarter:(k1 + 1) * quarter])


def _seqdft_tables(s):
    n2 = s // DFT_RADIX
    k2 = jnp.arange(n2, dtype=jnp.int32)
    s1 = jnp.arange(DFT_RADIX, dtype=jnp.int32)
    unit = 2.0 * jnp.pi / s
    a = ((k2[:, None] * (DFT_RADIX * k2[None, :])) % s).astype(F32) * unit
    b = ((s1[:, None] * k2[None, :]) % s).astype(F32) * unit
    ca, sa = jnp.cos(a)[None], jnp.sin(a)[None]
    cb, sb = jnp.cos(b)[:, :, None], jnp.sin(b)[:, :, None]
    wr = ca * cb - sa * sb
    wi = -(sa * cb + ca * sb)
    return jnp.concatenate([wr, wi], axis=1).astype(BF16)


def _chan_dft_table(gd, s):
    c = jnp.arange(gd, dtype=jnp.int32)
    j = (c[:, None] * c[None, :]) % gd
    ang = j.astype(F32) * (2.0 * jnp.pi / gd)
    scale = 1.0 / jnp.sqrt(jnp.asarray(s * gd, F32))
    eye = jnp.eye(DFT_SLAB // gd, dtype=F32)
    blocks = [jnp.kron(eye, t * scale) for t in (jnp.cos(ang), jnp.sin(ang))]
    return jnp.concatenate(blocks, axis=0).astype(BF16)


def _fourier2d(h_rm, w_tab, cs_tab):
    bsz, n_steps, _, n2, td = h_rm.shape
    s = DFT_RADIX * n2
    slab_scratch = [pltpu.VMEM((DFT_RADIX, n2, DFT_SLAB), F32),
                    pltpu.VMEM((DFT_RADIX, n2, DFT_SLAB), F32)]
    return pl.pallas_call(
        _fourier2d_kernel,
        grid=(bsz, n_steps),
        in_specs=[
            pl.BlockSpec((None, None, DFT_RADIX, n2, td), lambda b, j: (b, j, 0, 0, 0)),
            _resident((DFT_RADIX, 2 * n2, n2), lambda b, j: (0, 0, 0)),
            _resident((2 * DFT_SLAB, DFT_SLAB), lambda b, j: (0, 0)),
        ],
        out_specs=pl.BlockSpec((None, None, s, td), lambda b, j: (b, j, 0, 0)),
        out_shape=jax.ShapeDtypeStruct((bsz, n_steps, s, td), BF16),
        scratch_shapes=slab_scratch * (td // DFT_SLAB),
        compiler_params=_params(2),
        name="fourier2d",
    )(h_rm, w_tab, cs_tab)


def _retproj_kernel(h_ref, w_ref, cos_ref, sin_ref,
                    q_ref, kt_ref, v_ref, sg_ref, wkt_ref, *, dk, chunk):
    h = h_ref[...]
    half = dk // 2
    tm = h.shape[0]
    hdk = RET_HEADS * dk
    dv = v_ref.shape[2]
    dvt = RET_HEADS * dv

    @pl.when((pl.program_id(0) == 0) & (pl.program_id(1) == 0))
    def _():
        for c0 in range(0, hdk, dk):
            wk = w_ref[:, hdk + c0:hdk + c0 + dk].astype(F32)
            wkt_ref[c0:c0 + dk, :] = wk.T.astype(wkt_ref.dtype)

    cos, sin = cos_ref[...], sin_ref[...]
    cost, sint = cos.T, sin.T
    k_scale = dk ** -0.5
    for hd in range(RET_HEADS):
        c0 = hd * dk
        q = jnp.dot(h, w_ref[:, c0:c0 + dk], preferred_element_type=F32)
        q1, q2 = q[:, :half], q[:, half:]
        q_ref[hd, :, :half] = (q1 * cos - q2 * sin).astype(q_ref.dtype)
        q_ref[hd, :, half:] = (q1 * sin + q2 * cos).astype(q_ref.dtype)
        kt = lax.dot_general(wkt_ref[c0:c0 + dk, :], h, (((1,), (1,)), ((), ())),
                             preferred_element_type=F32)
        k1, k2 = kt[:half], kt[half:]
        r1 = (k1 * cost - k2 * sint) * k_scale
        r2 = (k1 * sint + k2 * cost) * k_scale
        for j in range(tm // chunk):
            kt_ref[j, c0:c0 + half, :] = r1[:, j * chunk:(j + 1) * chunk].astype(kt_ref.dtype)
            kt_ref[j, c0 + half:c0 + dk, :] = r2[:, j * chunk:(j + 1) * chunk].astype(kt_ref.dtype)
    step = min(256, dv)
    v0, g0 = 2 * hdk, 2 * hdk + dvt
    for hd in range(RET_HEADS):
        for o in range(0, dv, step):
            c0 = hd * dv + o
            v_ref[hd, :, o:o + step] = jnp.dot(h, w_ref[:, v0 + c0:v0 + c0 + step],
                                               preferred_element_type=F32).astype(v_ref.dtype)
            g = jnp.dot(h, w_ref[:, g0 + c0:g0 + c0 + step], preferred_element_type=F32)
            sg_ref[hd, :, o:o + step] = _silu(g).astype(sg_ref.dtype)


def _retproj(h, w, cos, sin, chunk, tm=ROW_TILE):
    bsz, s, d = h.shape
    hdk = d
    hdv = (w.shape[1] - 2 * hdk) // 2
    dk = hdk // RET_HEADS
    dv = hdv // RET_HEADS
    half = dk // 2
    tm = min(tm, s)
    row = lambda b, i: (b, i, 0)
    head_row = lambda b, i: (b, 0, i, 0)
    return pl.pallas_call(
        functools.partial(_retproj_kernel, dk=dk, chunk=chunk),
        grid=(bsz, s // tm),
        in_specs=[
            pl.BlockSpec((None, tm, d), row),
            _resident(w.shape, lambda b, i: (0, 0)),
            pl.BlockSpec((tm, half), lambda b, i: (i, 0)),
            pl.BlockSpec((tm, half), lambda b, i: (i, 0)),
        ],
        out_specs=[
            pl.BlockSpec((None, RET_HEADS, tm, dk), head_row),
            pl.BlockSpec((None, tm // chunk, hdk, chunk), lambda b, i: (b, i, 0, 0)),
            pl.BlockSpec((None, RET_HEADS, tm, dv), head_row),
            pl.BlockSpec((None, RET_HEADS, tm, dv), head_row),
        ],
        out_shape=[
            jax.ShapeDtypeStruct((bsz, RET_HEADS, s, dk), BF16),
            jax.ShapeDtypeStruct((bsz, s // chunk, hdk, chunk), BF16),
            jax.ShapeDtypeStruct((bsz, RET_HEADS, s, dv), BF16),
            jax.ShapeDtypeStruct((bsz, RET_HEADS, s, dv), BF16),
        ],
        scratch_shapes=[pltpu.VMEM((hdk, d), BF16)],
        compiler_params=_params(2),
        name="ret_proj",
    )(h, w, cos, sin)


def _retcore_kernel(cd_ref, q_ref, kt_ref, v_ref, dm_ref, qdf_ref, qdb_ref, kdf_ref, kdb_ref,
                    y_ref, r_all_ref, rf_ref, rb_ref):
    hd = pl.program_id(1)
    nc, dk, c = kt_ref.shape
    cd_f = cd_ref[0, hd]
    cd_b = cd_ref[1, hd]
    kdf = kdf_ref[...]
    kdb = kdb_ref[...]

    rf_ref[...] = jnp.zeros_like(rf_ref)
    rb_ref[...] = jnp.zeros_like(rb_ref)
    for it in range(nc):
        nf, nb = it, nc - 1 - it
        r_all_ref[nf, :dk, :] = rf_ref[...].astype(r_all_ref.dtype)
        r_all_ref[nb, dk:, :] = rb_ref[...].astype(r_all_ref.dtype)
        if it == nc - 1:
            break
        ktf = (kt_ref[nf].astype(F32) * kdf).astype(BF16)
        ktb = (kt_ref[nb].astype(F32) * kdb).astype(BF16)
        rf_ref[...] = rf_ref[...] * cd_f + jnp.dot(
            ktf, v_ref[nf * c:(nf + 1) * c, :], preferred_element_type=F32)
        rb_ref[...] = rb_ref[...] * cd_b + jnp.dot(
            ktb, v_ref[nb * c:(nb + 1) * c, :], preferred_element_type=F32)

    reps = dk // LANES
    qdf = jnp.tile(qdf_ref[...], (1, reps))
    qdb = jnp.tile(qdb_ref[...], (1, reps))
    for n in range(nc):
        rows = slice(n * c, (n + 1) * c)
        qn = q_ref[rows, :]
        sc = jnp.dot(qn, kt_ref[n], preferred_element_type=F32) * dm_ref[...]
        y = jnp.dot(sc.astype(BF16), v_ref[rows, :], preferred_element_type=F32)
        qf32 = qn.astype(F32)
        qfb = jnp.concatenate([(qf32 * qdf).astype(BF16), (qf32 * qdb).astype(BF16)], axis=1)
        y += jnp.dot(qfb, r_all_ref[n], preferred_element_type=F32)
        mu = jnp.mean(y, axis=-1, keepdims=True)
        yc = y - mu
        var = jnp.mean(yc * yc, axis=-1, keepdims=True)
        y_ref[rows, :] = (yc * lax.rsqrt(var + EPS)).astype(y_ref.dtype)


def _ret_decay_tables(c):
    hidx = jnp.arange(RET_HEADS, dtype=F32)
    lg_f = jnp.log1p(-jnp.exp2(-5.0 - hidx))
    lg_b = jnp.flip(lg_f)
    t = jnp.arange(c, dtype=jnp.int32)
    diff = (t[:, None] - t[None, :])
    fmask = diff >= 0
    bmask = diff < 0
    ef = jnp.exp(jnp.where(fmask, diff, 0).astype(F32)[None] * lg_f[:, None, None])
    eb = jnp.exp(jnp.where(bmask, -diff, 0).astype(F32)[None] * lg_b[:, None, None])
    dm = jnp.where(fmask[None], ef, eb)
    tf = t.astype(F32)
    ones = jnp.ones((1, 1, LANES), F32)
    qdf = jnp.exp((tf + 1.0)[None, :] * lg_f[:, None])[:, :, None] * ones
    qdb = jnp.exp((c - tf)[None, :] * lg_b[:, None])[:, :, None] * ones
    kdf = jnp.exp((c - 1.0 - tf)[None, :] * lg_f[:, None])[:, None, :]
    kdb = jnp.exp(tf[None, :] * lg_b[:, None])[:, None, :]
    cd = jnp.stack([jnp.exp(c * lg_f), jnp.exp(c * lg_b)])
    return cd, dm, qdf, qdb, kdf, kdb


def _retcore(q, kt, v, tables):
    bsz, _, s, dk = q.shape
    dv = v.shape[3]
    nc, c = kt.shape[1], kt.shape[3]
    cd, dm, qdf, qdb, kdf, kdb = tables
    per_head = lambda b, h: (h, 0, 0)
    head_block = lambda b, h: (b, h, 0, 0)
    return pl.pallas_call(
        _retcore_kernel,
        grid=(bsz, RET_HEADS),
        in_specs=[
            pl.BlockSpec(memory_space=pltpu.SMEM),
            pl.BlockSpec((None, None, s, dk), head_block),
            pl.BlockSpec((None, nc, dk, c), lambda b, h: (b, 0, h, 0)),
            pl.BlockSpec((None, None, s, dv), head_block),
            pl.BlockSpec((None, c, c), per_head),
            pl.BlockSpec((None, c, LANES), per_head),
            pl.BlockSpec((None, c, LANES), per_head),
            pl.BlockSpec((None, 1, c), per_head),
            pl.BlockSpec((None, 1, c), per_head),
        ],
        out_specs=pl.BlockSpec((None, None, s, dv), head_block),
        out_shape=jax.ShapeDtypeStruct((bsz, RET_HEADS, s, dv), BF16),
        scratch_shapes=[pltpu.VMEM((nc, 2 * dk, dv), BF16),
                        pltpu.VMEM((dk, dv), F32),
                        pltpu.VMEM((dk, dv), F32)],
        compiler_params=_params(2),
        name="ret_core",
    )(cd, q, kt, v, dm, qdf, qdb, kdf, kdb)


def _mix_ffn_kernel(*refs, fourier, emit, cast_reps):
    n_cast = len(cast_reps)
    n_out = (1 if emit == "final" else 2) + n_cast
    ins = refs[:len(refs) - n_out - 2]
    outs = refs[len(ins):len(ins) + n_out]
    h2_ref, a_ref = refs[len(ins) + n_out:]
    o_ref = outs[0]
    flat_step = pl.program_id(0) * pl.num_programs(1) + pl.program_id(1)
    for src_ref, dst_ref, rep in zip(ins[len(ins) - n_cast:], outs[len(outs) - n_cast:], cast_reps):
        if rep == 1:
            dst_ref[...] = src_ref[...].astype(dst_ref.dtype)
        else:
            @pl.when(flat_step % rep == 0)
            def _(src_ref=src_ref, dst_ref=dst_ref):
                dst_ref[...] = src_ref[...].astype(dst_ref.dtype)
    ins = ins[:len(ins) - n_cast]
    if emit == "x+h_rm":
        ins, perm_ref = ins[:-1], ins[-1]
    x_ref, vec_ref, gain_ref = ins[:3]
    g1, sh2, sc2, g2 = (vec_ref[r:r + 1, :] for r in range(4))
    if fourier:
        f_ref, wo_ref, win_ref, wout_ref = ins[3:]
    else:
        y_ref, sg_ref, wo_ref, win_ref, wout_ref = ins[3:]
    f = wout_ref.shape[0]
    tm = x_ref.shape[0]
    sub = tm // MIX_SUBTILES
    subtiles = [slice(r0, r0 + sub) for r0 in range(0, tm, sub)]
    for rs in subtiles:
        if fourier:
            z = jnp.concatenate([f_ref[k, rs, :] for k in range(f_ref.shape[0])], axis=1)
        else:
            z = jnp.concatenate(
                [sg_ref[k, rs, :] * y_ref[k, rs, :] for k in range(y_ref.shape[0])], axis=1)
        m = jnp.dot(z, wo_ref[...], preferred_element_type=F32)
        x1 = x_ref[rs, :] + g1 * m
        h2_ref[rs, :] = _norm_mod(x1, gain_ref[0:1, :], sh2, sc2).astype(h2_ref.dtype)
        o_ref[rs, :] = x1
    for rs in subtiles:
        for c0 in range(0, f, FFN_CHUNK):
            gate = jnp.dot(h2_ref[rs, :], win_ref[:, c0:c0 + FFN_CHUNK], preferred_element_type=F32)
            up = jnp.dot(h2_ref[rs, :], win_ref[:, f + c0:f + c0 + FFN_CHUNK],
                         preferred_element_type=F32)
            a_ref[rs, c0:c0 + FFN_CHUNK] = (_silu(gate) * up).astype(a_ref.dtype)
    ffns = [jnp.dot(a_ref[rs, :], wout_ref[...], preferred_element_type=F32) for rs in subtiles]
    for rs, ffn in zip(subtiles, ffns):
        x2 = o_ref[rs, :] + g2 * ffn
        hn = _norm_mod(x2, gain_ref[1:2, :], vec_ref[4:5, :], vec_ref[5:6, :])
        if emit == "final":
            o_ref[rs, :] = hn
            continue
        o_ref[rs, :] = x2
        hn_ref = outs[1]
        if emit == "x+h":
            hn_ref[rs, :] = hn.astype(hn_ref.dtype)
        else:
            _store_residue_major(hn, perm_ref, hn_ref, rs.start)


def _cast_rows(rows, n_steps):
    n = n_steps
    while n > 1 and (rows % n or (rows // n) % 16):
        n //= 2
    return rows // n


def _mix_ffn(x, vecs, gains, mix_inputs, wo, win, wout, *, fourier, emit, cast_jobs=(),
             tm=ROW_TILE):
    bsz, s, d = x.shape
    tm = min(tm, s)
    n_tiles = s // tm
    row = lambda b, i: (b, i, 0)
    in_specs = [pl.BlockSpec((None, tm, d), row),
                pl.BlockSpec((None, 8, d), lambda b, i: (b, 0, 0)),
                pl.BlockSpec((8, d), lambda b, i: (0, 0))]
    args = [x, vecs, gains]
    for a in mix_inputs:
        in_specs.append(pl.BlockSpec((None, a.shape[1], tm, a.shape[3]), lambda b, i: (b, 0, i, 0)))
        args.append(a)
    for a in (wo, win, wout):
        in_specs.append(_resident(a.shape, lambda b, i: (0, 0)))
        args.append(a)
    out_specs = [pl.BlockSpec((None, tm, d), row)]
    out_shape = [jax.ShapeDtypeStruct((bsz, s, d), F32)]
    scratch = [pltpu.VMEM((tm, d), BF16), pltpu.VMEM((tm, wout.shape[0]), BF16)]
    if emit == "x+h":
        out_specs.append(pl.BlockSpec((None, tm, d), row))
        out_shape.append(jax.ShapeDtypeStruct((bsz, s, d), BF16))
    elif emit == "x+h_rm":
        out_specs.append(_rm_out_spec(tm, d))
        out_shape.append(jax.ShapeDtypeStruct(_rm_shape(bsz, s, d), BF16))
        assert tm % PERM_ROWS == 0
        in_specs.append(_resident((PERM_ROWS, PERM_ROWS), lambda b, i: (0, 0)))
        args.append(_residue_perm())
    cast_reps = []
    for stack, layer in cast_jobs:
        _, rows, cols = stack.shape
        rp = _cast_rows(rows, bsz * n_tiles)
        rep = bsz * n_tiles // (rows // rp)
        cast_reps.append(rep)
        in_specs.append(pl.BlockSpec(
            (None, rp, cols), lambda b, i, layer=layer, rep=rep: (layer, (b * n_tiles + i) // rep, 0)))
        args.append(stack)
        out_specs.append(pl.BlockSpec((rp, cols), lambda b, i, rep=rep: ((b * n_tiles + i) // rep, 0)))
        out_shape.append(jax.ShapeDtypeStruct((rows, cols), BF16))
    return pl.pallas_call(
        functools.partial(_mix_ffn_kernel, fourier=fourier, emit=emit, cast_reps=tuple(cast_reps)),
        grid=(bsz, s // tm),
        in_specs=in_specs,
        out_specs=out_specs,
        out_shape=out_shape,
        scratch_shapes=scratch,
        compiler_params=_params(2),
        name="mix_ffn_fourier" if fourier else "mix_ffn_ret",
    )(*args)


def _rotary_tables(s, half):
    inv_freq = ROPE_BASE ** (-jnp.arange(half, dtype=F32) / half)
    ang = jnp.arange(s, dtype=F32)[:, None] * inv_freq[None, :]
    return jnp.cos(ang), jnp.sin(ang)


def kernel(x, c, w_ada, b_ada, norm_mix_g, norm_ffn_g, w_fourier_out, w_ret_in, w_ret_out,
           w_ffn_in, w_ffn_out, final_norm_g, w_ada_final, b_ada_final):
    bsz, s, d = x.shape
    depth = w_ada.shape[0]
    gd = d // FOURIER_GROUPS
    dk = d // RET_HEADS
    chunk = min(RET_CHUNK, s)

    mods = _ada(c, w_ada, b_ada).reshape(depth, bsz, 6, d)
    mod_final = _ada(c, w_ada_final[None], b_ada_final[None]).reshape(bsz, 2, d)

    seq_tab = _seqdft_tables(s)
    cs_tab = _chan_dft_table(gd, s)
    cos, sin = _rotary_tables(s, dk // 2)
    ret_tabs = _ret_decay_tables(chunk)

    def is_fourier(i):
        return i % N_MIXERS == 0

    def mixer_weights(i):
        j = i // N_MIXERS
        return [(w_fourier_out, j)] if is_fourier(i) else [(w_ret_in, j), (w_ret_out, j)]

    w_mix = [stack[j].astype(BF16) for stack, j in mixer_weights(0)]
    w_in, w_out = w_ffn_in[0].astype(BF16), w_ffn_out[0].astype(BF16)

    h = _prenorm(x, norm_mix_g[0], mods[0, :, 0], mods[0, :, 1], residue_major=is_fourier(0),
                 out_dtype=BF16, tm=4 * ROW_TILE)
    for i in range(depth):
        last = i == depth - 1
        if last:
            next_mod, next_gain, emit = mod_final, final_norm_g, "final"
        else:
            next_mod, next_gain = mods[i + 1, :, 0:2], norm_mix_g[i + 1]
            emit = "x+h_rm" if is_fourier(i + 1) else "x+h"
        pad = jnp.zeros((bsz, 2, d), F32)
        vecs = jnp.concatenate([mods[i, :, 2:6], next_mod, pad], axis=1)
        gains = jnp.concatenate([norm_ffn_g[i][None], next_gain[None], jnp.zeros((6, d), F32)])
        jobs = () if last else tuple(
            mixer_weights(i + 1) + [(w_ffn_in, i + 1), (w_ffn_out, i + 1)])
        if is_fourier(i):
            mix_in = (_fourier2d(h, seq_tab, cs_tab),)
        else:
            q, kt, v, sg = _retproj(h, w_mix[0], cos, sin, chunk, tm=2 * ROW_TILE)
            mix_in = (_retcore(q, kt, v, ret_tabs), sg)
        outs = _mix_ffn(x, vecs, gains, mix_in, w_mix[-1], w_in, w_out,
                        fourier=is_fourier(i), emit=emit, cast_jobs=jobs)
        if last:
            return outs[0]
        x, h = outs[:2]
        *w_mix, w_in, w_out = outs[2:]
```

```python
import functools
import math

import jax
import jax.numpy as jnp
from jax import lax
from jax.experimental import pallas as pl
from jax.experimental.pallas import tpu as pltpu

EPS = 1e-6
N_MIXERS = 2
FOURIER_GROUPS = 4
RET_HEADS = 4
ROPE_BASE = 10000.0

DFT_RADIX = 16
RET_CHUNK = 256
FFN_CHUNK = 256
LANES = 128
ROW_TILE = 512
FOURIER_STEP_COLS = 512
MIX_SUBTILES = 2
VMEM_LIMIT_BYTES = 56 * 1024 * 1024

F32 = jnp.float32
BF16 = jnp.bfloat16


def _params(n_grid_dims):
    return pltpu.CompilerParams(
        dimension_semantics=("arbitrary",) * n_grid_dims,
        vmem_limit_bytes=VMEM_LIMIT_BYTES)


def _resident(block_shape, index_map):
    return pl.BlockSpec(block_shape, index_map, pipeline_mode=pl.Buffered(1))


def _silu(v):
    return v * jax.nn.sigmoid(v)


def _norm_mod(x, g, shift, scale):
    ms = jnp.mean(x * x, axis=-1, keepdims=True)
    y = x * lax.rsqrt(ms + EPS)
    return (y * g) * (1.0 + scale) + shift


def _ada_kernel(c_ref, w_ref, b_ref, o_ref):
    ca = _silu(c_ref[...])
    w = w_ref[...]
    ca_hi = ca.astype(BF16)
    ca_lo = (ca - ca_hi.astype(F32)).astype(BF16)
    w_hi = w.astype(BF16)
    w_lo = (w - w_hi.astype(F32)).astype(BF16)
    nb = ca.shape[0]
    p = jnp.dot(jnp.concatenate([ca_hi, ca_lo], axis=0), w_hi, preferred_element_type=F32)
    q = jnp.dot(ca_hi, w_lo, preferred_element_type=F32)
    o_ref[...] = (p[:nb] + p[nb:]) + q + b_ref[...]


def _ada(c, w, b, tn=2048):
    n_layers, d, n = w.shape
    bsz = c.shape[0]
    tn = min(tn, n)
    while n % tn:
        tn //= 2
    return pl.pallas_call(
        _ada_kernel,
        grid=(n_layers, n // tn),
        in_specs=[
            pl.BlockSpec((bsz, d), lambda l, j: (0, 0)),
            pl.BlockSpec((None, d, tn), lambda l, j: (l, 0, j)),
            pl.BlockSpec((None, 1, tn), lambda l, j: (l, 0, j)),
        ],
        out_specs=pl.BlockSpec((None, bsz, tn), lambda l, j: (l, 0, j)),
        out_shape=jax.ShapeDtypeStruct((n_layers, bsz, n), F32),
        compiler_params=_params(2),
        name="ada_mod",
    )(c, w, b.reshape(n_layers, 1, n))


PERM_ROWS = DFT_RADIX * DFT_RADIX


def _residue_perm():
    r = jnp.arange(PERM_ROWS, dtype=jnp.int32)
    src = (r % DFT_RADIX) * DFT_RADIX + r // DFT_RADIX
    return (src[:, None] == r[None, :]).astype(BF16)


def _store_residue_major(h, p_ref, o_ref, row0=0):
    hb = h.astype(o_ref.dtype)
    slab = o_ref.shape[3]
    for r0 in range(0, h.shape[0], PERM_ROWS):
        p = jnp.dot(p_ref[...], hb[r0:r0 + PERM_ROWS], preferred_element_type=F32).astype(o_ref.dtype)
        j0 = (row0 + r0) // DFT_RADIX
        for k in range(o_ref.shape[0]):
            for s1 in range(DFT_RADIX):
                o_ref[k, s1, j0:j0 + DFT_RADIX, :] = (
                    p[s1 * DFT_RADIX:(s1 + 1) * DFT_RADIX, k * slab:(k + 1) * slab])


def _rm_shape(bsz, s, d):
    slab = min(FOURIER_STEP_COLS, d)
    return (bsz, d // slab, DFT_RADIX, s // DFT_RADIX, slab)


def _rm_out_spec(tm, d):
    slab = min(FOURIER_STEP_COLS, d)
    return pl.BlockSpec((None, d // slab, DFT_RADIX, tm // DFT_RADIX, slab),
                        lambda b, i: (b, 0, 0, i, 0))


def _prenorm_kernel(x_ref, g_ref, sh_ref, sc_ref, *rest, residue_major):
    h = _norm_mod(x_ref[...], g_ref[...], sh_ref[...], sc_ref[...])
    if residue_major:
        p_ref, o_ref = rest
        _store_residue_major(h, p_ref, o_ref)
    else:
        o_ref, = rest
        o_ref[...] = h.astype(o_ref.dtype)


def _prenorm(x, g, shift, scale, *, residue_major, out_dtype, tm=ROW_TILE):
    bsz, s, d = x.shape
    tm = min(tm, s)
    vec = lambda a: a.reshape(bsz, 1, d)
    in_specs = [
        pl.BlockSpec((None, tm, d), lambda b, i: (b, i, 0)),
        pl.BlockSpec((1, d), lambda b, i: (0, 0)),
        pl.BlockSpec((None, 1, d), lambda b, i: (b, 0, 0)),
        pl.BlockSpec((None, 1, d), lambda b, i: (b, 0, 0)),
    ]
    args = [x, g.reshape(1, d), vec(shift), vec(scale)]
    if residue_major:
        assert tm % PERM_ROWS == 0
        in_specs.append(_resident((PERM_ROWS, PERM_ROWS), lambda b, i: (0, 0)))
        args.append(_residue_perm())
        out_shape = jax.ShapeDtypeStruct(_rm_shape(bsz, s, d), out_dtype)
        out_spec = _rm_out_spec(tm, d)
    else:
        out_shape = jax.ShapeDtypeStruct((bsz, s, d), out_dtype)
        out_spec = pl.BlockSpec((None, tm, d), lambda b, i: (b, i, 0))
    return pl.pallas_call(
        functools.partial(_prenorm_kernel, residue_major=residue_major),
        grid=(bsz, s // tm),
        in_specs=in_specs,
        out_specs=out_spec,
        out_shape=out_shape,
        compiler_params=_params(2),
        name="prenorm_rm" if residue_major else "prenorm",
    )(*args)


def _dft4(x):
    (x0r, x0i), (x1r, x1i), (x2r, x2i), (x3r, x3i) = x
    t0r, t0i = x0r + x2r, x0i + x2i
    t1r, t1i = x0r - x2r, x0i - x2i
    t2r, t2i = x1r + x3r, x1i + x3i
    t3r, t3i = x1r - x3r, x1i - x3i
    return [(t0r + t2r, t0i + t2i),
            (t1r + t3i, t1i - t3r),
            (t0r - t2r, t0i - t2i),
            (t1r - t3i, t1i + t3r)]


def _twiddle16(z, p):
    zr, zi = z
    p = p % 16
    if p == 0:
        return zr, zi
    if p == 4:
        return zi, -zr
    if p == 8:
        return -zr, -zi
    if p == 12:
        return -zi, zr
    if p % 4 == 2:
        c = math.sqrt(0.5)
        sr = -c if p in (6, 10) else c
        si = -c if p in (2, 6) else c
        add, sub = zr + zi, zr - zi
        real = add * sr if sr == -si else sub * sr
        imag = sub * si if sr == -si else add * si
        return real, imag
    cr = math.cos(2.0 * math.pi * p / 16.0)
    ci = -math.sin(2.0 * math.pi * p / 16.0)
    return zr * cr - zi * ci, zr * ci + zi * cr


DFT_SLAB = 256


def _fourier2d_kernel(h_ref, w_ref, cs_ref, f_ref, *scratch):
    n2 = h_ref.shape[1]
    td = h_ref.shape[2]
    rows = 8
    slabs = [(c0, scratch[2 * i], scratch[2 * i + 1])
             for i, c0 in enumerate(range(0, td, DFT_SLAB))]
    for c0, br_ref, bi_ref in slabs:
        for s1 in range(DFT_RADIX):
            p = jnp.dot(w_ref[s1], h_ref[s1, :, c0:c0 + DFT_SLAB], preferred_element_type=F32)
            br_ref[s1] = p[:n2]
            bi_ref[s1] = p[n2:]

    quarter = max(n2 // 4, rows)
    for c0, br_ref, bi_ref in slabs:
        for q0 in range(0, n2, quarter):
            for r in range(q0, q0 + quarter, rows):
                rs = slice(r, r + rows)
                for l0 in range(0, DFT_SLAB, LANES):
                    ls = slice(l0, l0 + LANES)
                    us = [None] * DFT_RADIX
                    for b in range(4):
                        xs = [(br_ref[4 * a + b, rs, ls], bi_ref[4 * a + b, rs, ls])
                              for a in range(4)]
                        ub = _dft4(xs)
                        for c in range(4):
                            us[4 * c + b] = _twiddle16(ub[c], b * c)
                    for c in range(4):
                        ys = _dft4(us[4 * c:4 * c + 4])
                        for d in range(4):
                            br_ref[c + 4 * d, rs, ls] = ys[d][0]
                            bi_ref[c + 4 * d, rs, ls] = ys[d][1]
            qs = slice(q0, q0 + quarter)
            yq = jnp.concatenate(
                [br_ref[:, qs, :].reshape(DFT_RADIX * quarter, DFT_SLAB).astype(BF16),
                 bi_ref[:, qs, :].reshape(DFT_RADIX * quarter, DFT_SLAB).astype(BF16)], axis=1)
            fq = jnp.dot(yq, cs_ref[...], preferred_element_type=F32).astype(f_ref.dtype)
            for k1 in range(DFT_RADIX):
                f_ref[k1 * n2 + q0:k1 * n2 + q0 + quarter, c0:c0 + DFT_SLAB] = (
                    fq[k1 * quarter:(k1 + 1) * quarter])


def _seqdft_tables(s):
    n2 = s // DFT_RADIX
    k2 = jnp.arange(n2, dtype=jnp.int32)
    s1 = jnp.arange(DFT_RADIX, dtype=jnp.int32)
    unit = 2.0 * jnp.pi / s
    a = ((k2[:, None] * (DFT_RADIX * k2[None, :])) % s).astype(F32) * unit
    b = ((s1[:, None] * k2[None, :]) % s).astype(F32) * unit
    ca, sa = jnp.cos(a)[None], jnp.sin(a)[None]
    cb, sb = jnp.cos(b)[:, :, None], jnp.sin(b)[:, :, None]
    wr = ca * cb - sa * sb
    wi = -(sa * cb + ca * sb)
    return jnp.concatenate([wr, wi], axis=1).astype(BF16)


def _chan_dft_table(gd, s):
    c = jnp.arange(gd, dtype=jnp.int32)
    j = (c[:, None] * c[None, :]) % gd
    ang = j.astype(F32) * (2.0 * jnp.pi / gd)
    scale = 1.0 / jnp.sqrt(jnp.asarray(s * gd, F32))
    eye = jnp.eye(DFT_SLAB // gd, dtype=F32)
    blocks = [jnp.kron(eye, t * scale) for t in (jnp.cos(ang), jnp.sin(ang))]
    return jnp.concatenate(blocks, axis=0).astype(BF16)


def _fourier2d(h_rm, w_tab, cs_tab):
    bsz, n_steps, _, n2, td = h_rm.shape
    s = DFT_RADIX * n2
    slab_scratch = [pltpu.VMEM((DFT_RADIX, n2, DFT_SLAB), F32),
                    pltpu.VMEM((DFT_RADIX, n2, DFT_SLAB), F32)]
    return pl.pallas_call(
        _fourier2d_kernel,
        grid=(bsz, n_steps),
        in_specs=[
            pl.BlockSpec((None, None, DFT_RADIX, n2, td), lambda b, j: (b, j, 0, 0, 0)),
            _resident((DFT_RADIX, 2 * n2, n2), lambda b, j: (0, 0, 0)),
            _resident((2 * DFT_SLAB, DFT_SLAB), lambda b, j: (0, 0)),
        ],
        out_specs=pl.BlockSpec((None, None, s, td), lambda b, j: (b, j, 0, 0)),
        out_shape=jax.ShapeDtypeStruct((bsz, n_steps, s, td), BF16),
        scratch_shapes=slab_scratch * (td // DFT_SLAB),
        compiler_params=_params(2),
        name="fourier2d",
    )(h_rm, w_tab, cs_tab)


def _retproj_kernel(h_ref, w_ref, cos_ref, sin_ref,
                    q_ref, kt_ref, v_ref, sg_ref, wkt_ref, *, dk, chunk):
    h = h_ref[...]
    half = dk // 2
    tm = h.shape[0]
    hdk = RET_HEADS * dk
    dv = v_ref.shape[2]
    dvt = RET_HEADS * dv

    @pl.when((pl.program_id(0) == 0) & (pl.program_id(1) == 0))
    def _():
        for c0 in range(0, hdk, dk):
            wk = w_ref[:, hdk + c0:hdk + c0 + dk].astype(F32)
            wkt_ref[c0:c0 + dk, :] = wk.T.astype(wkt_ref.dtype)

    cos, sin = cos_ref[...], sin_ref[...]
    cost, sint = cos.T, sin.T
    k_scale = dk ** -0.5
    for hd in range(RET_HEADS):
        c0 = hd * dk
        q = jnp.dot(h, w_ref[:, c0:c0 + dk], preferred_element_type=F32)
        q1, q2 = q[:, :half], q[:, half:]
        q_ref[hd, :, :half] = (q1 * cos - q2 * sin).astype(q_ref.dtype)
        q_ref[hd, :, half:] = (q1 * sin + q2 * cos).astype(q_ref.dtype)
        kt = lax.dot_general(wkt_ref[c0:c0 + dk, :], h, (((1,), (1,)), ((), ())),
                             preferred_element_type=F32)
        k1, k2 = kt[:half], kt[half:]
        r1 = (k1 * cost - k2 * sint) * k_scale
        r2 = (k1 * sint + k2 * cost) * k_scale
        for j in range(tm // chunk):
            kt_ref[j, c0:c0 + half, :] = r1[:, j * chunk:(j + 1) * chunk].astype(kt_ref.dtype)
            kt_ref[j, c0 + half:c0 + dk, :] = r2[:, j * chunk:(j + 1) * chunk].astype(kt_ref.dtype)
    step = min(256, dv)
    v0, g0 = 2 * hdk, 2 * hdk + dvt
    for hd in range(RET_HEADS):
        for o in range(0, dv, step):
            c0 = hd * dv + o
            v_ref[hd, :, o:o + step] = jnp.dot(h, w_ref[:, v0 + c0:v0 + c0 + step],
                                               preferred_element_type=F32).astype(v_ref.dtype)
            g = jnp.dot(h, w_ref[:, g0 + c0:g0 + c0 + step], preferred_element_type=F32)
            sg_ref[hd, :, o:o + step] = _silu(g).astype(sg_ref.dtype)


def _retproj(h, w, cos, sin, chunk, tm=ROW_TILE):
    bsz, s, d = h.shape
    hdk = d
    hdv = (w.shape[1] - 2 * hdk) // 2
    dk = hdk // RET_HEADS
    dv = hdv // RET_HEADS
    half = dk // 2
    tm = min(tm, s)
    row = lambda b, i: (b, i, 0)
    head_row = lambda b, i: (b, 0, i, 0)
    return pl.pallas_call(
        functools.partial(_retproj_kernel, dk=dk, chunk=chunk),
        grid=(bsz, s // tm),
        in_specs=[
            pl.BlockSpec((None, tm, d), row),
            _resident(w.shape, lambda b, i: (0, 0)),
            pl.BlockSpec((tm, half), lambda b, i: (i, 0)),
            pl.BlockSpec((tm, half), lambda b, i: (i, 0)),
        ],
        out_specs=[
            pl.BlockSpec((None, RET_HEADS, tm, dk), head_row),
            pl.BlockSpec((None, tm // chunk, hdk, chunk), lambda b, i: (b, i, 0, 0)),
            pl.BlockSpec((None, RET_HEADS, tm, dv), head_row),
            pl.BlockSpec((None, RET_HEADS, tm, dv), head_row),
        ],
        out_shape=[
            jax.ShapeDtypeStruct((bsz, RET_HEADS, s, dk), BF16),
            jax.ShapeDtypeStruct((bsz, s // chunk, hdk, chunk), BF16),
            jax.ShapeDtypeStruct((bsz, RET_HEADS, s, dv), BF16),
            jax.ShapeDtypeStruct((bsz, RET_HEADS, s, dv), BF16),
        ],
        scratch_shapes=[pltpu.VMEM((hdk, d), BF16)],
        compiler_params=_params(2),
        name="ret_proj",
    )(h, w, cos, sin)


def _retcore_kernel(cd_ref, q_ref, kt_ref, v_ref, dm_ref, qdf_ref, qdb_ref, kdf_ref, kdb_ref,
                    y_ref, r_all_ref, rf_ref, rb_ref):
    hd = pl.program_id(1)
    nc, dk, c = kt_ref.shape
    cd_f = cd_ref[0, hd]
    cd_b = cd_ref[1, hd]
    kdf = kdf_ref[...]
    kdb = kdb_ref[...]

    rf_ref[...] = jnp.zeros_like(rf_ref)
    rb_ref[...] = jnp.zeros_like(rb_ref)
    for it in range(nc):
        nf, nb = it, nc - 1 - it
        r_all_ref[nf, :dk, :] = rf_ref[...].astype(r_all_ref.dtype)
        r_all_ref[nb, dk:, :] = rb_ref[...].astype(r_all_ref.dtype)
        if it == nc - 1:
            break
        ktf = (kt_ref[nf].astype(F32) * kdf).astype(BF16)
        ktb = (kt_ref[nb].astype(F32) * kdb).astype(BF16)
        rf_ref[...] = rf_ref[...] * cd_f + jnp.dot(
            ktf, v_ref[nf * c:(nf + 1) * c, :], preferred_element_type=F32)
        rb_ref[...] = rb_ref[...] * cd_b + jnp.dot(
            ktb, v_ref[nb * c:(nb + 1) * c, :], preferred_element_type=F32)

    reps = dk // LANES
    qdf = jnp.tile(qdf_ref[...], (1, reps))
    qdb = jnp.tile(qdb_ref[...], (1, reps))
    for n in range(nc):
        rows = slice(n * c, (n + 1) * c)
        qn = q_ref[rows, :]
        sc = jnp.dot(qn, kt_ref[n], preferred_element_type=F32) * dm_ref[...]
        y = jnp.dot(sc.astype(BF16), v_ref[rows, :], preferred_element_type=F32)
        qf32 = qn.astype(F32)
        qfb = jnp.concatenate([(qf32 * qdf).astype(BF16), (qf32 * qdb).astype(BF16)], axis=1)
        y += jnp.dot(qfb, r_all_ref[n], preferred_element_type=F32)
        mu = jnp.mean(y, axis=-1, keepdims=True)
        yc = y - mu
        var = jnp.mean(yc * yc, axis=-1, keepdims=True)
        y_ref[rows, :] = (yc * lax.rsqrt(var + EPS)).astype(y_ref.dtype)


def _ret_decay_tables(c):
    hidx = jnp.arange(RET_HEADS, dtype=F32)
    lg_f = jnp.log1p(-jnp.exp2(-5.0 - hidx))
    lg_b = jnp.flip(lg_f)
    t = jnp.arange(c, dtype=jnp.int32)
    diff = (t[:, None] - t[None, :])
    fmask = diff >= 0
    bmask = diff < 0
    ef = jnp.exp(jnp.where(fmask, diff, 0).astype(F32)[None] * lg_f[:, None, None])
    eb = jnp.exp(jnp.where(bmask, -diff, 0).astype(F32)[None] * lg_b[:, None, None])
    dm = jnp.where(fmask[None], ef, eb)
    tf = t.astype(F32)
    ones = jnp.ones((1, 1, LANES), F32)
    qdf = jnp.exp((tf + 1.0)[None, :] * lg_f[:, None])[:, :, None] * ones
    qdb = jnp.exp((c - tf)[None, :] * lg_b[:, None])[:, :, None] * ones
    kdf = jnp.exp((c - 1.0 - tf)[None, :] * lg_f[:, None])[:, None, :]
    kdb = jnp.exp(tf[None, :] * lg_b[:, None])[:, None, :]
    cd = jnp.stack([jnp.exp(c * lg_f), jnp.exp(c * lg_b)])
    return cd, dm, qdf, qdb, kdf, kdb


def _retcore(q, kt, v, tables):
    bsz, _, s, dk = q.shape
    dv = v.shape[3]
    nc, c = kt.shape[1], kt.shape[3]
    cd, dm, qdf, qdb, kdf, kdb = tables
    per_head = lambda b, h: (h, 0, 0)
    head_block = lambda b, h: (b, h, 0, 0)
    return pl.pallas_call(
        _retcore_kernel,
        grid=(bsz, RET_HEADS),
        in_specs=[
            pl.BlockSpec(memory_space=pltpu.SMEM),
            pl.BlockSpec((None, None, s, dk), head_block),
            pl.BlockSpec((None, nc, dk, c), lambda b, h: (b, 0, h, 0)),
            pl.BlockSpec((None, None, s, dv), head_block),
            pl.BlockSpec((None, c, c), per_head),
            pl.BlockSpec((None, c, LANES), per_head),
            pl.BlockSpec((None, c, LANES), per_head),
            pl.BlockSpec((None, 1, c), per_head),
            pl.BlockSpec((None, 1, c), per_head),
        ],
        out_specs=pl.BlockSpec((None, None, s, dv), head_block),
        out_shape=jax.ShapeDtypeStruct((bsz, RET_HEADS, s, dv), BF16),
        scratch_shapes=[pltpu.VMEM((nc, 2 * dk, dv), BF16),
                        pltpu.VMEM((dk, dv), F32),
                        pltpu.VMEM((dk, dv), F32)],
        compiler_params=_params(2),
        name="ret_core",
    )(cd, q, kt, v, dm, qdf, qdb, kdf, kdb)


def _mix_ffn_kernel(*refs, fourier, emit, cast_reps):
    n_cast = len(cast_reps)
    n_out = (1 if emit == "final" else 2) + n_cast
    ins = refs[:len(refs) - n_out - 2]
    outs = refs[len(ins):len(ins) + n_out]
    h2_ref, a_ref = refs[len(ins) + n_out:]
    o_ref = outs[0]
    flat_step = pl.program_id(0) * pl.num_programs(1) + pl.program_id(1)
    for src_ref, dst_ref, rep in zip(ins[len(ins) - n_cast:], outs[len(outs) - n_cast:], cast_reps):
        if rep == 1:
            dst_ref[...] = src_ref[...].astype(dst_ref.dtype)
        else:
            @pl.when(flat_step % rep == 0)
            def _(src_ref=src_ref, dst_ref=dst_ref):
                dst_ref[...] = src_ref[...].astype(dst_ref.dtype)
    ins = ins[:len(ins) - n_cast]
    if emit == "x+h_rm":
        ins, perm_ref = ins[:-1], ins[-1]
    x_ref, vec_ref, gain_ref = ins[:3]
    g1, sh2, sc2, g2 = (vec_ref[r:r + 1, :] for r in range(4))
    if fourier:
        f_ref, wo_ref, win_ref, wout_ref = ins[3:]
    else:
        y_ref, sg_ref, wo_ref, win_ref, wout_ref = ins[3:]
    f = wout_ref.shape[0]
    tm = x_ref.shape[0]
    sub = tm // MIX_SUBTILES
    subtiles = [slice(r0, r0 + sub) for r0 in range(0, tm, sub)]
    ms = []
    for rs in subtiles:
        if fourier:
            z = jnp.concatenate([f_ref[k, rs, :] for k in range(f_ref.shape[0])], axis=1)
        else:
            z = jnp.concatenate(
                [sg_ref[k, rs, :] * y_ref[k, rs, :] for k in range(y_ref.shape[0])], axis=1)
        ms.append(jnp.dot(z, wo_ref[...], preferred_element_type=F32))
    for rs, m in zip(subtiles, ms):
        x1 = x_ref[rs, :] + g1 * m
        h2_ref[rs, :] = _norm_mod(x1, gain_ref[0:1, :], sh2, sc2).astype(h2_ref.dtype)
        o_ref[rs, :] = x1
    for c0 in range(0, f, FFN_CHUNK):
        gus = [(jnp.dot(h2_ref[rs, :], win_ref[:, c0:c0 + FFN_CHUNK], preferred_element_type=F32),
                jnp.dot(h2_ref[rs, :], win_ref[:, f + c0:f + c0 + FFN_CHUNK],
                        preferred_element_type=F32)) for rs in subtiles]
        for rs, (gate, up) in zip(subtiles, gus):
            a_ref[rs, c0:c0 + FFN_CHUNK] = (_silu(gate) * up).astype(a_ref.dtype)
    ffns = [jnp.dot(a_ref[rs, :], wout_ref[...], preferred_element_type=F32) for rs in subtiles]
    for rs, ffn in zip(subtiles, ffns):
        x2 = o_ref[rs, :] + g2 * ffn
        hn = _norm_mod(x2, gain_ref[1:2, :], vec_ref[4:5, :], vec_ref[5:6, :])
        if emit == "final":
            o_ref[rs, :] = hn
            continue
        o_ref[rs, :] = x2
        hn_ref = outs[1]
        if emit == "x+h":
            hn_ref[rs, :] = hn.astype(hn_ref.dtype)
        else:
            _store_residue_major(hn, perm_ref, hn_ref, rs.start)


def _cast_rows(rows, n_steps):
    n = n_steps
    while n > 1 and (rows % n or (rows // n) % 16):
        n //= 2
    return rows // n


def _mix_ffn(x, vecs, gains, mix_inputs, wo, win, wout, *, fourier, emit, cast_jobs=(),
             tm=ROW_TILE):
    bsz, s, d = x.shape
    tm = min(tm, s)
    n_tiles = s // tm
    row = lambda b, i: (b, i, 0)
    in_specs = [pl.BlockSpec((None, tm, d), row),
                pl.BlockSpec((None, 8, d), lambda b, i: (b, 0, 0)),
                pl.BlockSpec((8, d), lambda b, i: (0, 0))]
    args = [x, vecs, gains]
    for a in mix_inputs:
        in_specs.append(pl.BlockSpec((None, a.shape[1], tm, a.shape[3]), lambda b, i: (b, 0, i, 0)))
        args.append(a)
    for a in (wo, win, wout):
        in_specs.append(_resident(a.shape, lambda b, i: (0, 0)))
        args.append(a)
    out_specs = [pl.BlockSpec((None, tm, d), row)]
    out_shape = [jax.ShapeDtypeStruct((bsz, s, d), F32)]
    scratch = [pltpu.VMEM((tm, d), BF16), pltpu.VMEM((tm, wout.shape[0]), BF16)]
    if emit == "x+h":
        out_specs.append(pl.BlockSpec((None, tm, d), row))
        out_shape.append(jax.ShapeDtypeStruct((bsz, s, d), BF16))
    elif emit == "x+h_rm":
        out_specs.append(_rm_out_spec(tm, d))
        out_shape.append(jax.ShapeDtypeStruct(_rm_shape(bsz, s, d), BF16))
        assert tm % PERM_ROWS == 0
        in_specs.append(_resident((PERM_ROWS, PERM_ROWS), lambda b, i: (0, 0)))
        args.append(_residue_perm())
    cast_reps = []
    for stack, layer in cast_jobs:
        _, rows, cols = stack.shape
        rp = _cast_rows(rows, bsz * n_tiles)
        rep = bsz * n_tiles // (rows // rp)
        cast_reps.append(rep)
        in_specs.append(pl.BlockSpec(
            (None, rp, cols), lambda b, i, layer=layer, rep=rep: (layer, (b * n_tiles + i) // rep, 0)))
        args.append(stack)
        out_specs.append(pl.BlockSpec((rp, cols), lambda b, i, rep=rep: ((b * n_tiles + i) // rep, 0)))
        out_shape.append(jax.ShapeDtypeStruct((rows, cols), BF16))
    return pl.pallas_call(
        functools.partial(_mix_ffn_kernel, fourier=fourier, emit=emit, cast_reps=tuple(cast_reps)),
        grid=(bsz, s // tm),
        in_specs=in_specs,
        out_specs=out_specs,
        out_shape=out_shape,
        scratch_shapes=scratch,
        compiler_params=_params(2),
        name="mix_ffn_fourier" if fourier else "mix_ffn_ret",
    )(*args)


def _rotary_tables(s, half):
    inv_freq = ROPE_BASE ** (-jnp.arange(half, dtype=F32) / half)
    ang = jnp.arange(s, dtype=F32)[:, None] * inv_freq[None, :]
    return jnp.cos(ang), jnp.sin(ang)


def kernel(x, c, w_ada, b_ada, norm_mix_g, norm_ffn_g, w_fourier_out, w_ret_in, w_ret_out,
           w_ffn_in, w_ffn_out, final_norm_g, w_ada_final, b_ada_final):
    bsz, s, d = x.shape
    depth = w_ada.shape[0]
    gd = d // FOURIER_GROUPS
    dk = d // RET_HEADS
    chunk = min(RET_CHUNK, s)

    mods = _ada(c, w_ada, b_ada).reshape(depth, bsz, 6, d)
    mod_final = _ada(c, w_ada_final[None], b_ada_final[None]).reshape(bsz, 2, d)

    seq_tab = _seqdft_tables(s)
    cs_tab = _chan_dft_table(gd, s)
    cos, sin = _rotary_tables(s, dk // 2)
    ret_tabs = _ret_decay_tables(chunk)

    def is_fourier(i):
        return i % N_MIXERS == 0

    def mixer_weights(i):
        j = i // N_MIXERS
        return [(w_fourier_out, j)] if is_fourier(i) else [(w_ret_in, j), (w_ret_out, j)]

    w_mix = [stack[j].astype(BF16) for stack, j in mixer_weights(0)]
    w_in, w_out = w_ffn_in[0].astype(BF16), w_ffn_out[0].astype(BF16)

    h = _prenorm(x, norm_mix_g[0], mods[0, :, 0], mods[0, :, 1], residue_major=is_fourier(0),
                 out_dtype=BF16, tm=4 * ROW_TILE)
    for i in range(depth):
        last = i == depth - 1
        if last:
            next_mod, next_gain, emit = mod_final, final_norm_g, "final"
        else:
            next_mod, next_gain = mods[i + 1, :, 0:2], norm_mix_g[i + 1]
            emit = "x+h_rm" if is_fourier(i + 1) else "x+h"
        pad = jnp.zeros((bsz, 2, d), F32)
        vecs = jnp.concatenate([mods[i, :, 2:6], next_mod, pad], axis=1)
        gains = jnp.concatenate([norm_ffn_g[i][None], next_gain[None], jnp.zeros((6, d), F32)])
        jobs = () if last else tuple(
            mixer_weights(i + 1) + [(w_ffn_in, i + 1), (w_ffn_out, i + 1)])
        if is_fourier(i):
            mix_in = (_fourier2d(h, seq_tab, cs_tab),)
        else:
            q, kt, v, sg = _retproj(h, w_mix[0], cos, sin, chunk, tm=2 * ROW_TILE)
            mix_in = (_retcore(q, kt, v, ret_tabs), sg)
        outs = _mix_ffn(x, vecs, gains, mix_in, w_mix[-1], w_in, w_out,
                        fourier=is_fourier(i), emit=emit, cast_jobs=jobs)
        if last:
            return outs[0]
        x, h = outs[:2]
        *w_mix, w_in, w_out = outs[2:]
```

```python
import functools
import math

import jax
import jax.numpy as jnp
from jax import lax
from jax.experimental import pallas as pl
from jax.experimental.pallas import tpu as pltpu

EPS = 1e-6
N_MIXERS = 2
FOURIER_GROUPS = 4
RET_HEADS = 4
ROPE_BASE = 10000.0

DFT_RADIX = 16
RET_CHUNK = 256
FFN_CHUNK = 256
LANES = 128
ROW_TILE = 512
FOURIER_STEP_COLS = 512
MIX_SUBTILES = 2
VMEM_LIMIT_BYTES = 56 * 1024 * 1024

F32 = jnp.float32
BF16 = jnp.bfloat16


def _params(n_grid_dims):
    return pltpu.CompilerParams(
        dimension_semantics=("arbitrary",) * n_grid_dims,
        vmem_limit_bytes=VMEM_LIMIT_BYTES)


def _resident(block_shape, index_map):
    return pl.BlockSpec(block_shape, index_map, pipeline_mode=pl.Buffered(1))


def _silu(v):
    return v * jax.nn.sigmoid(v)


def _norm_mod(x, g, shift, scale):
    ms = jnp.mean(x * x, axis=-1, keepdims=True)
    y = x * lax.rsqrt(ms + EPS)
    return (y * g) * (1.0 + scale) + shift


def _ada_kernel(c_ref, w_ref, b_ref, o_ref):
    ca = _silu(c_ref[...])
    w = w_ref[...]
    ca_hi = ca.astype(BF16)
    ca_lo = (ca - ca_hi.astype(F32)).astype(BF16)
    w_hi = w.astype(BF16)
    w_lo = (w - w_hi.astype(F32)).astype(BF16)
    nb = ca.shape[0]
    p = jnp.dot(jnp.concatenate([ca_hi, ca_lo], axis=0), w_hi, preferred_element_type=F32)
    q = jnp.dot(ca_hi, w_lo, preferred_element_type=F32)
    o_ref[...] = (p[:nb] + p[nb:]) + q + b_ref[...]


def _ada(c, w, b, tn=2048):
    n_layers, d, n = w.shape
    bsz = c.shape[0]
    tn = min(tn, n)
    while n % tn:
        tn //= 2
    return pl.pallas_call(
        _ada_kernel,
        grid=(n_layers, n // tn),
        in_specs=[
            pl.BlockSpec((bsz, d), lambda l, j: (0, 0)),
            pl.BlockSpec((None, d, tn), lambda l, j: (l, 0, j)),
            pl.BlockSpec((None, 1, tn), lambda l, j: (l, 0, j)),
        ],
        out_specs=pl.BlockSpec((None, bsz, tn), lambda l, j: (l, 0, j)),
        out_shape=jax.ShapeDtypeStruct((n_layers, bsz, n), F32),
        compiler_params=_params(2),
        name="ada_mod",
    )(c, w, b.reshape(n_layers, 1, n))


PERM_ROWS = DFT_RADIX * DFT_RADIX


def _residue_perm():
    r = jnp.arange(PERM_ROWS, dtype=jnp.int32)
    src = (r % DFT_RADIX) * DFT_RADIX + r // DFT_RADIX
    return (src[:, None] == r[None, :]).astype(BF16)


def _store_residue_major(h, p_ref, o_ref, row0=0):
    hb = h.astype(o_ref.dtype)
    slab = o_ref.shape[3]
    for r0 in range(0, h.shape[0], PERM_ROWS):
        p = jnp.dot(p_ref[...], hb[r0:r0 + PERM_ROWS], preferred_element_type=F32).astype(o_ref.dtype)
        j0 = (row0 + r0) // DFT_RADIX
        for k in range(o_ref.shape[0]):
            for s1 in range(DFT_RADIX):
                o_ref[k, s1, j0:j0 + DFT_RADIX, :] = (
                    p[s1 * DFT_RADIX:(s1 + 1) * DFT_RADIX, k * slab:(k + 1) * slab])


def _rm_shape(bsz, s, d):
    slab = min(FOURIER_STEP_COLS, d)
    return (bsz, d // slab, DFT_RADIX, s // DFT_RADIX, slab)


def _rm_out_spec(tm, d):
    slab = min(FOURIER_STEP_COLS, d)
    return pl.BlockSpec((None, d // slab, DFT_RADIX, tm // DFT_RADIX, slab),
                        lambda b, i: (b, 0, 0, i, 0))


def _prenorm_kernel(x_ref, g_ref, sh_ref, sc_ref, *rest, residue_major):
    h = _norm_mod(x_ref[...], g_ref[...], sh_ref[...], sc_ref[...])
    if residue_major:
        p_ref, o_ref = rest
        _store_residue_major(h, p_ref, o_ref)
    else:
        o_ref, = rest
        o_ref[...] = h.astype(o_ref.dtype)


def _prenorm(x, g, shift, scale, *, residue_major, out_dtype, tm=ROW_TILE):
    bsz, s, d = x.shape
    tm = min(tm, s)
    vec = lambda a: a.reshape(bsz, 1, d)
    in_specs = [
        pl.BlockSpec((None, tm, d), lambda b, i: (b, i, 0)),
        pl.BlockSpec((1, d), lambda b, i: (0, 0)),
        pl.BlockSpec((None, 1, d), lambda b, i: (b, 0, 0)),
        pl.BlockSpec((None, 1, d), lambda b, i: (b, 0, 0)),
    ]
    args = [x, g.reshape(1, d), vec(shift), vec(scale)]
    if residue_major:
        assert tm % PERM_ROWS == 0
        in_specs.append(_resident((PERM_ROWS, PERM_ROWS), lambda b, i: (0, 0)))
        args.append(_residue_perm())
        out_shape = jax.ShapeDtypeStruct(_rm_shape(bsz, s, d), out_dtype)
        out_spec = _rm_out_spec(tm, d)
    else:
        out_shape = jax.ShapeDtypeStruct((bsz, s, d), out_dtype)
        out_spec = pl.BlockSpec((None, tm, d), lambda b, i: (b, i, 0))
    return pl.pallas_call(
        functools.partial(_prenorm_kernel, residue_major=residue_major),
        grid=(bsz, s // tm),
        in_specs=in_specs,
        out_specs=out_spec,
        out_shape=out_shape,
        compiler_params=_params(2),
        name="prenorm_rm" if residue_major else "prenorm",
    )(*args)


def _dft4(x):
    (x0r, x0i), (x1r, x1i), (x2r, x2i), (x3r, x3i) = x
    t0r, t0i = x0r + x2r, x0i + x2i
    t1r, t1i = x0r - x2r, x0i - x2i
    t2r, t2i = x1r + x3r, x1i + x3i
    t3r, t3i = x1r - x3r, x1i - x3i
    return [(t0r + t2r, t0i + t2i),
            (t1r + t3i, t1i - t3r),
            (t0r - t2r, t0i - t2i),
            (t1r - t3i, t1i + t3r)]


def _twiddle16(z, p):
    zr, zi = z
    p = p % 16
    if p == 0:
        return zr, zi
    if p == 4:
        return zi, -zr
    if p == 8:
        return -zr, -zi
    if p == 12:
        return -zi, zr
    if p % 4 == 2:
        c = math.sqrt(0.5)
        sr = -c if p in (6, 10) else c
        si = -c if p in (2, 6) else c
        add, sub = zr + zi, zr - zi
        real = add * sr if sr == -si else sub * sr
        imag = sub * si if sr == -si else add * si
        return real, imag
    cr = math.cos(2.0 * math.pi * p / 16.0)
    ci = -math.sin(2.0 * math.pi * p / 16.0)
    return zr * cr - zi * ci, zr * ci + zi * cr


DFT_SLAB = 256


def _fourier2d_kernel(h_ref, w_ref, cs_ref, f_ref, *scratch):
    n2 = h_ref.shape[1]
    td = h_ref.shape[2]
    rows = 8
    slabs = [(c0, scratch[2 * i], scratch[2 * i + 1])
             for i, c0 in enumerate(range(0, td, DFT_SLAB))]
    for c0, br_ref, bi_ref in slabs:
        for s1 in range(DFT_RADIX):
            p = jnp.dot(w_ref[s1], h_ref[s1, :, c0:c0 + DFT_SLAB], preferred_element_type=F32)
            br_ref[s1] = p[:n2]
            bi_ref[s1] = p[n2:]

    quarter = max(n2 // 4, rows)
    for c0, br_ref, bi_ref in slabs:
        for q0 in range(0, n2, quarter):
            for r in range(q0, q0 + quarter, rows):
                rs = slice(r, r + rows)
                for l0 in range(0, DFT_SLAB, LANES):
                    ls = slice(l0, l0 + LANES)
                    us = [None] * DFT_RADIX
                    for b in range(4):
                        xs = [(br_ref[4 * a + b, rs, ls], bi_ref[4 * a + b, rs, ls])
                              for a in range(4)]
                        ub = _dft4(xs)
                        for c in range(4):
                            us[4 * c + b] = _twiddle16(ub[c], b * c)
                    for c in range(4):
                        ys = _dft4(us[4 * c:4 * c + 4])
                        for d in range(4):
                            br_ref[c + 4 * d, rs, ls] = ys[d][0]
                            bi_ref[c + 4 * d, rs, ls] = ys[d][1]
            qs = slice(q0, q0 + quarter)
            yq = jnp.concatenate(
                [br_ref[:, qs, :].reshape(DFT_RADIX * quarter, DFT_SLAB).astype(BF16),
                 bi_ref[:, qs, :].reshape(DFT_RADIX * quarter, DFT_SLAB).astype(BF16)], axis=1)
            fq = jnp.dot(yq, cs_ref[...], preferred_element_type=F32).astype(f_ref.dtype)
            for k1 in range(DFT_RADIX):
                f_ref[k1 * n2 + q0:k1 * n2 + q0 + quarter, c0:c0 + DFT_SLAB] = (
                    fq[k1 * quarter:(k1 + 1) * quarter])


def _seqdft_tables(s):
    n2 = s // DFT_RADIX
    k2 = jnp.arange(n2, dtype=jnp.int32)
    s1 = jnp.arange(DFT_RADIX, dtype=jnp.int32)
    unit = 2.0 * jnp.pi / s
    a = ((k2[:, None] * (DFT_RADIX * k2[None, :])) % s).astype(F32) * unit
    b = ((s1[:, None] * k2[None, :]) % s).astype(F32) * unit
    ca, sa = jnp.cos(a)[None], jnp.sin(a)[None]
    cb, sb = jnp.cos(b)[:, :, None], jnp.sin(b)[:, :, None]
    wr = ca * cb - sa * sb
    wi = -(sa * cb + ca * sb)
    return jnp.concatenate([wr, wi], axis=1).astype(BF16)


def _chan_dft_table(gd, s):
    c = jnp.arange(gd, dtype=jnp.int32)
    j = (c[:, None] * c[None, :]) % gd
    ang = j.astype(F32) * (2.0 * jnp.pi / gd)
    scale = 1.0 / jnp.sqrt(jnp.asarray(s * gd, F32))
    eye = jnp.eye(DFT_SLAB // gd, dtype=F32)
    blocks = [jnp.kron(eye, t * scale) for t in (jnp.cos(ang), jnp.sin(ang))]
    return jnp.concatenate(blocks, axis=0).astype(BF16)


def _fourier2d(h_rm, w_tab, cs_tab):
    bsz, n_steps, _, n2, td = h_rm.shape
    s = DFT_RADIX * n2
    slab_scratch = [pltpu.VMEM((DFT_RADIX, n2, DFT_SLAB), F32),
                    pltpu.VMEM((DFT_RADIX, n2, DFT_SLAB), F32)]
    return pl.pallas_call(
        _fourier2d_kernel,
        grid=(bsz, n_steps),
        in_specs=[
            pl.BlockSpec((None, None, DFT_RADIX, n2, td), lambda b, j: (b, j, 0, 0, 0)),
            _resident((DFT_RADIX, 2 * n2, n2), lambda b, j: (0, 0, 0)),
            _resident((2 * DFT_SLAB, DFT_SLAB), lambda b, j: (0, 0)),
        ],
        out_specs=pl.BlockSpec((None, None, s, td), lambda b, j: (b, j, 0, 0)),
        out_shape=jax.ShapeDtypeStruct((bsz, n_steps, s, td), BF16),
        scratch_shapes=slab_scratch * (td // DFT_SLAB),
        compiler_params=_params(2),
        name="fourier2d",
    )(h_rm, w_tab, cs_tab)


def _retproj_kernel(h_ref, w_ref, cos_ref, sin_ref, qdf_ref, qdb_ref,
                    q_ref, qfb_ref, kt_ref, v_ref, sg_ref, wkt_ref, *, dk, chunk):
    h = h_ref[...]
    half = dk // 2
    tm = h.shape[0]
    hdk = RET_HEADS * dk
    dv = v_ref.shape[2]
    dvt = RET_HEADS * dv

    @pl.when((pl.program_id(0) == 0) & (pl.program_id(1) == 0))
    def _():
        for c0 in range(0, hdk, dk):
            wk = w_ref[:, hdk + c0:hdk + c0 + dk].astype(F32)
            wkt_ref[c0:c0 + dk, :] = wk.T.astype(wkt_ref.dtype)

    cos, sin = cos_ref[...], sin_ref[...]
    cost, sint = cos.T, sin.T
    k_scale = dk ** -0.5
    for hd in range(RET_HEADS):
        c0 = hd * dk
        q = jnp.dot(h, w_ref[:, c0:c0 + dk], preferred_element_type=F32)
        q1, q2 = q[:, :half], q[:, half:]
        o1, o2 = q1 * cos - q2 * sin, q1 * sin + q2 * cos
        q_ref[hd, :, :half] = o1.astype(q_ref.dtype)
        q_ref[hd, :, half:] = o2.astype(q_ref.dtype)
        for k, d_ref in enumerate((qdf_ref, qdb_ref)):
            dec = jnp.tile(d_ref[hd], (tm // chunk, max(1, half // LANES)))[:, :half]
            qfb_ref[hd, :, k * dk:k * dk + half] = (o1 * dec).astype(qfb_ref.dtype)
            qfb_ref[hd, :, k * dk + half:(k + 1) * dk] = (o2 * dec).astype(qfb_ref.dtype)
        kt = lax.dot_general(wkt_ref[c0:c0 + dk, :], h, (((1,), (1,)), ((), ())),
                             preferred_element_type=F32)
        k1, k2 = kt[:half], kt[half:]
        r1 = (k1 * cost - k2 * sint) * k_scale
        r2 = (k1 * sint + k2 * cost) * k_scale
        for j in range(tm // chunk):
            kt_ref[j, c0:c0 + half, :] = r1[:, j * chunk:(j + 1) * chunk].astype(kt_ref.dtype)
            kt_ref[j, c0 + half:c0 + dk, :] = r2[:, j * chunk:(j + 1) * chunk].astype(kt_ref.dtype)
    step = min(256, dv)
    v0, g0 = 2 * hdk, 2 * hdk + dvt
    for hd in range(RET_HEADS):
        for o in range(0, dv, step):
            c0 = hd * dv + o
            v_ref[hd, :, o:o + step] = jnp.dot(h, w_ref[:, v0 + c0:v0 + c0 + step],
                                               preferred_element_type=F32).astype(v_ref.dtype)
            g = jnp.dot(h, w_ref[:, g0 + c0:g0 + c0 + step], preferred_element_type=F32)
            sg_ref[hd, :, o:o + step] = _silu(g).astype(sg_ref.dtype)


def _retproj(h, w, cos, sin, qdf, qdb, chunk, tm=ROW_TILE):
    bsz, s, d = h.shape
    hdk = d
    hdv = (w.shape[1] - 2 * hdk) // 2
    dk = hdk // RET_HEADS
    dv = hdv // RET_HEADS
    half = dk // 2
    tm = min(tm, s)
    row = lambda b, i: (b, i, 0)
    head_row = lambda b, i: (b, 0, i, 0)
    return pl.pallas_call(
        functools.partial(_retproj_kernel, dk=dk, chunk=chunk),
        grid=(bsz, s // tm),
        in_specs=[
            pl.BlockSpec((None, tm, d), row),
            _resident(w.shape, lambda b, i: (0, 0)),
            pl.BlockSpec((tm, half), lambda b, i: (i, 0)),
            pl.BlockSpec((tm, half), lambda b, i: (i, 0)),
            _resident(qdf.shape, lambda b, i: (0, 0, 0)),
            _resident(qdb.shape, lambda b, i: (0, 0, 0)),
        ],
        out_specs=[
            pl.BlockSpec((None, RET_HEADS, tm, dk), head_row),
            pl.BlockSpec((None, RET_HEADS, tm, 2 * dk), head_row),
            pl.BlockSpec((None, tm // chunk, hdk, chunk), lambda b, i: (b, i, 0, 0)),
            pl.BlockSpec((None, RET_HEADS, tm, dv), head_row),
            pl.BlockSpec((None, RET_HEADS, tm, dv), head_row),
        ],
        out_shape=[
            jax.ShapeDtypeStruct((bsz, RET_HEADS, s, dk), BF16),
            jax.ShapeDtypeStruct((bsz, RET_HEADS, s, 2 * dk), BF16),
            jax.ShapeDtypeStruct((bsz, s // chunk, hdk, chunk), BF16),
            jax.ShapeDtypeStruct((bsz, RET_HEADS, s, dv), BF16),
            jax.ShapeDtypeStruct((bsz, RET_HEADS, s, dv), BF16),
        ],
        scratch_shapes=[pltpu.VMEM((hdk, d), BF16)],
        compiler_params=_params(2),
        name="ret_proj",
    )(h, w, cos, sin, qdf, qdb)


def _retcore_kernel(cd_ref, q_ref, qfb_ref, kt_ref, v_ref, dm_ref, kdf_ref, kdb_ref,
                    y_ref, r_all_ref, rf_ref, rb_ref):
    hd = pl.program_id(1)
    nc, dk, c = kt_ref.shape
    cd_f = cd_ref[0, hd]
    cd_b = cd_ref[1, hd]
    kdf = kdf_ref[...]
    kdb = kdb_ref[...]

    rf_ref[...] = jnp.zeros_like(rf_ref)
    rb_ref[...] = jnp.zeros_like(rb_ref)
    for it in range(nc):
        nf, nb = it, nc - 1 - it
        r_all_ref[nf, :dk, :] = rf_ref[...].astype(r_all_ref.dtype)
        r_all_ref[nb, dk:, :] = rb_ref[...].astype(r_all_ref.dtype)
        if it == nc - 1:
            break
        ktf = (kt_ref[nf].astype(F32) * kdf).astype(BF16)
        ktb = (kt_ref[nb].astype(F32) * kdb).astype(BF16)
        rf_ref[...] = rf_ref[...] * cd_f + jnp.dot(
            ktf, v_ref[nf * c:(nf + 1) * c, :], preferred_element_type=F32)
        rb_ref[...] = rb_ref[...] * cd_b + jnp.dot(
            ktb, v_ref[nb * c:(nb + 1) * c, :], preferred_element_type=F32)

    for n in range(nc):
        rows = slice(n * c, (n + 1) * c)
        sc = jnp.dot(q_ref[rows, :], kt_ref[n], preferred_element_type=F32) * dm_ref[...]
        y = jnp.dot(sc.astype(BF16), v_ref[rows, :], preferred_element_type=F32)
        y += jnp.dot(qfb_ref[rows, :], r_all_ref[n], preferred_element_type=F32)
        mu = jnp.mean(y, axis=-1, keepdims=True)
        yc = y - mu
        var = jnp.mean(yc * yc, axis=-1, keepdims=True)
        y_ref[rows, :] = (yc * lax.rsqrt(var + EPS)).astype(y_ref.dtype)


def _ret_decay_tables(c):
    hidx = jnp.arange(RET_HEADS, dtype=F32)
    lg_f = jnp.log1p(-jnp.exp2(-5.0 - hidx))
    lg_b = jnp.flip(lg_f)
    t = jnp.arange(c, dtype=jnp.int32)
    diff = (t[:, None] - t[None, :])
    fmask = diff >= 0
    bmask = diff < 0
    ef = jnp.exp(jnp.where(fmask, diff, 0).astype(F32)[None] * lg_f[:, None, None])
    eb = jnp.exp(jnp.where(bmask, -diff, 0).astype(F32)[None] * lg_b[:, None, None])
    dm = jnp.where(fmask[None], ef, eb)
    tf = t.astype(F32)
    ones = jnp.ones((1, 1, LANES), F32)
    qdf = jnp.exp((tf + 1.0)[None, :] * lg_f[:, None])[:, :, None] * ones
    qdb = jnp.exp((c - tf)[None, :] * lg_b[:, None])[:, :, None] * ones
    kdf = jnp.exp((c - 1.0 - tf)[None, :] * lg_f[:, None])[:, None, :]
    kdb = jnp.exp(tf[None, :] * lg_b[:, None])[:, None, :]
    cd = jnp.stack([jnp.exp(c * lg_f), jnp.exp(c * lg_b)])
    return cd, dm, qdf, qdb, kdf, kdb


def _retcore(q, qfb, kt, v, tables):
    bsz, _, s, dk = q.shape
    dv = v.shape[3]
    nc, c = kt.shape[1], kt.shape[3]
    cd, dm, _, _, kdf, kdb = tables
    per_head = lambda b, h: (h, 0, 0)
    head_block = lambda b, h: (b, h, 0, 0)
    return pl.pallas_call(
        _retcore_kernel,
        grid=(bsz, RET_HEADS),
        in_specs=[
            pl.BlockSpec(memory_space=pltpu.SMEM),
            pl.BlockSpec((None, None, s, dk), head_block),
            pl.BlockSpec((None, None, s, 2 * dk), head_block),
            pl.BlockSpec((None, nc, dk, c), lambda b, h: (b, 0, h, 0)),
            pl.BlockSpec((None, None, s, dv), head_block),
            pl.BlockSpec((None, c, c), per_head),
            pl.BlockSpec((None, 1, c), per_head),
            pl.BlockSpec((None, 1, c), per_head),
        ],
        out_specs=pl.BlockSpec((None, None, s, dv), head_block),
        out_shape=jax.ShapeDtypeStruct((bsz, RET_HEADS, s, dv), BF16),
        scratch_shapes=[pltpu.VMEM((nc, 2 * dk, dv), BF16),
                        pltpu.VMEM((dk, dv), F32),
                        pltpu.VMEM((dk, dv), F32)],
        compiler_params=_params(2),
        name="ret_core",
    )(cd, q, qfb, kt, v, dm, kdf, kdb)


def _mix_ffn_kernel(*refs, fourier, emit, cast_reps):
    n_cast = len(cast_reps)
    n_out = (1 if emit == "final" else 2) + n_cast
    ins = refs[:len(refs) - n_out - 2]
    outs = refs[len(ins):len(ins) + n_out]
    h2_ref, a_ref = refs[len(ins) + n_out:]
    o_ref = outs[0]
    flat_step = pl.program_id(0) * pl.num_programs(1) + pl.program_id(1)
    for src_ref, dst_ref, rep in zip(ins[len(ins) - n_cast:], outs[len(outs) - n_cast:], cast_reps):
        if rep == 1:
            dst_ref[...] = src_ref[...].astype(dst_ref.dtype)
        else:
            @pl.when(flat_step % rep == 0)
            def _(src_ref=src_ref, dst_ref=dst_ref):
                dst_ref[...] = src_ref[...].astype(dst_ref.dtype)
    ins = ins[:len(ins) - n_cast]
    if emit == "x+h_rm":
        ins, perm_ref = ins[:-1], ins[-1]
    x_ref, vec_ref, gain_ref = ins[:3]
    g1, sh2, sc2, g2 = (vec_ref[r:r + 1, :] for r in range(4))
    if fourier:
        f_ref, wo_ref, win_ref, wout_ref = ins[3:]
    else:
        y_ref, sg_ref, wo_ref, win_ref, wout_ref = ins[3:]
    f = wout_ref.shape[0]
    tm = x_ref.shape[0]
    sub = tm // MIX_SUBTILES
    subtiles = [slice(r0, r0 + sub) for r0 in range(0, tm, sub)]
    ms = []
    for rs in subtiles:
        if fourier:
            z = jnp.concatenate([f_ref[k, rs, :] for k in range(f_ref.shape[0])], axis=1)
        else:
            z = jnp.concatenate(
                [sg_ref[k, rs, :] * y_ref[k, rs, :] for k in range(y_ref.shape[0])], axis=1)
        ms.append(jnp.dot(z, wo_ref[...], preferred_element_type=F32))
    for rs, m in zip(subtiles, ms):
        x1 = x_ref[rs, :] + g1 * m
        h2_ref[rs, :] = _norm_mod(x1, gain_ref[0:1, :], sh2, sc2).astype(h2_ref.dtype)
        o_ref[rs, :] = x1
    for c0 in range(0, f, FFN_CHUNK):
        gus = [(jnp.dot(h2_ref[rs, :], win_ref[:, c0:c0 + FFN_CHUNK], preferred_element_type=F32),
                jnp.dot(h2_ref[rs, :], win_ref[:, f + c0:f + c0 + FFN_CHUNK],
                        preferred_element_type=F32)) for rs in subtiles]
        for rs, (gate, up) in zip(subtiles, gus):
            a_ref[rs, c0:c0 + FFN_CHUNK] = (_silu(gate) * up).astype(a_ref.dtype)
    ffns = [jnp.dot(a_ref[rs, :], wout_ref[...], preferred_element_type=F32) for rs in subtiles]
    for rs, ffn in zip(subtiles, ffns):
        x2 = o_ref[rs, :] + g2 * ffn
        hn = _norm_mod(x2, gain_ref[1:2, :], vec_ref[4:5, :], vec_ref[5:6, :])
        if emit == "final":
            o_ref[rs, :] = hn
            continue
        o_ref[rs, :] = x2
        hn_ref = outs[1]
        if emit == "x+h":
            hn_ref[rs, :] = hn.astype(hn_ref.dtype)
        else:
            _store_residue_major(hn, perm_ref, hn_ref, rs.start)


def _cast_rows(rows, n_steps):
    n = n_steps
    while n > 1 and (rows % n or (rows // n) % 16):
        n //= 2
    return rows // n


def _mix_ffn(x, vecs, gains, mix_inputs, wo, win, wout, *, fourier, emit, cast_jobs=(),
             tm=ROW_TILE):
    bsz, s, d = x.shape
    tm = min(tm, s)
    n_tiles = s // tm
    row = lambda b, i: (b, i, 0)
    in_specs = [pl.BlockSpec((None, tm, d), row),
                pl.BlockSpec((None, 8, d), lambda b, i: (b, 0, 0)),
                pl.BlockSpec((8, d), lambda b, i: (0, 0))]
    args = [x, vecs, gains]
    for a in mix_inputs:
        in_specs.append(pl.BlockSpec((None, a.shape[1], tm, a.shape[3]), lambda b, i: (b, 0, i, 0)))
        args.append(a)
    for a in (wo, win, wout):
        in_specs.append(_resident(a.shape, lambda b, i: (0, 0)))
        args.append(a)
    out_specs = [pl.BlockSpec((None, tm, d), row)]
    out_shape = [jax.ShapeDtypeStruct((bsz, s, d), F32)]
    scratch = [pltpu.VMEM((tm, d), BF16), pltpu.VMEM((tm, wout.shape[0]), BF16)]
    if emit == "x+h":
        out_specs.append(pl.BlockSpec((None, tm, d), row))
        out_shape.append(jax.ShapeDtypeStruct((bsz, s, d), BF16))
    elif emit == "x+h_rm":
        out_specs.append(_rm_out_spec(tm, d))
        out_shape.append(jax.ShapeDtypeStruct(_rm_shape(bsz, s, d), BF16))
        assert tm % PERM_ROWS == 0
        in_specs.append(_resident((PERM_ROWS, PERM_ROWS), lambda b, i: (0, 0)))
        args.append(_residue_perm())
    cast_reps = []
    for stack, layer in cast_jobs:
        _, rows, cols = stack.shape
        rp = _cast_rows(rows, bsz * n_tiles)
        rep = bsz * n_tiles // (rows // rp)
        cast_reps.append(rep)
        in_specs.append(pl.BlockSpec(
            (None, rp, cols), lambda b, i, layer=layer, rep=rep: (layer, (b * n_tiles + i) // rep, 0)))
        args.append(stack)
        out_specs.append(pl.BlockSpec((rp, cols), lambda b, i, rep=rep: ((b * n_tiles + i) // rep, 0)))
        out_shape.append(jax.ShapeDtypeStruct((rows, cols), BF16))
    return pl.pallas_call(
        functools.partial(_mix_ffn_kernel, fourier=fourier, emit=emit, cast_reps=tuple(cast_reps)),
        grid=(bsz, s // tm),
        in_specs=in_specs,
        out_specs=out_specs,
        out_shape=out_shape,
        scratch_shapes=scratch,
        compiler_params=_params(2),
        name="mix_ffn_fourier" if fourier else "mix_ffn_ret",
    )(*args)


def _rotary_tables(s, half):
    inv_freq = ROPE_BASE ** (-jnp.arange(half, dtype=F32) / half)
    ang = jnp.arange(s, dtype=F32)[:, None] * inv_freq[None, :]
    return jnp.cos(ang), jnp.sin(ang)


def kernel(x, c, w_ada, b_ada, norm_mix_g, norm_ffn_g, w_fourier_out, w_ret_in, w_ret_out,
           w_ffn_in, w_ffn_out, final_norm_g, w_ada_final, b_ada_final):
    bsz, s, d = x.shape
    depth = w_ada.shape[0]
    gd = d // FOURIER_GROUPS
    dk = d // RET_HEADS
    chunk = min(RET_CHUNK, s)

    mods = _ada(c, w_ada, b_ada).reshape(depth, bsz, 6, d)
    mod_final = _ada(c, w_ada_final[None], b_ada_final[None]).reshape(bsz, 2, d)

    seq_tab = _seqdft_tables(s)
    cs_tab = _chan_dft_table(gd, s)
    cos, sin = _rotary_tables(s, dk // 2)
    ret_tabs = _ret_decay_tables(chunk)

    def is_fourier(i):
        return i % N_MIXERS == 0

    def mixer_weights(i):
        j = i // N_MIXERS
        return [(w_fourier_out, j)] if is_fourier(i) else [(w_ret_in, j), (w_ret_out, j)]

    w_mix = [stack[j].astype(BF16) for stack, j in mixer_weights(0)]
    w_in, w_out = w_ffn_in[0].astype(BF16), w_ffn_out[0].astype(BF16)

    h = _prenorm(x, norm_mix_g[0], mods[0, :, 0], mods[0, :, 1], residue_major=is_fourier(0),
                 out_dtype=BF16, tm=4 * ROW_TILE)
    for i in range(depth):
        last = i == depth - 1
        if last:
            next_mod, next_gain, emit = mod_final, final_norm_g, "final"
        else:
            next_mod, next_gain = mods[i + 1, :, 0:2], norm_mix_g[i + 1]
            emit = "x+h_rm" if is_fourier(i + 1) else "x+h"
        pad = jnp.zeros((bsz, 2, d), F32)
        vecs = jnp.concatenate([mods[i, :, 2:6], next_mod, pad], axis=1)
        gains = jnp.concatenate([norm_ffn_g[i][None], next_gain[None], jnp.zeros((6, d), F32)])
        jobs = () if last else tuple(
            mixer_weights(i + 1) + [(w_ffn_in, i + 1), (w_ffn_out, i + 1)])
        if is_fourier(i):
            mix_in = (_fourier2d(h, seq_tab, cs_tab),)
        else:
            q, qfb, kt, v, sg = _retproj(h, w_mix[0], cos, sin, ret_tabs[2], ret_tabs[3], chunk)
            mix_in = (_retcore(q, qfb, kt, v, ret_tabs), sg)
        outs = _mix_ffn(x, vecs, gains, mix_in, w_mix[-1], w_in, w_out,
                        fourier=is_fourier(i), emit=emit, cast_jobs=jobs)
        if last:
            return outs[0]
        x, h = outs[:2]
        *w_mix, w_in, w_out = outs[2:]
```
